```python
import jax, jax.numpy as jnp
from jax import lax
import numpy as np

D_MODEL = 1024
BATCH = 4
SEQ = 4096
DEPTH = 4

CHUNK = 64
N_EVEN = (DEPTH + 1) // 2
N_ODD = DEPTH // 2
ALPHA = (2 * DEPTH) ** 0.25
BETA = (8 * DEPTH) ** -0.25
LN_EPS = 1e-5
NEG = -1e30

POOL_WIDTH = D_MODEL // 2
POOL_WINDOWS = (2, 4, 8, 16)
POOL_GROUP = POOL_WIDTH // len(POOL_WINDOWS)
ML_HEADS = 4
ML_DK = (D_MODEL // 2) // ML_HEADS
ML_DV = (D_MODEL // 2) // ML_HEADS
ML_QK = ML_HEADS * ML_DK
ML_WIDTH = ML_HEADS * ML_DV
CONV_K = 4
EVEN_IN = POOL_WIDTH + 2 * ML_QK + 2 * ML_WIDTH + 2 * ML_HEADS
EVEN_MIX = POOL_WIDTH + ML_WIDTH
GLA_HEADS = 4
GLA_DK = (D_MODEL // 2) // GLA_HEADS
GLA_DV = D_MODEL // GLA_HEADS
GLA_QK = GLA_HEADS * GLA_DK
GLA_V = GLA_HEADS * GLA_DV
GLA_RANK = 16
GLA_TAU = 16.0
GLA_CHUNK = 16
ODD_IN = 2 * GLA_QK + 2 * GLA_V + GLA_RANK
N_EXPERTS = 32
TOP_K = 4
D_EXPERT = D_MODEL
SWIGLU_LIMIT = 7.0
SWIGLU_ALPHA = 1.702

kernel_name = "hybrid_pool_mlstm_gla_moe_deepnorm"


def layer_norm(x, g, b):
    xf = x.astype(jnp.float32)
    mu = xf.mean(-1, keepdims=True)
    var = jnp.square(xf - mu).mean(-1, keepdims=True)
    return ((xf - mu) * lax.rsqrt(var + LN_EPS) * g + b).astype(x.dtype)


def head_norm(h, g):
    mu = h.mean(-1, keepdims=True)
    var = jnp.square(h - mu).mean(-1, keepdims=True)
    hn = (h - mu) * lax.rsqrt(var + LN_EPS)
    return hn.reshape(h.shape[:2] + (-1,)) * g


def to_chunks(t, L):
    B_, S_ = t.shape[:2]
    t = t.astype(jnp.float32).reshape((B_, S_ // L, L) + t.shape[2:])
    return jnp.swapaxes(jnp.moveaxis(t, 1, 0), 2, 3)


def from_chunks(t):
    t = jnp.moveaxis(jnp.swapaxes(t, 2, 3), 0, 1)
    return t.reshape((t.shape[0], t.shape[1] * t.shape[2]) + t.shape[3:])


def pool_mixer(u, w_pool, pool_scale):
    B_, S_, _ = u.shape
    uf = u.astype(jnp.float32).reshape(B_, S_, len(POOL_WINDOWS), POOL_GROUP)
    cs = jnp.cumsum(uf, axis=1)
    pos = jnp.arange(1, S_ + 1, dtype=jnp.float32)
    means = []
    for gi, w in enumerate(POOL_WINDOWS):
        c = cs[:, :, gi]
        lag = jnp.pad(c, ((0, 0), (w, 0), (0, 0)))[:, :S_]
        cnt = jnp.minimum(pos, float(w))[:, None]
        means.append((c - lag) / cnt)
    d = (jnp.stack(means, axis=2) - uf).astype(u.dtype)
    y = jnp.einsum('bsgc,gcd->bsgd', d, w_pool).reshape(B_, S_, POOL_WIDTH)
    return y * pool_scale


def causal_dwconv(x, w, b):
    C = x.shape[-1]
    y = lax.conv_general_dilated(x, w[:, None, :], window_strides=(1,),
                                 padding=((CONV_K - 1, 0),),
                                 dimension_numbers=('NWC', 'WIO', 'NWC'),
                                 feature_group_count=C)
    return y + b


def mlstm(q, k, v, ig, fg):
    B_ = q.shape[0]
    k = k * (ML_DK ** -0.5)
    lf = jax.nn.log_sigmoid(fg.astype(jnp.float32))
    xs = (to_chunks(q, CHUNK), to_chunks(k, CHUNK), to_chunks(v, CHUNK),
          to_chunks(ig, CHUNK), to_chunks(lf, CHUNK))
    mask = jnp.tril(jnp.ones((CHUNK, CHUNK), dtype=bool))

    def step(carry, inp):
        C, n, m = carry
        qc, kc, vc, ic, lfc = inp
        b = jnp.cumsum(lfc, axis=-1)
        g = b[..., -1]
        D = jnp.where(mask, b[..., :, None] - b[..., None, :] + ic[..., None, :], NEG)
        m_inter = b + m[..., None]
        m_t = jnp.maximum(m_inter, D.max(-1))
        Wts = jnp.exp(D - m_t[..., None])
        Sts = jnp.einsum('bhtd,bhsd->bhts', qc, kc) * Wts
        sc = jnp.exp(m_inter - m_t)
        num = jnp.einsum('bhts,bhsv->bhtv', Sts, vc) + sc[..., None] * jnp.einsum('bhtd,bhdv->bhtv', qc, C)
        den = Sts.sum(-1) + sc * jnp.einsum('bhtd,bhd->bht', qc, n)
        h = num / jnp.maximum(jnp.abs(den), jnp.exp(-m_t))[..., None]
        a = g[..., None] - b + ic
        m_new = jnp.maximum(g + m, a.max(-1))
        decay = jnp.exp(g + m - m_new)
        wk = jnp.exp(a - m_new[..., None])[..., None] * kc
        C_new = decay[..., None, None] * C + jnp.einsum('bhsd,bhsv->bhdv', wk, vc)
        n_new = decay[..., None] * n + wk.sum(-2)
        return (C_new, n_new, m_new), h

    init = (jnp.zeros((B_, ML_HEADS, ML_DK, ML_DV), jnp.float32),
            jnp.zeros((B_, ML_HEADS, ML_DK), jnp.float32),
            jnp.zeros((B_, ML_HEADS), jnp.float32))
    _, hs = lax.scan(step, init, xs)
    return from_chunks(hs)


def gla(q, k, v, lg):
    B_ = q.shape[0]
    q = q * (GLA_DK ** -0.5)
    xs = (to_chunks(q, GLA_CHUNK), to_chunks(k, GLA_CHUNK),
          to_chunks(v, GLA_CHUNK), to_chunks(lg, GLA_CHUNK))
    mask = jnp.tril(jnp.ones((GLA_CHUNK, GLA_CHUNK), dtype=bool))[..., None]

    def step(S, inp):
        qc, kc, vc, gc = inp
        Bc = jnp.cumsum(gc, axis=-2)
        E = jnp.where(mask, Bc[..., :, None, :] - Bc[..., None, :, :], NEG)
        A = jnp.einsum('bhtd,bhsd,bhtsd->bhts', qc, kc, jnp.exp(E))
        o = jnp.einsum('bhts,bhsv->bhtv', A, vc) + jnp.einsum('bhtd,bhdv->bhtv', qc * jnp.exp(Bc), S)
        Bl = Bc[..., -1:, :]
        S_new = jnp.exp(Bl[..., 0, :])[..., None] * S + jnp.einsum('bhsd,bhsv->bhdv', kc * jnp.exp(Bl - Bc), vc)
        return S_new, o

    _, os_ = lax.scan(step, jnp.zeros((B_, GLA_HEADS, GLA_DK, GLA_DV), jnp.float32), xs)
    return from_chunks(os_)


def even_mixer(x, w_in, w_pool, pool_scale, conv_w, conv_b, i_bias, f_bias, ml_norm, w_out):
    B_, S_, _ = x.shape
    p = x @ w_in
    o1 = POOL_WIDTH
    o2 = o1 + 2 * ML_QK
    o3 = o2 + ML_WIDTH
    o4 = o3 + ML_WIDTH
    u, qk, vv, og, gates = jnp.split(p, [o1, o2, o3, o4], axis=-1)
    y_pool = pool_mixer(u, w_pool, pool_scale)
    qk = jax.nn.silu(causal_dwconv(qk, conv_w, conv_b))
    q, k = jnp.split(qk, 2, axis=-1)
    q = q.reshape(B_, S_, ML_HEADS, ML_DK)
    k = k.reshape(B_, S_, ML_HEADS, ML_DK)
    vv = vv.reshape(B_, S_, ML_HEADS, ML_DV)
    ig = gates[..., :ML_HEADS] + i_bias
    fg = gates[..., ML_HEADS:] + f_bias
    h = mlstm(q, k, vv, ig, fg)
    y_ml = (head_norm(h, ml_norm) * jax.nn.sigmoid(og.astype(jnp.float32))).astype(x.dtype)
    return jnp.concatenate([y_pool, y_ml], axis=-1) @ w_out


def odd_mixer(x, w_in, gla_w2, gla_b, gla_norm, w_out):
    B_, S_, _ = x.shape
    p = x @ w_in
    q, k, v, r, glr = jnp.split(p, [GLA_QK, 2 * GLA_QK, 2 * GLA_QK + GLA_V, 2 * GLA_QK + 2 * GLA_V], axis=-1)
    lg = jax.nn.log_sigmoid((glr @ gla_w2 + gla_b).astype(jnp.float32)) / GLA_TAU
    o = gla(q.reshape(B_, S_, GLA_HEADS, GLA_DK), k.reshape(B_, S_, GLA_HEADS, GLA_DK),
            v.reshape(B_, S_, GLA_HEADS, GLA_DV), lg.reshape(B_, S_, GLA_HEADS, GLA_DK))
    o = head_norm(o, gla_norm) * jax.nn.silu(r.astype(jnp.float32))
    return o.astype(x.dtype) @ w_out


def moe(x, router_w, router_b, w_gu, b_gu, w_down, b_down):
    B_, S_, D_ = x.shape
    xt = x.reshape(-1, D_)
    T = xt.shape[0]
    logits = (xt @ router_w + router_b).astype(jnp.float32)
    top_v, top_e = lax.top_k(logits, TOP_K)
    gate = jax.nn.softmax(top_v, axis=-1)
    flat_e = top_e.reshape(-1)
    order = jnp.argsort(flat_e)
    e_sorted = flat_e[order]
    tok = order // TOP_K
    sizes = jnp.bincount(flat_e, length=N_EXPERTS).astype(jnp.int32)
    xs = xt[tok]
    h = lax.ragged_dot(xs, w_gu, sizes) + b_gu[e_sorted]
    glu = jnp.minimum(h[:, 0::2], SWIGLU_LIMIT)
    lin = jnp.clip(h[:, 1::2], -SWIGLU_LIMIT, SWIGLU_LIMIT)
    act = glu * jax.nn.sigmoid(SWIGLU_ALPHA * glu) * (lin + 1.0)
    y = lax.ragged_dot(act, w_down, sizes) + b_down[e_sorted]
    y = y * gate.reshape(-1)[order][:, None].astype(y.dtype)
    out = jax.ops.segment_sum(y, tok, num_segments=T)
    return out.reshape(B_, S_, D_)


def setup_inputs(seed: int = 0) -> dict:
    key = jax.random.key(seed)
    ks = jax.random.split(key, 26)
    f32 = jnp.float32
    nrm = lambda k, shape, s: jax.random.normal(k, shape, f32) * s
    return {
        "x": nrm(ks[0], (BATCH, SEQ, D_MODEL), 1.0),
        "even_w_in": nrm(ks[1], (N_EVEN, D_MODEL, EVEN_IN), D_MODEL ** -0.5),
        "pool_w": nrm(ks[2], (N_EVEN, len(POOL_WINDOWS), POOL_GROUP, POOL_GROUP), POOL_GROUP ** -0.5),
        "pool_scale": 1.0 + nrm(ks[3], (N_EVEN, POOL_WIDTH), 0.1),
        "conv_w": nrm(ks[4], (N_EVEN, CONV_K, 2 * ML_QK), CONV_K ** -0.5),
        "conv_b": nrm(ks[5], (N_EVEN, 2 * ML_QK), 0.02),
        "i_bias": nrm(ks[6], (N_EVEN, ML_HEADS), 0.1),
        "f_bias": jnp.linspace(3.0, 6.0, ML_HEADS, dtype=f32)[None, :] + nrm(ks[7], (N_EVEN, ML_HEADS), 0.1),
        "ml_norm": 1.0 + nrm(ks[8], (N_EVEN, ML_WIDTH), 0.02),
        "even_w_out": nrm(ks[9], (N_EVEN, EVEN_MIX, D_MODEL), BETA * EVEN_MIX ** -0.5),
        "odd_w_in": nrm(ks[10], (N_ODD, D_MODEL, ODD_IN), D_MODEL ** -0.5),
        "gla_w2": nrm(ks[11], (N_ODD, GLA_RANK, GLA_QK), GLA_RANK ** -0.5),
        "gla_b": nrm(ks[12], (N_ODD, GLA_QK), 0.1),
        "gla_norm": 1.0 + nrm(ks[13], (N_ODD, GLA_V), 0.02),
        "odd_w_out": nrm(ks[14], (N_ODD, GLA_V, D_MODEL), BETA * GLA_V ** -0.5),
        "ln1_g": 1.0 + nrm(ks[15], (DEPTH, D_MODEL), 0.02),
        "ln1_b": nrm(ks[16], (DEPTH, D_MODEL), 0.02),
        "ln2_g": 1.0 + nrm(ks[17], (DEPTH, D_MODEL), 0.02),
        "ln2_b": nrm(ks[18], (DEPTH, D_MODEL), 0.02),
        "router_w": nrm(ks[19], (DEPTH, D_MODEL, N_EXPERTS), D_MODEL ** -0.5),
        "router_b": nrm(ks[20], (DEPTH, N_EXPERTS), 0.01),
        "w_gate_up": nrm(ks[21], (DEPTH, N_EXPERTS, D_MODEL, 2 * D_EXPERT), D_MODEL ** -0.5),
        "b_gate_up": nrm(ks[22], (DEPTH, N_EXPERTS, 2 * D_EXPERT), 0.01),
        "w_down": nrm(ks[23], (DEPTH, N_EXPERTS, D_EXPERT, D_MODEL), BETA * D_EXPERT ** -0.5),
        "b_down": nrm(ks[24], (DEPTH, N_EXPERTS, D_MODEL), 0.01),
    }


def reference(x, even_w_in, pool_w, pool_scale, conv_w, conv_b, i_bias, f_bias, ml_norm,
              even_w_out, odd_w_in, gla_w2, gla_b, gla_norm, odd_w_out,
              ln1_g, ln1_b, ln2_g, ln2_b, router_w, router_b,
              w_gate_up, b_gate_up, w_down, b_down):
    for layer in range(DEPTH):
        i = layer // 2
        if layer % 2 == 0:
            mix = even_mixer(x, even_w_in[i], pool_w[i], pool_scale[i], conv_w[i], conv_b[i],
                             i_bias[i], f_bias[i], ml_norm[i], even_w_out[i])
        else:
            mix = odd_mixer(x, odd_w_in[i], gla_w2[i], gla_b[i], gla_norm[i], odd_w_out[i])
        x = layer_norm(ALPHA * x + mix, ln1_g[layer], ln1_b[layer])
        ffn = moe(x, router_w[layer], router_b[layer], w_gate_up[layer], b_gate_up[layer],
                  w_down[layer], b_down[layer])
        x = layer_norm(ALPHA * x + ffn, ln2_g[layer], ln2_b[layer])
    return x
```

```python
import functools

import jax
import jax.numpy as jnp
from jax import lax
from jax.experimental import pallas as pl
from jax.experimental.pallas import tpu as pltpu

F32 = jnp.float32
BF16 = jnp.bfloat16
I32 = jnp.int32

LANES = 128
SUBLANES = 8
VMEM_LIMIT = 56 * 1024 * 1024

DEPTH = 4
ALPHA = (2 * DEPTH) ** 0.25
LN_EPS = 1e-5
NEG = -1e30

POOL_WINDOWS = (2, 4, 8, 16)
POOL_CARRY = 16
CONV_K = 4
CONV_CARRY = 8
HEADS = 4
HEAD_DK = 128
GLA_DV = 256
GLA_RANK = 16
GLA_TAU = 16.0
GLA_SUB = 16
N_EXPERTS = 32
TOP_K = 4
SWIGLU_LIMIT = 7.0
SWIGLU_ALPHA = 1.702

SEQ_CHUNK = 128
FFN_TILE = 256
COMBINE_TILE = 128
DISPATCH_GROUP = 128


def _dot(a, b):
    return jnp.dot(a, b, preferred_element_type=F32)


def _dot_nt(a, b):
    return lax.dot_general(a, b, (((1,), (1,)), ((), ())), preferred_element_type=F32)


def _split3(a):
    a1 = a.astype(BF16)
    r1 = a - a1.astype(F32)
    a2 = r1.astype(BF16)
    r2 = r1 - a2.astype(F32)
    return a1, a2, r2.astype(BF16)


def _dot01_left(m01, a):
    a1, a2, a3 = _split3(a)
    return _dot(m01, a1) + _dot(m01, a2) + _dot(m01, a3)


def _dot01_right(a, m01):
    a1, a2, a3 = _split3(a)
    return _dot(a1, m01) + _dot(a2, m01) + _dot(a3, m01)


def _log_sigmoid(x):
    return jnp.minimum(x, 0.0) - jnp.log1p(jnp.exp(-jnp.abs(x)))


def _sigmoid(x):
    return 1.0 / (1.0 + jnp.exp(-x))


def _params(sem):
    return pltpu.CompilerParams(dimension_semantics=sem, vmem_limit_bytes=VMEM_LIMIT)


def _proj_kernel(x_ref, w_ref, o_ref, xb_ref):
    @pl.when(pl.program_id(1) == 0)
    def _():
        xb_ref[...] = x_ref[...].astype(BF16)

    o_ref[...] = _dot(xb_ref[...], w_ref[...].astype(BF16))


def _project(x2d, w3d, li, n_cols, tn=512):
    T, D = x2d.shape
    tm = min(1024, T)
    return pl.pallas_call(
        _proj_kernel,
        out_shape=jax.ShapeDtypeStruct((T, n_cols), F32),
        grid=(T // tm, n_cols // tn),
        in_specs=[pl.BlockSpec((tm, D), lambda i, j: (i, 0)),
                  pl.BlockSpec((None, D, tn), lambda i, j: (li, 0, j))],
        out_specs=pl.BlockSpec((tm, tn), lambda i, j: (i, j)),
        scratch_shapes=[pltpu.VMEM((tm, D), BF16)],
        compiler_params=_params(("arbitrary", "arbitrary")),
        name="in_proj",
    )(x2d, w3d)


def _even_mixer_kernel(p_ref, x_ref, wgc_ref, wgr_ref, gbr_ref, gbc_ref, poolw_ref, pscale_ref,
                       convw_ref, convb_ref, mlnorm_ref, mix_ref,
                       ubuf, qkbuf, c_st, n_st, m_st):
    L = SEQ_CHUNK
    c = pl.program_id(1)

    @pl.when(c == 0)
    def _():
        ubuf[0:POOL_CARRY, :] = jnp.zeros((POOL_CARRY, 512), F32)
        qkbuf[0:CONV_CARRY, :] = jnp.zeros((CONV_CARRY, 1024), F32)
        c_st[...] = jnp.zeros_like(c_st)
        n_st[...] = jnp.zeros_like(n_st)
        m_st[...] = jnp.zeros_like(m_st)

    ubuf[POOL_CARRY:POOL_CARRY + L, :] = p_ref[:, 0:512]
    pos1 = lax.broadcasted_iota(I32, (L, 1), 0) + c * L + 1
    for g, w in enumerate(POOL_WINDOWS):
        lo = g * 128
        acc = ubuf[pl.ds(POOL_CARRY, L), lo:lo + 128]
        cur = acc
        for j in range(1, w):
            acc = acc + ubuf[pl.ds(POOL_CARRY - j, L), lo:lo + 128]
        cnt = jnp.minimum(pos1, w).astype(F32)
        d = acc / cnt - cur
        yp = _dot(d.astype(BF16), poolw_ref[g].astype(BF16)) * pscale_ref[:, lo:lo + 128]
        mix_ref[:, lo:lo + 128] = yp.astype(BF16)
    ubuf[0:POOL_CARRY, :] = ubuf[L:L + POOL_CARRY, :]

    qkbuf[CONV_CARRY:CONV_CARRY + L, :] = p_ref[:, 512:1536]
    acc = jnp.zeros((L, 1024), F32) + convb_ref[...]
    for j in range(CONV_K):
        acc = acc + convw_ref[j:j + 1, :] * qkbuf[pl.ds(CONV_CARRY - (CONV_K - 1) + j, L), :]
    qk = acc * _sigmoid(acc)
    qkbuf[0:CONV_CARRY, :] = qkbuf[L:L + CONV_CARRY, :]

    xb = x_ref[...].astype(BF16)
    gcol = _dot(xb, wgc_ref[...].astype(BF16)) + gbr_ref[...]
    grow = _dot_nt(wgr_ref[...].astype(BF16), xb) + gbc_ref[:, 0:1]
    ri = lax.broadcasted_iota(I32, (L, L), 0)
    ci = lax.broadcasted_iota(I32, (L, L), 1)
    causal = ci <= ri
    tri_l = jnp.where(causal, 1.0, 0.0).astype(BF16)
    tri_u = jnp.where(ri <= ci, 1.0, 0.0).astype(BF16)
    b_col = _dot01_left(tri_l, _log_sigmoid(gcol))
    b_row = _dot01_right(_log_sigmoid(grow), tri_u)

    kscale = HEAD_DK ** -0.5
    for h in range(HEADS):
        lo = h * 128
        qh = qk[:, lo:lo + 128]
        kh = qk[:, 512 + lo:512 + lo + 128] * kscale
        vh = p_ref[:, 1536 + lo:1536 + lo + 128]
        og = p_ref[:, 2048 + lo:2048 + lo + 128]
        b_c = b_col[:, 4 + h:5 + h]
        b_r = b_row[4 + h:5 + h, :]
        ig_c = gcol[:, h:h + 1]
        ig_r = grow[h:h + 1, :]
        g = b_r[:, L - 1:L]
        m_prev = m_st[h][:, 0:1]
        qb = qh.astype(BF16)
        kb = kh.astype(BF16)
        vb = vh.astype(BF16)

        dmat = jnp.where(causal, b_c - b_r + ig_r, NEG)
        m_inter = b_c + m_prev
        m_t = jnp.maximum(m_inter, jnp.max(dmat, axis=1, keepdims=True))
        sts = _dot_nt(qb, kb) * jnp.exp(dmat - m_t)
        sc = jnp.exp(m_inter - m_t)
        num = _dot(sts.astype(BF16), vb) + sc * _dot(qb, c_st[h].astype(BF16))
        qn = jnp.sum(qh * n_st[h], axis=1, keepdims=True)
        den = jnp.sum(sts, axis=1, keepdims=True) + sc * qn
        hh = num / jnp.maximum(jnp.abs(den), jnp.exp(-m_t))

        a_c = g - b_c + ig_c
        a_r = g - b_r + ig_r
        m_new = jnp.maximum(g + m_prev, jnp.max(a_r, axis=1, keepdims=True))
        decay = jnp.exp(g + m_prev - m_new)
        wk = jnp.exp(a_c - m_new) * kh
        c_st[h] = decay * c_st[h] + _dot(wk.T.astype(BF16), vb)
        n_st[h] = decay * n_st[h] + jnp.sum(wk, axis=0, keepdims=True)
        m_st[h] = jnp.broadcast_to(m_new, (1, LANES))

        mu = jnp.mean(hh, axis=1, keepdims=True)
        hc = hh - mu
        var = jnp.mean(hc * hc, axis=1, keepdims=True)
        hn = hc * lax.rsqrt(var + LN_EPS) * mlnorm_ref[:, lo:lo + 128]
        mix_ref[:, 512 + lo:512 + lo + 128] = (hn * _sigmoid(og)).astype(BF16)


def _even_mixer(p, x2d, B, S, w_in, li, pool_w, pool_scale, conv_w, conv_b, i_bias, f_bias, ml_norm):
    T, D = x2d.shape
    L = SEQ_CHUNK
    nc = S // L
    wg = w_in[li, :, 2560:2568]
    wgc = jnp.pad(wg, ((0, 0), (0, LANES - 8)))
    wgr = wg.T
    gb = jnp.concatenate([i_bias[li], f_bias[li]])
    gbr = jnp.pad(gb, (0, LANES - 8)).reshape(1, LANES)
    gbc = jnp.broadcast_to(gb.reshape(8, 1), (8, LANES))
    row = lambda b, c: (b * nc + c, 0)
    full2 = lambda b, c: (0, 0)
    return pl.pallas_call(
        _even_mixer_kernel,
        out_shape=jax.ShapeDtypeStruct((T, D), BF16),
        grid=(B, nc),
        in_specs=[pl.BlockSpec((L, 2560), row),
                  pl.BlockSpec((L, D), row),
                  pl.BlockSpec((D, LANES), full2),
                  pl.BlockSpec((8, D), full2),
                  pl.BlockSpec((1, LANES), full2),
                  pl.BlockSpec((8, LANES), full2),
                  pl.BlockSpec((None, 4, 128, 128), lambda b, c: (li, 0, 0, 0)),
                  pl.BlockSpec((None, 1, 512), lambda b, c: (li, 0, 0)),
                  pl.BlockSpec((None, CONV_K, 1024), lambda b, c: (li, 0, 0)),
                  pl.BlockSpec((None, 1, 1024), lambda b, c: (li, 0, 0)),
                  pl.BlockSpec((None, 1, 512), lambda b, c: (li, 0, 0))],
        out_specs=pl.BlockSpec((L, D), row),
        scratch_shapes=[pltpu.VMEM((L + POOL_CARRY, 512), F32),
                        pltpu.VMEM((L + CONV_CARRY, 1024), F32),
                        pltpu.VMEM((HEADS, 128, 128), F32),
                        pltpu.VMEM((HEADS, 1, 128), F32),
                        pltpu.VMEM((HEADS, 1, LANES), F32)],
        compiler_params=_params(("arbitrary", "arbitrary")),
        name="even_mixer",
    )(p, x2d, wgc, wgr, gbr, gbc, pool_w, pool_scale.reshape(-1, 1, 512), conv_w,
      conv_b.reshape(-1, 1, 1024), ml_norm.reshape(-1, 1, 512))


def _odd_mixer_kernel(p_ref, x_ref, wr_ref, w2_ref, gb_ref, gnorm_ref, mix_ref, s_st):
    L = SEQ_CHUNK
    c = pl.program_id(1)

    @pl.when(c == 0)
    def _():
        s_st[...] = jnp.zeros_like(s_st)

    xb = x_ref[...].astype(BF16)
    glr = _dot(xb, wr_ref[...].astype(BF16))
    z = _dot(glr.astype(BF16), w2_ref[...].astype(BF16)) + gb_ref[...]
    lg = _log_sigmoid(z) / GLA_TAU

    ri = lax.broadcasted_iota(I32, (L, L), 0)
    ci = lax.broadcasted_iota(I32, (L, L), 1)
    one = lambda m: jnp.where(m, 1.0, 0.0).astype(BF16)
    causal = ci <= ri
    blk0 = (ri // GLA_SUB) * GLA_SUB
    m_in = one(causal & (ci >= blk0))
    m_all = one(causal)
    lg3 = _split3(lg)
    cum = lambda m01: _dot(m01, lg3[0]) + _dot(m01, lg3[1]) + _dot(m01, lg3[2])
    b_in = cum(m_in)
    b_all = cum(m_all)

    levels = []
    for hs in (64, 32, 16):
        ps = 2 * hs
        pair0 = (ri // ps) * ps
        refrow = pair0 + hs - 1
        upper = (ri - pair0) >= hs
        dq = cum(one(upper & (ci > refrow) & causal))
        dk = cum(one((~upper) & (ci > ri) & (ci <= refrow)))
        rcol = lax.broadcasted_iota(I32, (L, HEAD_DK), 0)
        up_c = ((rcol % ps) >= hs)
        same_pair = (ri // ps) == (ci // ps)
        levels.append((dq, dk, up_c, same_pair, ps))

    rows16 = lax.broadcasted_iota(I32, (GLA_SUB, HEAD_DK), 0)
    lane16 = lax.broadcasted_iota(I32, (GLA_SUB, LANES), 1)
    eye = ri == ci
    qscale = HEAD_DK ** -0.5
    for h in range(HEADS):
        lo = h * 128
        qh = p_ref[:, lo:lo + 128] * qscale
        kh = p_ref[:, 512 + lo:512 + lo + 128]
        vb = p_ref[:, 1024 + h * GLA_DV:1024 + (h + 1) * GLA_DV].astype(BF16)
        rg = p_ref[:, 2048 + h * GLA_DV:2048 + (h + 1) * GLA_DV]
        bh = b_all[:, lo:lo + 128]
        bin_h = b_in[:, lo:lo + 128]

        o = _dot((qh * jnp.exp(bh)).astype(BF16), s_st[h].astype(BF16))

        amat = jnp.zeros((L, L), F32)
        for dq, dk, up_c, same_pair, ps in levels:
            qt = jnp.where(up_c, qh * jnp.exp(dq[:, lo:lo + 128]), 0.0)
            kt = jnp.where(up_c, 0.0, kh * jnp.exp(dk[:, lo:lo + 128]))
            a_lev = _dot_nt(qt.astype(BF16), kt.astype(BF16))
            amat = amat + (a_lev if ps == L else jnp.where(same_pair, a_lev, 0.0))

        blocks = []
        for blk in range(L // GLA_SUB):
            r0 = blk * GLA_SUB
            qb16 = qh[r0:r0 + GLA_SUB, :]
            kb16 = kh[r0:r0 + GLA_SUB, :]
            bb16 = bin_h[r0:r0 + GLA_SUB, :]
            ablk = jnp.zeros((GLA_SUB, LANES), F32)
            for s in range(GLA_SUB):
                e = jnp.where(rows16 >= s, bb16 - bb16[s:s + 1, :], NEG)
                pr = qb16 * kb16[s:s + 1, :] * jnp.exp(e)
                a = jnp.sum(pr, axis=1, keepdims=True)
                ablk = jnp.where(lane16 == (r0 + s), a, ablk)
            blocks.append(ablk)
        amat = amat + jnp.concatenate(blocks, axis=0)

        o = o + _dot(amat.astype(BF16), vb)

        bl = bh[L - 1:L, :]
        kdec = kh * jnp.exp(bl - bh)
        ecol = jnp.sum(jnp.where(eye, jnp.exp(bl), 0.0), axis=1, keepdims=True)
        s_st[h] = ecol * s_st[h] + _dot(kdec.T.astype(BF16), vb)

        mu = jnp.mean(o, axis=1, keepdims=True)
        oc = o - mu
        var = jnp.mean(oc * oc, axis=1, keepdims=True)
        on = oc * lax.rsqrt(var + LN_EPS) * gnorm_ref[:, h * GLA_DV:(h + 1) * GLA_DV]
        mix_ref[:, h * GLA_DV:(h + 1) * GLA_DV] = (on * (rg * _sigmoid(rg))).astype(BF16)


def _odd_mixer(p, x2d, B, S, w_in, li, gla_w2, gla_b, gla_norm):
    T, D = x2d.shape
    L = SEQ_CHUNK
    nc = S // L
    wr = jnp.pad(w_in[li, :, 3072:3072 + GLA_RANK], ((0, 0), (0, LANES - GLA_RANK)))
    w2 = jnp.pad(gla_w2[li], ((0, LANES - GLA_RANK), (0, 0)))
    row = lambda b, c: (b * nc + c, 0)
    full2 = lambda b, c: (0, 0)
    return pl.pallas_call(
        _odd_mixer_kernel,
        out_shape=jax.ShapeDtypeStruct((T, D), BF16),
        grid=(B, nc),
        in_specs=[pl.BlockSpec((L, 3072), row),
                  pl.BlockSpec((L, D), row),
                  pl.BlockSpec((D, LANES), full2),
                  pl.BlockSpec((LANES, 512), full2),
                  pl.BlockSpec((None, 1, 512), lambda b, c: (li, 0, 0)),
                  pl.BlockSpec((None, 1, 1024), lambda b, c: (li, 0, 0))],
        out_specs=pl.BlockSpec((L, D), row),
        scratch_shapes=[pltpu.VMEM((HEADS, HEAD_DK, GLA_DV), F32)],
        compiler_params=_params(("arbitrary", "arbitrary")),
        name="odd_mixer",
    )(p, x2d, wr, w2, gla_b.reshape(-1, 1, 512), gla_norm.reshape(-1, 1, 1024))


def _post_kernel(mix_ref, x_ref, wout_ref, g_ref, b_ref, rw_ref, rb_ref,
                 x1_ref, x1r_ref, eid_ref, rank_ref, gate_ref, cnt_ref,
                 wb_ref, carry_ref):
    tm = mix_ref.shape[0]
    i = pl.program_id(0)

    @pl.when(i == 0)
    def _():
        wb_ref[...] = wout_ref[...].astype(BF16)
        carry_ref[...] = jnp.zeros_like(carry_ref)

    y = _dot(mix_ref[...], wb_ref[...]) + ALPHA * x_ref[...]
    mu = jnp.mean(y, axis=1, keepdims=True)
    yc = y - mu
    var = jnp.mean(yc * yc, axis=1, keepdims=True)
    x1 = yc * lax.rsqrt(var + LN_EPS) * g_ref[...] + b_ref[...]
    x1_ref[...] = x1
    for c in range(SUBLANES):
        x1r_ref[pl.ds(c, tm, stride=SUBLANES), :] = x1[:, c * LANES:(c + 1) * LANES]

    hi = x1.astype(BF16)
    lo = (x1 - hi.astype(F32)).astype(BF16)
    rw = rw_ref[...]
    whi = rw.astype(BF16)
    wlo = (rw - whi.astype(F32)).astype(BF16)
    logits = _dot(hi, whi) + _dot(lo, whi) + _dot(hi, wlo) + rb_ref[...]

    lane = lax.broadcasted_iota(I32, (tm, LANES), 1)
    lane_f = lane.astype(F32)
    l = logits
    vals, ids, ohs = [], [], []
    for _ in range(TOP_K):
        m = jnp.max(l, axis=1, keepdims=True)
        idx = jnp.min(jnp.where(l == m, lane_f, float(LANES)), axis=1, keepdims=True)
        oh = lane_f == idx
        vals.append(m)
        ids.append(idx)
        ohs.append(oh)
        l = jnp.where(oh, -jnp.inf, l)
    es = [jnp.exp(v - vals[0]) for v in vals]
    tot = es[0] + es[1] + es[2] + es[3]

    onehot = jnp.zeros((tm, LANES), F32)
    for oh in ohs:
        onehot = onehot + jnp.where(oh, 1.0, 0.0)
    ri = lax.broadcasted_iota(I32, (tm, tm), 0)
    ci = lax.broadcasted_iota(I32, (tm, tm), 1)
    strict = jnp.where(ci < ri, 1.0, 0.0).astype(BF16)
    before = _dot(strict, onehot.astype(BF16)) + carry_ref[...]
    carry_ref[...] = carry_ref[...] + jnp.sum(onehot, axis=0, keepdims=True)
    cnt_ref[...] = carry_ref[...]

    eid = jnp.zeros((tm, LANES), F32)
    rank = jnp.zeros((tm, LANES), F32)
    gate = jnp.zeros((tm, LANES), F32)
    for k in range(TOP_K):
        rk = jnp.sum(jnp.where(ohs[k], before, 0.0), axis=1, keepdims=True)
        sel = lane == k
        eid = jnp.where(sel, ids[k], eid)
        rank = jnp.where(sel, rk, rank)
        gate = jnp.where(sel, es[k] / tot, gate)
    eid_ref[...] = eid.astype(I32)
    rank_ref[...] = rank.astype(I32)
    gate_ref[...] = gate


def _post(mix, x2d, w_out, li, ln_g, ln_b, router_w, router_b, layer):
    T, D = x2d.shape
    tm = min(256, T)
    rw = jnp.pad(router_w[layer], ((0, 0), (0, LANES - N_EXPERTS)))
    rb = jnp.pad(router_b[layer], (0, LANES - N_EXPERTS), constant_values=NEG).reshape(1, LANES)
    row = lambda i: (i, 0)
    full2 = lambda i: (0, 0)
    lane_out = jax.ShapeDtypeStruct((T, LANES), I32)
    return pl.pallas_call(
        _post_kernel,
        out_shape=(jax.ShapeDtypeStruct((T, D), F32),
                   jax.ShapeDtypeStruct((T * SUBLANES, LANES), F32),
                   lane_out, lane_out,
                   jax.ShapeDtypeStruct((T, LANES), F32),
                   jax.ShapeDtypeStruct((1, LANES), F32)),
        grid=(T // tm,),
        in_specs=[pl.BlockSpec((tm, D), row),
                  pl.BlockSpec((tm, D), row),
                  pl.BlockSpec((None, D, D), lambda i: (li, 0, 0)),
                  pl.BlockSpec((None, 1, D), lambda i: (layer, 0, 0)),
                  pl.BlockSpec((None, 1, D), lambda i: (layer, 0, 0)),
                  pl.BlockSpec((D, LANES), full2),
                  pl.BlockSpec((1, LANES), full2)],
        out_specs=(pl.BlockSpec((tm, D), row),
                   pl.BlockSpec((tm * SUBLANES, LANES), row),
                   pl.BlockSpec((tm, LANES), row),
                   pl.BlockSpec((tm, LANES), row),
                   pl.BlockSpec((tm, LANES), row),
                   pl.BlockSpec((1, LANES), full2)),
        scratch_shapes=[pltpu.VMEM((D, D), BF16), pltpu.VMEM((1, LANES), F32)],
        compiler_params=_params(("arbitrary",)),
        name="out_proj_ln_router",
    )(mix, x2d, w_out, ln_g.reshape(-1, 1, D), ln_b.reshape(-1, 1, D), rw, rb)


def _pos_kernel(eid_ref, rank_ref, off_ref, pos_ref):
    tm = eid_ref.shape[0]
    lane = lax.broadcasted_iota(I32, (tm, LANES), 1)
    lane_f = lane.astype(F32)
    eid = eid_ref[...].astype(F32)
    off = off_ref[...]
    pos = jnp.zeros((tm, LANES), F32)
    for k in range(TOP_K):
        ek = jnp.sum(jnp.where(lane == k, eid, 0.0), axis=1, keepdims=True)
        ok = jnp.sum(jnp.where(lane_f == ek, off, 0.0), axis=1, keepdims=True)
        pos = jnp.where(lane == k, ok, pos)
    pos_ref[...] = pos.astype(I32) + rank_ref[...]


def _positions(eid, rank, off_row):
    T = eid.shape[0]
    tm = min(1024, T)
    row = lambda i: (i, 0)
    return pl.pallas_call(
        _pos_kernel,
        out_shape=jax.ShapeDtypeStruct((T, LANES), I32),
        grid=(T // tm,),
        in_specs=[pl.BlockSpec((tm, LANES), row), pl.BlockSpec((tm, LANES), row),
                  pl.BlockSpec((1, LANES), lambda i: (0, 0))],
        out_specs=pl.BlockSpec((tm, LANES), row),
        compiler_params=_params(("arbitrary",)),
        name="route_positions",
    )(eid, rank, off_row)


def _dispatch_kernel(pos_ref, cnt_ref, off_ref, nv_ref, x_ref, xs_ref, zbuf, sems, zsem):
    T = x_ref.shape[0] // SUBLANES
    G = min(DISPATCH_GROUP, T)
    n_groups = T // G
    gi = pl.program_id(0)
    slot = gi % 2

    def row_copy(t, k):
        src = x_ref.at[pl.ds(pl.multiple_of(t * SUBLANES, SUBLANES), SUBLANES), :]
        p = pos_ref[t * TOP_K + k]
        dst = xs_ref.at[pl.ds(pl.multiple_of(p * SUBLANES, SUBLANES), SUBLANES), :]
        return pltpu.make_async_copy(src, dst, sems.at[slot])

    def group_wait(s):
        n = G * TOP_K * SUBLANES
        pltpu.make_async_copy(xs_ref.at[pl.ds(0, n), :], xs_ref.at[pl.ds(0, n), :], sems.at[s]).wait()

    @pl.when(gi >= 2)
    def _():
        group_wait(slot)

    def tok(tt, carry):
        t = gi * G + tt
        for k in range(TOP_K):
            row_copy(t, k).start()
        return carry

    lax.fori_loop(0, G, tok, 0)

    @pl.when(gi == n_groups - 1)
    def _():
        group_wait(slot)
        if n_groups >= 2:
            group_wait(1 - slot)

        zbuf[...] = jnp.zeros_like(zbuf)

        def pad_expert(e, carry):
            cnt = cnt_ref[e]
            npad = (FFN_TILE - cnt % FFN_TILE) % FFN_TILE
            base = off_ref[e] + cnt

            def zcopy(r):
                dst = xs_ref.at[pl.ds(pl.multiple_of((base + r) * SUBLANES, SUBLANES), SUBLANES), :]
                return pltpu.make_async_copy(zbuf.at[pl.ds(0, SUBLANES), :], dst, zsem)

            def start(r, c2):
                zcopy(r).start()
                return c2

            def wait(r, c2):
                zcopy(r).wait()
                return c2

            lax.fori_loop(0, npad, start, 0)
            lax.fori_loop(0, npad, wait, 0)
            return carry

        lax.fori_loop(0, N_EXPERTS, pad_expert, 0)

        tile_rows = FFN_TILE * SUBLANES
        n_tiles = xs_ref.shape[0] // tile_rows

        def tcopy(j):
            dst = xs_ref.at[pl.ds(pl.multiple_of(j * tile_rows, tile_rows), tile_rows), :]
            return pltpu.make_async_copy(zbuf, dst, zsem)

        def tstart(j, c2):
            tcopy(j).start()
            return c2

        def twait(j, c2):
            tcopy(j).wait()
            return c2

        lax.fori_loop(nv_ref[0], n_tiles, tstart, 0)
        lax.fori_loop(nv_ref[0], n_tiles, twait, 0)


def _dispatch(pos_flat, counts, offs, n_valid, x1r, n_rows):
    T = x1r.shape[0] // SUBLANES
    return pl.pallas_call(
        _dispatch_kernel,
        out_shape=jax.ShapeDtypeStruct((n_rows * SUBLANES, LANES), F32),
        grid_spec=pltpu.PrefetchScalarGridSpec(
            num_scalar_prefetch=4,
            grid=(T // min(DISPATCH_GROUP, T),),
            in_specs=[pl.BlockSpec(memory_space=pl.ANY)],
            out_specs=pl.BlockSpec(memory_space=pl.ANY),
            scratch_shapes=[pltpu.VMEM((FFN_TILE * SUBLANES, LANES), F32),
                            pltpu.SemaphoreType.DMA((2,)),
                            pltpu.SemaphoreType.DMA(())]),
        compiler_params=_params(("arbitrary",)),
        name="dispatch_rows",
    )(pos_flat, counts, offs, n_valid, x1r)


def _ffn_kernel(te_ref, nv_ref, xs_ref, wgu_ref, wd_ref, bg_ref, bl_ref, bd_ref, y_ref,
                xb_ref, wg_s, wl_s, wd_s):
    tm = FFN_TILE
    j = pl.program_id(0)
    valid = j < nv_ref[0]
    first = jnp.logical_or(j == 0, te_ref[j] != te_ref[jnp.maximum(j - 1, 0)])

    @pl.when(jnp.logical_and(valid, first))
    def _():
        pi = lax.broadcasted_iota(I32, (256, 256), 0)
        pj = lax.broadcasted_iota(I32, (256, 256), 1)
        src = jnp.where(pj < 128, 2 * pj, 2 * (pj - 128) + 1)
        perm = jnp.where(pi == src, 1.0, 0.0).astype(BF16)
        for blk in range(8):
            r = _dot(wgu_ref[:, blk * 256:(blk + 1) * 256].astype(BF16), perm)
            wg_s[:, blk * 128:(blk + 1) * 128] = r[:, 0:128].astype(BF16)
            wl_s[:, blk * 128:(blk + 1) * 128] = r[:, 128:256].astype(BF16)
        wd_s[...] = wd_ref[...].astype(BF16)

    @pl.when(valid)
    def _():
        for c in range(SUBLANES):
            xb_ref[:, c * LANES:(c + 1) * LANES] = xs_ref[pl.ds(c, tm, stride=SUBLANES), :].astype(BF16)
        xb = xb_ref[...]
        hg = _dot(xb, wg_s[...]) + bg_ref[...]
        hl = _dot(xb, wl_s[...]) + bl_ref[...]
        glu = jnp.minimum(hg, SWIGLU_LIMIT)
        lin = jnp.clip(hl, -SWIGLU_LIMIT, SWIGLU_LIMIT)
        act = glu * _sigmoid(SWIGLU_ALPHA * glu) * (lin + 1.0)
        y = _dot(act.astype(BF16), wd_s[...]) + bd_ref[...]
        for c in range(SUBLANES):
            y_ref[pl.ds(c, tm, stride=SUBLANES), :] = y[:, c * LANES:(c + 1) * LANES]

    @pl.when(jnp.logical_not(valid))
    def _():
        y_ref[...] = jnp.zeros_like(y_ref)


def _ffn(tile_expert, n_valid, xs, w_gate_up, w_down, bg, bl, b_down, layer, n_tiles):
    tm = FFN_TILE
    D = w_down.shape[-1]
    tile = lambda j, te, nv: (jnp.minimum(j, nv[0] - 1), 0)
    out_tile = lambda j, te, nv: (j, 0)
    return pl.pallas_call(
        _ffn_kernel,
        out_shape=jax.ShapeDtypeStruct(xs.shape, F32),
        grid_spec=pltpu.PrefetchScalarGridSpec(
            num_scalar_prefetch=2,
            grid=(n_tiles,),
            in_specs=[pl.BlockSpec((tm * SUBLANES, LANES), tile),
                      pl.BlockSpec((None, None, D, 2 * D), lambda j, te, nv: (layer, te[j], 0, 0)),
                      pl.BlockSpec((None, None, D, D), lambda j, te, nv: (layer, te[j], 0, 0)),
                      pl.BlockSpec((None, None, 1, D), lambda j, te, nv: (layer, te[j], 0, 0)),
                      pl.BlockSpec((None, None, 1, D), lambda j, te, nv: (layer, te[j], 0, 0)),
                      pl.BlockSpec((None, None, 1, D), lambda j, te, nv: (layer, te[j], 0, 0))],
            out_specs=pl.BlockSpec((tm * SUBLANES, LANES), out_tile),
            scratch_shapes=[pltpu.VMEM((tm, D), BF16),
                            pltpu.VMEM((D, D), BF16),
                            pltpu.VMEM((D, D), BF16),
                            pltpu.VMEM((D, D), BF16)]),
        compiler_params=_params(("arbitrary",)),
        name="expert_ffn",
    )(tile_expert, n_valid, xs, w_gate_up, w_down, bg, bl, b_down)


def _combine_kernel(pos_ref, y_ref, gate_ref, x1_ref, g_ref, b_ref, o_ref, ybuf, zs, sems):
    tc = COMBINE_TILE
    i = pl.program_id(0)
    n = pl.num_programs(0)

    def issue(tile, slot):
        def tok(tt, carry):
            for k in range(TOP_K):
                p = pos_ref[(tile * tc + tt) * TOP_K + k]
                src = y_ref.at[pl.ds(pl.multiple_of(p * SUBLANES, SUBLANES), SUBLANES), :]
                dst = ybuf.at[slot, pl.ds(pl.multiple_of((k * tc + tt) * SUBLANES, SUBLANES), SUBLANES), :]
                pltpu.make_async_copy(src, dst, sems.at[slot]).start()
            return carry

        lax.fori_loop(0, tc, tok, 0)

    @pl.when(i == 0)
    def _():
        issue(0, 0)

    @pl.when(i + 1 < n)
    def _():
        issue(i + 1, (i + 1) % 2)

    slot = i % 2
    rows = TOP_K * tc * SUBLANES
    pltpu.make_async_copy(y_ref.at[pl.ds(0, rows), :], ybuf.at[slot], sems.at[slot]).wait()

    gate = gate_ref[...]
    s1 = jnp.zeros((tc, 1), F32)
    for c in range(SUBLANES):
        z = ALPHA * x1_ref[:, c * LANES:(c + 1) * LANES]
        for k in range(TOP_K):
            z = z + gate[:, k:k + 1] * ybuf[slot, pl.ds(k * tc * SUBLANES + c, tc, stride=SUBLANES), :]
        zs[:, c * LANES:(c + 1) * LANES] = z
        s1 = s1 + jnp.sum(z, axis=1, keepdims=True)
    zv = zs[...]
    mu = s1 / zv.shape[1]
    zc = zv - mu
    var = jnp.mean(zc * zc, axis=1, keepdims=True)
    o_ref[...] = zc * lax.rsqrt(var + LN_EPS) * g_ref[...] + b_ref[...]


def _combine(pos_flat, y, gate, x1, ln_g, ln_b, layer):
    T, D = x1.shape
    tc = COMBINE_TILE
    row = lambda i, pos: (i, 0)
    return pl.pallas_call(
        _combine_kernel,
        out_shape=jax.ShapeDtypeStruct((T, D), F32),
        grid_spec=pltpu.PrefetchScalarGridSpec(
            num_scalar_prefetch=1,
            grid=(T // tc,),
            in_specs=[pl.BlockSpec(memory_space=pl.ANY),
                      pl.BlockSpec((tc, LANES), row),
                      pl.BlockSpec((tc, D), row),
                      pl.BlockSpec((None, 1, D), lambda i, pos: (layer, 0, 0)),
                      pl.BlockSpec((None, 1, D), lambda i, pos: (layer, 0, 0))],
            out_specs=pl.BlockSpec((tc, D), row),
            scratch_shapes=[pltpu.VMEM((2, TOP_K * tc * SUBLANES, LANES), F32),
                            pltpu.VMEM((tc, D), F32),
                            pltpu.SemaphoreType.DMA((2,))]),
        compiler_params=_params(("arbitrary",)),
        name="combine_ln",
    )(pos_flat, y, gate, x1, ln_g.reshape(-1, 1, D), ln_b.reshape(-1, 1, D))


def _moe(x1, x1r, eid, rank, gate, cnt, layer, w_gate_up, b_gate_up, w_down, b_down, ln_g, ln_b):
    T, D = x1.shape
    tm = FFN_TILE
    n_tiles = (T * TOP_K) // tm + N_EXPERTS
    n_rows = n_tiles * tm
    counts = cnt[0, :N_EXPERTS].astype(I32)
    tiles_e = (counts + tm - 1) // tm
    tile_end = jnp.cumsum(tiles_e)
    offs = (tile_end - tiles_e) * tm
    n_valid = tile_end[-1]
    jj = jnp.minimum(jnp.arange(n_tiles, dtype=I32), n_valid - 1)
    tile_expert = jnp.minimum(jnp.searchsorted(tile_end, jj, side="right"), N_EXPERTS - 1).astype(I32)
    off_row = jnp.pad(offs, (0, LANES - N_EXPERTS)).astype(F32).reshape(1, LANES)

    pos = _positions(eid, rank, off_row)
    pos_flat = pos[:, :TOP_K].reshape(-1)
    nv = n_valid.reshape(1).astype(I32)
    xs = _dispatch(pos_flat, counts, offs.astype(I32), nv, x1r, n_rows)
    bgu = b_gate_up.reshape(DEPTH, N_EXPERTS, D, 2)
    bg = bgu[..., 0].reshape(DEPTH, N_EXPERTS, 1, D)
    bl = bgu[..., 1].reshape(DEPTH, N_EXPERTS, 1, D)
    y = _ffn(tile_expert, nv, xs, w_gate_up, w_down, bg, bl,
             b_down.reshape(DEPTH, N_EXPERTS, 1, D), layer, n_tiles)
    return _combine(pos_flat, y, gate, x1, ln_g, ln_b, layer)


def kernel(x, even_w_in, pool_w, pool_scale, conv_w, conv_b, i_bias, f_bias, ml_norm, even_w_out, odd_w_in, gla_w2, gla_b, gla_norm, odd_w_out, ln1_g, ln1_b, ln2_g, ln2_b, router_w, router_b, w_gate_up, b_gate_up, w_down, b_down):
    B, S, D = x.shape
    xt = x.reshape(B * S, D)
    for layer in range(DEPTH):
        li = layer // 2
        if layer % 2 == 0:
            p = _project(xt, even_w_in, li, 2560)
            mix = _even_mixer(p, xt, B, S, even_w_in, li, pool_w, pool_scale, conv_w, conv_b,
                              i_bias, f_bias, ml_norm)
            w_out = even_w_out
        else:
            p = _project(xt, odd_w_in, li, 3072)
            mix = _odd_mixer(p, xt, B, S, odd_w_in, li, gla_w2, gla_b, gla_norm)
            w_out = odd_w_out
        x1, x1r, eid, rank, gate, cnt = _post(mix, xt, w_out, li, ln1_g, ln1_b, router_w, router_b, layer)
        xt = _moe(x1, x1r, eid, rank, gate, cnt, layer, w_gate_up, b_gate_up, w_down, b_down,
                  ln2_g, ln2_b)
    return xt.reshape(B, S, D)
```

```python
import functools

import jax
import jax.numpy as jnp
from jax import lax
from jax.experimental import pallas as pl
from jax.experimental.pallas import tpu as pltpu

F32 = jnp.float32
BF16 = jnp.bfloat16
I32 = jnp.int32

LANES = 128
SUBLANES = 8
VMEM_LIMIT = 56 * 1024 * 1024

DEPTH = 4
ALPHA = (2 * DEPTH) ** 0.25
LN_EPS = 1e-5
NEG = -1e30

POOL_WINDOWS = (2, 4, 8, 16)
POOL_CARRY = 16
CONV_K = 4
CONV_CARRY = 8
HEADS = 4
HEAD_DK = 128
GLA_DV = 256
GLA_RANK = 16
GLA_TAU = 16.0
GLA_SUB = 16
N_EXPERTS = 32
TOP_K = 4
SWIGLU_LIMIT = 7.0
SWIGLU_ALPHA = 1.702

SEQ_CHUNK = 128
FFN_TILE = 256
COMBINE_TILE = 128
DISPATCH_GROUP = 128


def _dot(a, b):
    return jnp.dot(a, b, preferred_element_type=F32)


def _dot_nt(a, b):
    return lax.dot_general(a, b, (((1,), (1,)), ((), ())), preferred_element_type=F32)


def _split3(a):
    a1 = a.astype(BF16)
    r1 = a - a1.astype(F32)
    a2 = r1.astype(BF16)
    r2 = r1 - a2.astype(F32)
    return a1, a2, r2.astype(BF16)


def _dot01_left(m01, a):
    a1, a2, a3 = _split3(a)
    return _dot(m01, a1) + _dot(m01, a2) + _dot(m01, a3)


def _dot01_right(a, m01):
    a1, a2, a3 = _split3(a)
    return _dot(a1, m01) + _dot(a2, m01) + _dot(a3, m01)


def _log_sigmoid(x):
    return jnp.minimum(x, 0.0) - jnp.log1p(jnp.exp(-jnp.abs(x)))


def _sigmoid(x):
    return 1.0 / (1.0 + jnp.exp(-x))


def _params(sem):
    return pltpu.CompilerParams(dimension_semantics=sem, vmem_limit_bytes=VMEM_LIMIT)


def _proj_kernel(x_ref, w_ref, o_ref, xb_ref):
    @pl.when(pl.program_id(1) == 0)
    def _():
        xb_ref[...] = x_ref[...].astype(BF16)

    o_ref[...] = _dot(xb_ref[...], w_ref[...].astype(BF16))


def _project(x2d, w3d, li, n_cols, tn=512):
    T, D = x2d.shape
    tm = min(1024, T)
    return pl.pallas_call(
        _proj_kernel,
        out_shape=jax.ShapeDtypeStruct((T, n_cols), F32),
        grid=(T // tm, n_cols // tn),
        in_specs=[pl.BlockSpec((tm, D), lambda i, j: (i, 0)),
                  pl.BlockSpec((None, D, tn), lambda i, j: (li, 0, j))],
        out_specs=pl.BlockSpec((tm, tn), lambda i, j: (i, j)),
        scratch_shapes=[pltpu.VMEM((tm, D), BF16)],
        compiler_params=_params(("arbitrary", "arbitrary")),
        name="in_proj",
    )(x2d, w3d)


def _even_mixer_kernel(p_ref, x_ref, wgc_ref, wgr_ref, gbr_ref, gbc_ref, poolw_ref, pscale_ref,
                       convw_ref, convb_ref, mlnorm_ref, mix_ref,
                       ubuf, qkbuf, c_st, n_st, m_st):
    L = SEQ_CHUNK
    c = pl.program_id(1)

    @pl.when(c == 0)
    def _():
        ubuf[0:POOL_CARRY, :] = jnp.zeros((POOL_CARRY, 512), F32)
        qkbuf[0:CONV_CARRY, :] = jnp.zeros((CONV_CARRY, 1024), F32)
        c_st[...] = jnp.zeros_like(c_st)
        n_st[...] = jnp.zeros_like(n_st)
        m_st[...] = jnp.zeros_like(m_st)

    ubuf[POOL_CARRY:POOL_CARRY + L, :] = p_ref[:, 0:512]
    pos1 = lax.broadcasted_iota(I32, (L, 1), 0) + c * L + 1
    for g, w in enumerate(POOL_WINDOWS):
        lo = g * 128
        acc = ubuf[pl.ds(POOL_CARRY, L), lo:lo + 128]
        cur = acc
        for j in range(1, w):
            acc = acc + ubuf[pl.ds(POOL_CARRY - j, L), lo:lo + 128]
        cnt = jnp.minimum(pos1, w).astype(F32)
        d = acc / cnt - cur
        yp = _dot(d.astype(BF16), poolw_ref[g].astype(BF16)) * pscale_ref[:, lo:lo + 128]
        mix_ref[:, lo:lo + 128] = yp.astype(BF16)
    ubuf[0:POOL_CARRY, :] = ubuf[L:L + POOL_CARRY, :]

    qkbuf[CONV_CARRY:CONV_CARRY + L, :] = p_ref[:, 512:1536]
    acc = jnp.zeros((L, 1024), F32) + convb_ref[...]
    for j in range(CONV_K):
        acc = acc + convw_ref[j:j + 1, :] * qkbuf[pl.ds(CONV_CARRY - (CONV_K - 1) + j, L), :]
    qk = acc * _sigmoid(acc)
    qkbuf[0:CONV_CARRY, :] = qkbuf[L:L + CONV_CARRY, :]

    xb = x_ref[...].astype(BF16)
    gcol = _dot(xb, wgc_ref[...].astype(BF16)) + gbr_ref[...]
    grow = _dot_nt(wgr_ref[...].astype(BF16), xb) + gbc_ref[:, 0:1]
    ri = lax.broadcasted_iota(I32, (L, L), 0)
    ci = lax.broadcasted_iota(I32, (L, L), 1)
    causal = ci <= ri
    tri_l = jnp.where(causal, 1.0, 0.0).astype(BF16)
    tri_u = jnp.where(ri <= ci, 1.0, 0.0).astype(BF16)
    b_col = _dot01_left(tri_l, _log_sigmoid(gcol))
    b_row = _dot01_right(_log_sigmoid(grow), tri_u)

    kscale = HEAD_DK ** -0.5
    for h in range(HEADS):
        lo = h * 128
        qh = qk[:, lo:lo + 128]
        kh = qk[:, 512 + lo:512 + lo + 128] * kscale
        vh = p_ref[:, 1536 + lo:1536 + lo + 128]
        og = p_ref[:, 2048 + lo:2048 + lo + 128]
        b_c = b_col[:, 4 + h:5 + h]
        b_r = b_row[4 + h:5 + h, :]
        ig_c = gcol[:, h:h + 1]
        ig_r = grow[h:h + 1, :]
        g = b_r[:, L - 1:L]
        m_prev = m_st[h][:, 0:1]
        qb = qh.astype(BF16)
        kb = kh.astype(BF16)
        vb = vh.astype(BF16)

        dmat = jnp.where(causal, b_c - b_r + ig_r, NEG)
        m_inter = b_c + m_prev
        m_t = jnp.maximum(m_inter, jnp.max(dmat, axis=1, keepdims=True))
        sts = _dot_nt(qb, kb) * jnp.exp(dmat - m_t)
        sc = jnp.exp(m_inter - m_t)
        num = _dot(sts.astype(BF16), vb) + sc * _dot(qb, c_st[h].astype(BF16))
        qn = jnp.sum(qh * n_st[h], axis=1, keepdims=True)
        den = jnp.sum(sts, axis=1, keepdims=True) + sc * qn
        hh = num / jnp.maximum(jnp.abs(den), jnp.exp(-m_t))

        a_c = g - b_c + ig_c
        a_r = g - b_r + ig_r
        m_new = jnp.maximum(g + m_prev, jnp.max(a_r, axis=1, keepdims=True))
        decay = jnp.exp(g + m_prev - m_new)
        wk = jnp.exp(a_c - m_new) * kh
        c_st[h] = decay * c_st[h] + _dot(wk.T.astype(BF16), vb)
        n_st[h] = decay * n_st[h] + jnp.sum(wk, axis=0, keepdims=True)
        m_st[h] = jnp.broadcast_to(m_new, (1, LANES))

        mu = jnp.mean(hh, axis=1, keepdims=True)
        hc = hh - mu
        var = jnp.mean(hc * hc, axis=1, keepdims=True)
        hn = hc * lax.rsqrt(var + LN_EPS) * mlnorm_ref[:, lo:lo + 128]
        mix_ref[:, 512 + lo:512 + lo + 128] = (hn * _sigmoid(og)).astype(BF16)


def _even_mixer(p, x2d, B, S, w_in, li, pool_w, pool_scale, conv_w, conv_b, i_bias, f_bias, ml_norm):
    T, D = x2d.shape
    L = SEQ_CHUNK
    nc = S // L
    wg = w_in[li, :, 2560:2568]
    wgc = jnp.pad(wg, ((0, 0), (0, LANES - 8)))
    wgr = wg.T
    gb = jnp.concatenate([i_bias[li], f_bias[li]])
    gbr = jnp.pad(gb, (0, LANES - 8)).reshape(1, LANES)
    gbc = jnp.broadcast_to(gb.reshape(8, 1), (8, LANES))
    row = lambda b, c: (b * nc + c, 0)
    full2 = lambda b, c: (0, 0)
    return pl.pallas_call(
        _even_mixer_kernel,
        out_shape=jax.ShapeDtypeStruct((T, D), BF16),
        grid=(B, nc),
        in_specs=[pl.BlockSpec((L, 2560), row),
                  pl.BlockSpec((L, D), row),
                  pl.BlockSpec((D, LANES), full2),
                  pl.BlockSpec((8, D), full2),
                  pl.BlockSpec((1, LANES), full2),
                  pl.BlockSpec((8, LANES), full2),
                  pl.BlockSpec((None, 4, 128, 128), lambda b, c: (li, 0, 0, 0)),
                  pl.BlockSpec((None, 1, 512), lambda b, c: (li, 0, 0)),
                  pl.BlockSpec((None, CONV_K, 1024), lambda b, c: (li, 0, 0)),
                  pl.BlockSpec((None, 1, 1024), lambda b, c: (li, 0, 0)),
                  pl.BlockSpec((None, 1, 512), lambda b, c: (li, 0, 0))],
        out_specs=pl.BlockSpec((L, D), row),
        scratch_shapes=[pltpu.VMEM((L + POOL_CARRY, 512), F32),
                        pltpu.VMEM((L + CONV_CARRY, 1024), F32),
                        pltpu.VMEM((HEADS, 128, 128), F32),
                        pltpu.VMEM((HEADS, 1, 128), F32),
                        pltpu.VMEM((HEADS, 1, LANES), F32)],
        compiler_params=_params(("arbitrary", "arbitrary")),
        name="even_mixer",
    )(p, x2d, wgc, wgr, gbr, gbc, pool_w, pool_scale.reshape(-1, 1, 512), conv_w,
      conv_b.reshape(-1, 1, 1024), ml_norm.reshape(-1, 1, 512))


def _odd_mixer_kernel(p_ref, x_ref, wr_ref, w2_ref, gb_ref, gnorm_ref, mix_ref, s_st):
    L = SEQ_CHUNK
    c = pl.program_id(1)

    @pl.when(c == 0)
    def _():
        s_st[...] = jnp.zeros_like(s_st)

    xb = x_ref[...].astype(BF16)
    glr = _dot(xb, wr_ref[...].astype(BF16))
    z = _dot(glr.astype(BF16), w2_ref[...].astype(BF16)) + gb_ref[...]
    lg = _log_sigmoid(z) / GLA_TAU

    ri = lax.broadcasted_iota(I32, (L, L), 0)
    ci = lax.broadcasted_iota(I32, (L, L), 1)
    one = lambda m: jnp.where(m, 1.0, 0.0).astype(BF16)
    causal = ci <= ri
    blk0 = (ri // GLA_SUB) * GLA_SUB
    m_in = one(causal & (ci >= blk0))
    m_all = one(causal)
    lg3 = _split3(lg)
    cum = lambda m01: _dot(m01, lg3[0]) + _dot(m01, lg3[1]) + _dot(m01, lg3[2])
    b_in = cum(m_in)
    b_all = cum(m_all)

    levels = []
    for hs in (64, 32, 16):
        ps = 2 * hs
        pair0 = (ri // ps) * ps
        refrow = pair0 + hs - 1
        upper = (ri - pair0) >= hs
        dq = cum(one(upper & (ci > refrow) & causal))
        dk = cum(one((~upper) & (ci > ri) & (ci <= refrow)))
        rcol = lax.broadcasted_iota(I32, (L, HEAD_DK), 0)
        up_c = ((rcol % ps) >= hs)
        same_pair = (ri // ps) == (ci // ps)
        levels.append((dq, dk, up_c, same_pair, ps))

    rows16 = lax.broadcasted_iota(I32, (GLA_SUB, HEAD_DK), 0)
    lane16 = lax.broadcasted_iota(I32, (GLA_SUB, LANES), 1)
    eye = ri == ci
    qscale = HEAD_DK ** -0.5
    for h in range(HEADS):
        lo = h * 128
        qh = p_ref[:, lo:lo + 128] * qscale
        kh = p_ref[:, 512 + lo:512 + lo + 128]
        vb = p_ref[:, 1024 + h * GLA_DV:1024 + (h + 1) * GLA_DV].astype(BF16)
        rg = p_ref[:, 2048 + h * GLA_DV:2048 + (h + 1) * GLA_DV]
        bh = b_all[:, lo:lo + 128]
        bin_h = b_in[:, lo:lo + 128]

        o = _dot((qh * jnp.exp(bh)).astype(BF16), s_st[h].astype(BF16))

        amat = jnp.zeros((L, L), F32)
        for dq, dk, up_c, same_pair, ps in levels:
            qt = jnp.where(up_c, qh * jnp.exp(dq[:, lo:lo + 128]), 0.0)
            kt = jnp.where(up_c, 0.0, kh * jnp.exp(dk[:, lo:lo + 128]))
            a_lev = _dot_nt(qt.astype(BF16), kt.astype(BF16))
            amat = amat + (a_lev if ps == L else jnp.where(same_pair, a_lev, 0.0))

        blocks = []
        for blk in range(L // GLA_SUB):
            r0 = blk * GLA_SUB
            qb16 = qh[r0:r0 + GLA_SUB, :]
            kb16 = kh[r0:r0 + GLA_SUB, :]
            bb16 = bin_h[r0:r0 + GLA_SUB, :]
            ablk = jnp.zeros((GLA_SUB, LANES), F32)
            for s in range(GLA_SUB):
                e = jnp.where(rows16 >= s, bb16 - bb16[s:s + 1, :], NEG)
                pr = qb16 * kb16[s:s + 1, :] * jnp.exp(e)
                a = jnp.sum(pr, axis=1, keepdims=True)
                ablk = jnp.where(lane16 == (r0 + s), a, ablk)
            blocks.append(ablk)
        amat = amat + jnp.concatenate(blocks, axis=0)

        o = o + _dot(amat.astype(BF16), vb)

        bl = bh[L - 1:L, :]
        kdec = kh * jnp.exp(bl - bh)
        ecol = jnp.sum(jnp.where(eye, jnp.exp(bl), 0.0), axis=1, keepdims=True)
        s_st[h] = ecol * s_st[h] + _dot(kdec.T.astype(BF16), vb)

        mu = jnp.mean(o, axis=1, keepdims=True)
        oc = o - mu
        var = jnp.mean(oc * oc, axis=1, keepdims=True)
        on = oc * lax.rsqrt(var + LN_EPS) * gnorm_ref[:, h * GLA_DV:(h + 1) * GLA_DV]
        mix_ref[:, h * GLA_DV:(h + 1) * GLA_DV] = (on * (rg * _sigmoid(rg))).astype(BF16)


def _odd_mixer(p, x2d, B, S, w_in, li, gla_w2, gla_b, gla_norm):
    T, D = x2d.shape
    L = SEQ_CHUNK
    nc = S // L
    wr = jnp.pad(w_in[li, :, 3072:3072 + GLA_RANK], ((0, 0), (0, LANES - GLA_RANK)))
    w2 = jnp.pad(gla_w2[li], ((0, LANES - GLA_RANK), (0, 0)))
    row = lambda b, c: (b * nc + c, 0)
    full2 = lambda b, c: (0, 0)
    return pl.pallas_call(
        _odd_mixer_kernel,
        out_shape=jax.ShapeDtypeStruct((T, D), BF16),
        grid=(B, nc),
        in_specs=[pl.BlockSpec((L, 3072), row),
                  pl.BlockSpec((L, D), row),
                  pl.BlockSpec((D, LANES), full2),
                  pl.BlockSpec((LANES, 512), full2),
                  pl.BlockSpec((None, 1, 512), lambda b, c: (li, 0, 0)),
                  pl.BlockSpec((None, 1, 1024), lambda b, c: (li, 0, 0))],
        out_specs=pl.BlockSpec((L, D), row),
        scratch_shapes=[pltpu.VMEM((HEADS, HEAD_DK, GLA_DV), F32)],
        compiler_params=_params(("arbitrary", "arbitrary")),
        name="odd_mixer",
    )(p, x2d, wr, w2, gla_b.reshape(-1, 1, 512), gla_norm.reshape(-1, 1, 1024))


def _post_kernel(mix_ref, x_ref, wout_ref, g_ref, b_ref, rw_ref, rb_ref,
                 x1_ref, x1r_ref, eid_ref, rank_ref, gate_ref, cnt_ref,
                 wb_ref, carry_ref):
    tm = mix_ref.shape[0]
    i = pl.program_id(0)

    @pl.when(i == 0)
    def _():
        wb_ref[...] = wout_ref[...].astype(BF16)
        carry_ref[...] = jnp.zeros_like(carry_ref)

    y = _dot(mix_ref[...], wb_ref[...]) + ALPHA * x_ref[...]
    mu = jnp.mean(y, axis=1, keepdims=True)
    yc = y - mu
    var = jnp.mean(yc * yc, axis=1, keepdims=True)
    x1 = yc * lax.rsqrt(var + LN_EPS) * g_ref[...] + b_ref[...]
    x1_ref[...] = x1
    for c in range(SUBLANES):
        x1r_ref[pl.ds(c, tm, stride=SUBLANES), :] = x1[:, c * LANES:(c + 1) * LANES]

    hi = x1.astype(BF16)
    lo = (x1 - hi.astype(F32)).astype(BF16)
    rw = rw_ref[...]
    whi = rw.astype(BF16)
    wlo = (rw - whi.astype(F32)).astype(BF16)
    logits = _dot(hi, whi) + _dot(lo, whi) + _dot(hi, wlo) + rb_ref[...]

    lane = lax.broadcasted_iota(I32, (tm, LANES), 1)
    lane_f = lane.astype(F32)
    l = logits
    vals, ids, ohs = [], [], []
    for _ in range(TOP_K):
        m = jnp.max(l, axis=1, keepdims=True)
        idx = jnp.min(jnp.where(l == m, lane_f, float(LANES)), axis=1, keepdims=True)
        oh = lane_f == idx
        vals.append(m)
        ids.append(idx)
        ohs.append(oh)
        l = jnp.where(oh, -jnp.inf, l)
    es = [jnp.exp(v - vals[0]) for v in vals]
    tot = es[0] + es[1] + es[2] + es[3]

    onehot = jnp.zeros((tm, LANES), F32)
    for oh in ohs:
        onehot = onehot + jnp.where(oh, 1.0, 0.0)
    ri = lax.broadcasted_iota(I32, (tm, tm), 0)
    ci = lax.broadcasted_iota(I32, (tm, tm), 1)
    strict = jnp.where(ci < ri, 1.0, 0.0).astype(BF16)
    before = _dot(strict, onehot.astype(BF16)) + carry_ref[...]
    carry_ref[...] = carry_ref[...] + jnp.sum(onehot, axis=0, keepdims=True)
    cnt_ref[...] = carry_ref[...]

    eid = jnp.zeros((tm, LANES), F32)
    rank = jnp.zeros((tm, LANES), F32)
    gate = jnp.zeros((tm, LANES), F32)
    for k in range(TOP_K):
        rk = jnp.sum(jnp.where(ohs[k], before, 0.0), axis=1, keepdims=True)
        sel = lane == k
        eid = jnp.where(sel, ids[k], eid)
        rank = jnp.where(sel, rk, rank)
        gate = jnp.where(sel, es[k] / tot, gate)
    eid_ref[...] = eid.astype(I32)
    rank_ref[...] = rank.astype(I32)
    gate_ref[...] = gate


def _post(mix, x2d, w_out, li, ln_g, ln_b, router_w, router_b, layer):
    T, D = x2d.shape
    tm = min(256, T)
    rw = jnp.pad(router_w[layer], ((0, 0), (0, LANES - N_EXPERTS)))
    rb = jnp.pad(router_b[layer], (0, LANES - N_EXPERTS), constant_values=NEG).reshape(1, LANES)
    row = lambda i: (i, 0)
    full2 = lambda i: (0, 0)
    lane_out = jax.ShapeDtypeStruct((T, LANES), I32)
    return pl.pallas_call(
        _post_kernel,
        out_shape=(jax.ShapeDtypeStruct((T, D), F32),
                   jax.ShapeDtypeStruct((T * SUBLANES, LANES), F32),
                   lane_out, lane_out,
                   jax.ShapeDtypeStruct((T, LANES), F32),
                   jax.ShapeDtypeStruct((1, LANES), F32)),
        grid=(T // tm,),
        in_specs=[pl.BlockSpec((tm, D), row),
                  pl.BlockSpec((tm, D), row),
                  pl.BlockSpec((None, D, D), lambda i: (li, 0, 0)),
                  pl.BlockSpec((None, 1, D), lambda i: (layer, 0, 0)),
                  pl.BlockSpec((None, 1, D), lambda i: (layer, 0, 0)),
                  pl.BlockSpec((D, LANES), full2),
                  pl.BlockSpec((1, LANES), full2)],
        out_specs=(pl.BlockSpec((tm, D), row),
                   pl.BlockSpec((tm * SUBLANES, LANES), row),
                   pl.BlockSpec((tm, LANES), row),
                   pl.BlockSpec((tm, LANES), row),
                   pl.BlockSpec((tm, LANES), row),
                   pl.BlockSpec((1, LANES), full2)),
        scratch_shapes=[pltpu.VMEM((D, D), BF16), pltpu.VMEM((1, LANES), F32)],
        compiler_params=_params(("arbitrary",)),
        name="out_proj_ln_router",
    )(mix, x2d, w_out, ln_g.reshape(-1, 1, D), ln_b.reshape(-1, 1, D), rw, rb)


def _pos_kernel(eid_ref, rank_ref, off_ref, pos_ref):
    tm = eid_ref.shape[0]
    lane = lax.broadcasted_iota(I32, (tm, LANES), 1)
    lane_f = lane.astype(F32)
    eid = eid_ref[...].astype(F32)
    off = off_ref[...]
    pos = jnp.zeros((tm, LANES), F32)
    for k in range(TOP_K):
        ek = jnp.sum(jnp.where(lane == k, eid, 0.0), axis=1, keepdims=True)
        ok = jnp.sum(jnp.where(lane_f == ek, off, 0.0), axis=1, keepdims=True)
        pos = jnp.where(lane == k, ok, pos)
    pos_ref[...] = pos.astype(I32) + rank_ref[...]


def _positions(eid, rank, off_row):
    T = eid.shape[0]
    tm = min(1024, T)
    row = lambda i: (i, 0)
    return pl.pallas_call(
        _pos_kernel,
        out_shape=jax.ShapeDtypeStruct((T, LANES), I32),
        grid=(T // tm,),
        in_specs=[pl.BlockSpec((tm, LANES), row), pl.BlockSpec((tm, LANES), row),
                  pl.BlockSpec((1, LANES), lambda i: (0, 0))],
        out_specs=pl.BlockSpec((tm, LANES), row),
        compiler_params=_params(("arbitrary",)),
        name="route_positions",
    )(eid, rank, off_row)


def _dispatch_kernel(pos_ref, cnt_ref, off_ref, nv_ref, x_ref, xs_ref, stage, zbuf, sems, zsem, *, n_groups):
    G = x_ref.shape[0] // SUBLANES
    gi = pl.program_id(0)
    slot = gi % 2

    def row_copy(tt, k):
        src = stage.at[slot, pl.ds(pl.multiple_of(tt * SUBLANES, SUBLANES), SUBLANES), :]
        p = pos_ref[(gi * G + tt) * TOP_K + k]
        dst = xs_ref.at[pl.ds(pl.multiple_of(p * SUBLANES, SUBLANES), SUBLANES), :]
        return pltpu.make_async_copy(src, dst, sems.at[slot])

    def group_wait(s):
        n = G * TOP_K * SUBLANES
        pltpu.make_async_copy(xs_ref.at[pl.ds(0, n), :], xs_ref.at[pl.ds(0, n), :], sems.at[s]).wait()

    @pl.when(gi >= 2)
    def _():
        group_wait(slot)

    stage[slot] = x_ref[...]

    def tok(tt, carry):
        for k in range(TOP_K):
            row_copy(tt, k).start()
        return carry

    lax.fori_loop(0, G, tok, 0)

    @pl.when(gi == n_groups - 1)
    def _():
        group_wait(slot)
        if n_groups >= 2:
            group_wait(1 - slot)

        zbuf[...] = jnp.zeros_like(zbuf)

        def pad_expert(e, carry):
            cnt = cnt_ref[e]
            npad = (FFN_TILE - cnt % FFN_TILE) % FFN_TILE
            base = off_ref[e] + cnt

            def zcopy(r):
                dst = xs_ref.at[pl.ds(pl.multiple_of((base + r) * SUBLANES, SUBLANES), SUBLANES), :]
                return pltpu.make_async_copy(zbuf.at[pl.ds(0, SUBLANES), :], dst, zsem)

            def start(r, c2):
                zcopy(r).start()
                return c2

            def wait(r, c2):
                zcopy(r).wait()
                return c2

            lax.fori_loop(0, npad, start, 0)
            lax.fori_loop(0, npad, wait, 0)
            return carry

        lax.fori_loop(0, N_EXPERTS, pad_expert, 0)

        tile_rows = FFN_TILE * SUBLANES
        n_tiles = xs_ref.shape[0] // tile_rows

        def tcopy(j):
            dst = xs_ref.at[pl.ds(pl.multiple_of(j * tile_rows, tile_rows), tile_rows), :]
            return pltpu.make_async_copy(zbuf, dst, zsem)

        def tstart(j, c2):
            tcopy(j).start()
            return c2

        def twait(j, c2):
            tcopy(j).wait()
            return c2

        lax.fori_loop(nv_ref[0], n_tiles, tstart, 0)
        lax.fori_loop(nv_ref[0], n_tiles, twait, 0)


def _dispatch(pos_flat, counts, offs, n_valid, x1r, n_rows):
    T = x1r.shape[0] // SUBLANES
    G = min(DISPATCH_GROUP, T)
    return pl.pallas_call(
        functools.partial(_dispatch_kernel, n_groups=T // G),
        out_shape=jax.ShapeDtypeStruct((n_rows * SUBLANES, LANES), F32),
        grid_spec=pltpu.PrefetchScalarGridSpec(
            num_scalar_prefetch=4,
            grid=(T // G,),
            in_specs=[pl.BlockSpec((G * SUBLANES, LANES), lambda i, *_: (i, 0))],
            out_specs=pl.BlockSpec(memory_space=pl.ANY),
            scratch_shapes=[pltpu.VMEM((2, G * SUBLANES, LANES), F32),
                            pltpu.VMEM((FFN_TILE * SUBLANES, LANES), F32),
                            pltpu.SemaphoreType.DMA((2,)),
                            pltpu.SemaphoreType.DMA(())]),
        compiler_params=_params(("arbitrary",)),
        name="dispatch_rows",
    )(pos_flat, counts, offs, n_valid, x1r)


def _ffn_kernel(te_ref, nv_ref, xs_ref, wgu_ref, wd_ref, bg_ref, bl_ref, bd_ref, y_ref,
                xb_ref, wg_s, wl_s, wd_s):
    tm = FFN_TILE
    j = pl.program_id(0)
    valid = j < nv_ref[0]
    first = jnp.logical_or(j == 0, te_ref[j] != te_ref[jnp.maximum(j - 1, 0)])

    @pl.when(jnp.logical_and(valid, first))
    def _():
        pi = lax.broadcasted_iota(I32, (256, 256), 0)
        pj = lax.broadcasted_iota(I32, (256, 256), 1)
        src = jnp.where(pj < 128, 2 * pj, 2 * (pj - 128) + 1)
        perm = jnp.where(pi == src, 1.0, 0.0).astype(BF16)
        for blk in range(8):
            r = _dot(wgu_ref[:, blk * 256:(blk + 1) * 256].astype(BF16), perm)
            wg_s[:, blk * 128:(blk + 1) * 128] = r[:, 0:128].astype(BF16)
            wl_s[:, blk * 128:(blk + 1) * 128] = r[:, 128:256].astype(BF16)
        wd_s[...] = wd_ref[...].astype(BF16)

    @pl.when(valid)
    def _():
        for c in range(SUBLANES):
            xb_ref[:, c * LANES:(c + 1) * LANES] = xs_ref[pl.ds(c, tm, stride=SUBLANES), :].astype(BF16)
        xb = xb_ref[...]
        hg = _dot(xb, wg_s[...]) + bg_ref[...]
        hl = _dot(xb, wl_s[...]) + bl_ref[...]
        glu = jnp.minimum(hg, SWIGLU_LIMIT)
        lin = jnp.clip(hl, -SWIGLU_LIMIT, SWIGLU_LIMIT)
        act = glu * _sigmoid(SWIGLU_ALPHA * glu) * (lin + 1.0)
        y = _dot(act.astype(BF16), wd_s[...]) + bd_ref[...]
        for c in range(SUBLANES):
            y_ref[pl.ds(c, tm, stride=SUBLANES), :] = y[:, c * LANES:(c + 1) * LANES]

    @pl.when(jnp.logical_not(valid))
    def _():
        y_ref[...] = jnp.zeros_like(y_ref)


def _ffn(tile_expert, n_valid, xs, w_gate_up, w_down, bg, bl, b_down, layer, n_tiles):
    tm = FFN_TILE
    D = w_down.shape[-1]
    tile = lambda j, te, nv: (jnp.minimum(j, nv[0] - 1), 0)
    out_tile = lambda j, te, nv: (j, 0)
    return pl.pallas_call(
        _ffn_kernel,
        out_shape=jax.ShapeDtypeStruct(xs.shape, F32),
        grid_spec=pltpu.PrefetchScalarGridSpec(
            num_scalar_prefetch=2,
            grid=(n_tiles,),
            in_specs=[pl.BlockSpec((tm * SUBLANES, LANES), tile),
                      pl.BlockSpec((None, None, D, 2 * D), lambda j, te, nv: (layer, te[j], 0, 0)),
                      pl.BlockSpec((None, None, D, D), lambda j, te, nv: (layer, te[j], 0, 0)),
                      pl.BlockSpec((None, None, 1, D), lambda j, te, nv: (layer, te[j], 0, 0)),
                      pl.BlockSpec((None, None, 1, D), lambda j, te, nv: (layer, te[j], 0, 0)),
                      pl.BlockSpec((None, None, 1, D), lambda j, te, nv: (layer, te[j], 0, 0))],
            out_specs=pl.BlockSpec((tm * SUBLANES, LANES), out_tile),
            scratch_shapes=[pltpu.VMEM((tm, D), BF16),
                            pltpu.VMEM((D, D), BF16),
                            pltpu.VMEM((D, D), BF16),
                            pltpu.VMEM((D, D), BF16)]),
        compiler_params=_params(("arbitrary",)),
        name="expert_ffn",
    )(tile_expert, n_valid, xs, w_gate_up, w_down, bg, bl, b_down)


def _combine_kernel(pos_ref, y_ref, gate_ref, x1_ref, g_ref, b_ref, o_ref, ybuf, zs, sems):
    tc = COMBINE_TILE
    i = pl.program_id(0)
    n = pl.num_programs(0)

    def issue(tile, slot):
        def tok(tt, carry):
            for k in range(TOP_K):
                p = pos_ref[(tile * tc + tt) * TOP_K + k]
                src = y_ref.at[pl.ds(pl.multiple_of(p * SUBLANES, SUBLANES), SUBLANES), :]
                dst = ybuf.at[slot, pl.ds(pl.multiple_of((k * tc + tt) * SUBLANES, SUBLANES), SUBLANES), :]
                pltpu.make_async_copy(src, dst, sems.at[slot]).start()
            return carry

        lax.fori_loop(0, tc, tok, 0)

    @pl.when(i == 0)
    def _():
        issue(0, 0)

    @pl.when(i + 1 < n)
    def _():
        issue(i + 1, (i + 1) % 2)

    slot = i % 2
    rows = TOP_K * tc * SUBLANES
    pltpu.make_async_copy(y_ref.at[pl.ds(0, rows), :], ybuf.at[slot], sems.at[slot]).wait()

    gate = gate_ref[...]
    s1 = jnp.zeros((tc, 1), F32)
    for c in range(SUBLANES):
        z = ALPHA * x1_ref[:, c * LANES:(c + 1) * LANES]
        for k in range(TOP_K):
            z = z + gate[:, k:k + 1] * ybuf[slot, pl.ds(k * tc * SUBLANES + c, tc, stride=SUBLANES), :]
        zs[:, c * LANES:(c + 1) * LANES] = z
        s1 = s1 + jnp.sum(z, axis=1, keepdims=True)
    zv = zs[...]
    mu = s1 / zv.shape[1]
    zc = zv - mu
    var = jnp.mean(zc * zc, axis=1, keepdims=True)
    o_ref[...] = zc * lax.rsqrt(var + LN_EPS) * g_ref[...] + b_ref[...]


def _combine(pos_flat, y, gate, x1, ln_g, ln_b, layer):
    T, D = x1.shape
    tc = COMBINE_TILE
    row = lambda i, pos: (i, 0)
    return pl.pallas_call(
        _combine_kernel,
        out_shape=jax.ShapeDtypeStruct((T, D), F32),
        grid_spec=pltpu.PrefetchScalarGridSpec(
            num_scalar_prefetch=1,
            grid=(T // tc,),
            in_specs=[pl.BlockSpec(memory_space=pl.ANY),
                      pl.BlockSpec((tc, LANES), row),
                      pl.BlockSpec((tc, D), row),
                      pl.BlockSpec((None, 1, D), lambda i, pos: (layer, 0, 0)),
                      pl.BlockSpec((None, 1, D), lambda i, pos: (layer, 0, 0))],
            out_specs=pl.BlockSpec((tc, D), row),
            scratch_shapes=[pltpu.VMEM((2, TOP_K * tc * SUBLANES, LANES), F32),
                            pltpu.VMEM((tc, D), F32),
                            pltpu.SemaphoreType.DMA((2,))]),
        compiler_params=_params(("arbitrary",)),
        name="combine_ln",
    )(pos_flat, y, gate, x1, ln_g.reshape(-1, 1, D), ln_b.reshape(-1, 1, D))


def _moe(x1, x1r, eid, rank, gate, cnt, layer, w_gate_up, b_gate_up, w_down, b_down, ln_g, ln_b):
    T, D = x1.shape
    tm = FFN_TILE
    n_tiles = (T * TOP_K) // tm + N_EXPERTS
    n_rows = n_tiles * tm
    counts = cnt[0, :N_EXPERTS].astype(I32)
    tiles_e = (counts + tm - 1) // tm
    tile_end = jnp.cumsum(tiles_e)
    offs = (tile_end - tiles_e) * tm
    n_valid = tile_end[-1]
    jj = jnp.minimum(jnp.arange(n_tiles, dtype=I32), n_valid - 1)
    tile_expert = jnp.minimum(jnp.sum(tile_end[None, :] <= jj[:, None], axis=1), N_EXPERTS - 1).astype(I32)
    off_row = jnp.pad(offs, (0, LANES - N_EXPERTS)).astype(F32).reshape(1, LANES)

    pos = _positions(eid, rank, off_row)
    pos_flat = pos[:, :TOP_K].reshape(-1)
    nv = n_valid.reshape(1).astype(I32)
    xs = _dispatch(pos_flat, counts, offs.astype(I32), nv, x1r, n_rows)
    bgu = b_gate_up.reshape(DEPTH, N_EXPERTS, D, 2)
    bg = bgu[..., 0].reshape(DEPTH, N_EXPERTS, 1, D)
    bl = bgu[..., 1].reshape(DEPTH, N_EXPERTS, 1, D)
    y = _ffn(tile_expert, nv, xs, w_gate_up, w_down, bg, bl,
             b_down.reshape(DEPTH, N_EXPERTS, 1, D), layer, n_tiles)
    return _combine(pos_flat, y, gate, x1, ln_g, ln_b, layer)


def kernel(x, even_w_in, pool_w, pool_scale, conv_w, conv_b, i_bias, f_bias, ml_norm, even_w_out, odd_w_in, gla_w2, gla_b, gla_norm, odd_w_out, ln1_g, ln1_b, ln2_g, ln2_b, router_w, router_b, w_gate_up, b_gate_up, w_down, b_down):
    B, S, D = x.shape
    xt = x.reshape(B * S, D)
    for layer in range(DEPTH):
        li = layer // 2
        if layer % 2 == 0:
            p = _project(xt, even_w_in, li, 2560)
            mix = _even_mixer(p, xt, B, S, even_w_in, li, pool_w, pool_scale, conv_w, conv_b,
                              i_bias, f_bias, ml_norm)
            w_out = even_w_out
        else:
            p = _project(xt, odd_w_in, li, 3072)
            mix = _odd_mixer(p, xt, B, S, odd_w_in, li, gla_w2, gla_b, gla_norm)
            w_out = odd_w_out
        x1, x1r, eid, rank, gate, cnt = _post(mix, xt, w_out, li, ln1_g, ln1_b, router_w, router_b, layer)
        xt = _moe(x1, x1r, eid, rank, gate, cnt, layer, w_gate_up, b_gate_up, w_down, b_down,
                  ln2_g, ln2_b)
    return xt.reshape(B, S, D)
```

```python
import functools

import jax
import jax.numpy as jnp
from jax import lax
from jax.experimental import pallas as pl
from jax.experimental.pallas import tpu as pltpu

F32 = jnp.float32
BF16 = jnp.bfloat16
I32 = jnp.int32

LANES = 128
SUBLANES = 8
VMEM_LIMIT = 56 * 1024 * 1024

DEPTH = 4
ALPHA = (2 * DEPTH) ** 0.25
LN_EPS = 1e-5
NEG = -1e30

POOL_WINDOWS = (2, 4, 8, 16)
POOL_CARRY = 16
CONV_K = 4
CONV_CARRY = 8
HEADS = 4
HEAD_DK = 128
GLA_DV = 256
GLA_RANK = 16
GLA_TAU = 16.0
GLA_SUB = 16
N_EXPERTS = 32
TOP_K = 4
SWIGLU_LIMIT = 7.0
SWIGLU_ALPHA = 1.702

SEQ_CHUNK = 128
FFN_TILE = 256
COMBINE_TILE = 128
DISPATCH_GROUP = 128


def _dot(a, b):
    return jnp.dot(a, b, preferred_element_type=F32)


def _dot_nt(a, b):
    return lax.dot_general(a, b, (((1,), (1,)), ((), ())), preferred_element_type=F32)


def _split3(a):
    a1 = a.astype(BF16)
    r1 = a - a1.astype(F32)
    a2 = r1.astype(BF16)
    r2 = r1 - a2.astype(F32)
    return a1, a2, r2.astype(BF16)


def _dot01_left(m01, a):
    a1, a2, a3 = _split3(a)
    return _dot(m01, a1) + _dot(m01, a2) + _dot(m01, a3)


def _dot01_right(a, m01):
    a1, a2, a3 = _split3(a)
    return _dot(a1, m01) + _dot(a2, m01) + _dot(a3, m01)


def _log_sigmoid(x):
    return jnp.minimum(x, 0.0) - jnp.log1p(jnp.exp(-jnp.abs(x)))


def _sigmoid(x):
    return 1.0 / (1.0 + jnp.exp(-x))


def _params(sem):
    return pltpu.CompilerParams(dimension_semantics=sem, vmem_limit_bytes=VMEM_LIMIT)


def _proj_kernel(x_ref, w_hbm, o_ref, wb_ref, stage, sems, *, li, n_cols, tn):
    @pl.when(pl.program_id(0) == 0)
    def _():
        n_chunks = n_cols // tn

        def chunk_copy(n):
            return pltpu.make_async_copy(w_hbm.at[li, :, pl.ds(n * tn, tn)], stage.at[n % 2], sems.at[n % 2])

        chunk_copy(0).start()
        for n in range(n_chunks):
            if n + 1 < n_chunks:
                chunk_copy(n + 1).start()
            chunk_copy(n).wait()
            wb_ref[:, n * tn:(n + 1) * tn] = stage[n % 2].astype(BF16)

    o_ref[...] = _dot(x_ref[...].astype(BF16), wb_ref[...])


def _project(x2d, w3d, li, n_cols, tn=512):
    T, D = x2d.shape
    tm = min(512, T)
    return pl.pallas_call(
        functools.partial(_proj_kernel, li=li, n_cols=n_cols, tn=tn),
        out_shape=jax.ShapeDtypeStruct((T, n_cols), F32),
        grid=(T // tm,),
        in_specs=[pl.BlockSpec((tm, D), lambda i: (i, 0)),
                  pl.BlockSpec(memory_space=pl.ANY)],
        out_specs=pl.BlockSpec((tm, n_cols), lambda i: (i, 0)),
        scratch_shapes=[pltpu.VMEM((D, n_cols), BF16),
                        pltpu.VMEM((2, D, tn), F32),
                        pltpu.SemaphoreType.DMA((2,))],
        compiler_params=_params(("arbitrary",)),
        name="in_proj",
    )(x2d, w3d)


def _even_mixer_kernel(p_ref, x_ref, wgc_ref, wgr_ref, gbr_ref, gbc_ref, poolw_ref, pscale_ref,
                       convw_ref, convb_ref, mlnorm_ref, mix_ref,
                       ubuf, qkbuf, c_st, n_st, m_st):
    L = SEQ_CHUNK
    c = pl.program_id(1)

    @pl.when(c == 0)
    def _():
        ubuf[0:POOL_CARRY, :] = jnp.zeros((POOL_CARRY, 512), F32)
        qkbuf[0:CONV_CARRY, :] = jnp.zeros((CONV_CARRY, 1024), F32)
        c_st[...] = jnp.zeros_like(c_st)
        n_st[...] = jnp.zeros_like(n_st)
        m_st[...] = jnp.zeros_like(m_st)

    ubuf[POOL_CARRY:POOL_CARRY + L, :] = p_ref[:, 0:512]
    pos1 = lax.broadcasted_iota(I32, (L, 1), 0) + c * L + 1
    for g, w in enumerate(POOL_WINDOWS):
        lo = g * 128
        acc = ubuf[pl.ds(POOL_CARRY, L), lo:lo + 128]
        cur = acc
        for j in range(1, w):
            acc = acc + ubuf[pl.ds(POOL_CARRY - j, L), lo:lo + 128]
        cnt = jnp.minimum(pos1, w).astype(F32)
        d = acc / cnt - cur
        yp = _dot(d.astype(BF16), poolw_ref[g].astype(BF16)) * pscale_ref[:, lo:lo + 128]
        mix_ref[:, lo:lo + 128] = yp.astype(BF16)
    ubuf[0:POOL_CARRY, :] = ubuf[L:L + POOL_CARRY, :]

    qkbuf[CONV_CARRY:CONV_CARRY + L, :] = p_ref[:, 512:1536]
    acc = jnp.zeros((L, 1024), F32) + convb_ref[...]
    for j in range(CONV_K):
        acc = acc + convw_ref[j:j + 1, :] * qkbuf[pl.ds(CONV_CARRY - (CONV_K - 1) + j, L), :]
    qk = acc * _sigmoid(acc)
    qkbuf[0:CONV_CARRY, :] = qkbuf[L:L + CONV_CARRY, :]

    xb = x_ref[...].astype(BF16)
    gcol = _dot(xb, wgc_ref[...].astype(BF16)) + gbr_ref[...]
    grow = _dot_nt(wgr_ref[...].astype(BF16), xb) + gbc_ref[:, 0:1]
    ri = lax.broadcasted_iota(I32, (L, L), 0)
    ci = lax.broadcasted_iota(I32, (L, L), 1)
    causal = ci <= ri
    tri_l = jnp.where(causal, 1.0, 0.0).astype(BF16)
    tri_u = jnp.where(ri <= ci, 1.0, 0.0).astype(BF16)
    b_col = _dot01_left(tri_l, _log_sigmoid(gcol))
    b_row = _dot01_right(_log_sigmoid(grow), tri_u)

    kscale = HEAD_DK ** -0.5
    for h in range(HEADS):
        lo = h * 128
        qh = qk[:, lo:lo + 128]
        kh = qk[:, 512 + lo:512 + lo + 128] * kscale
        vh = p_ref[:, 1536 + lo:1536 + lo + 128]
        og = p_ref[:, 2048 + lo:2048 + lo + 128]
        b_c = b_col[:, 4 + h:5 + h]
        b_r = b_row[4 + h:5 + h, :]
        ig_c = gcol[:, h:h + 1]
        ig_r = grow[h:h + 1, :]
        g = b_r[:, L - 1:L]
        m_prev = m_st[h][:, 0:1]
        qb = qh.astype(BF16)
        kb = kh.astype(BF16)
        vb = vh.astype(BF16)

        dmat = jnp.where(causal, b_c - b_r + ig_r, NEG)
        m_inter = b_c + m_prev
        m_t = jnp.maximum(m_inter, jnp.max(dmat, axis=1, keepdims=True))
        sts = _dot_nt(qb, kb) * jnp.exp(dmat - m_t)
        sc = jnp.exp(m_inter - m_t)
        num = _dot(sts.astype(BF16), vb) + sc * _dot(qb, c_st[h].astype(BF16))
        qn = jnp.sum(qh * n_st[h], axis=1, keepdims=True)
        den = jnp.sum(sts, axis=1, keepdims=True) + sc * qn
        hh = num / jnp.maximum(jnp.abs(den), jnp.exp(-m_t))

        a_c = g - b_c + ig_c
        a_r = g - b_r + ig_r
        m_new = jnp.maximum(g + m_prev, jnp.max(a_r, axis=1, keepdims=True))
        decay = jnp.exp(g + m_prev - m_new)
        wk = jnp.exp(a_c - m_new) * kh
        c_st[h] = decay * c_st[h] + _dot(wk.T.astype(BF16), vb)
        n_st[h] = decay * n_st[h] + jnp.sum(wk, axis=0, keepdims=True)
        m_st[h] = jnp.broadcast_to(m_new, (1, LANES))

        mu = jnp.mean(hh, axis=1, keepdims=True)
        hc = hh - mu
        var = jnp.mean(hc * hc, axis=1, keepdims=True)
        hn = hc * lax.rsqrt(var + LN_EPS) * mlnorm_ref[:, lo:lo + 128]
        mix_ref[:, 512 + lo:512 + lo + 128] = (hn * _sigmoid(og)).astype(BF16)


def _even_mixer(p, x2d, B, S, w_in, li, pool_w, pool_scale, conv_w, conv_b, i_bias, f_bias, ml_norm):
    T, D = x2d.shape
    L = SEQ_CHUNK
    nc = S // L
    wg = w_in[li, :, 2560:2568]
    wgc = jnp.pad(wg, ((0, 0), (0, LANES - 8)))
    wgr = wg.T
    gb = jnp.concatenate([i_bias[li], f_bias[li]])
    gbr = jnp.pad(gb, (0, LANES - 8)).reshape(1, LANES)
    gbc = jnp.broadcast_to(gb.reshape(8, 1), (8, LANES))
    row = lambda b, c: (b * nc + c, 0)
    full2 = lambda b, c: (0, 0)
    return pl.pallas_call(
        _even_mixer_kernel,
        out_shape=jax.ShapeDtypeStruct((T, D), BF16),
        grid=(B, nc),
        in_specs=[pl.BlockSpec((L, 2560), row),
                  pl.BlockSpec((L, D), row),
                  pl.BlockSpec((D, LANES), full2),
                  pl.BlockSpec((8, D), full2),
                  pl.BlockSpec((1, LANES), full2),
                  pl.BlockSpec((8, LANES), full2),
                  pl.BlockSpec((None, 4, 128, 128), lambda b, c: (li, 0, 0, 0)),
                  pl.BlockSpec((None, 1, 512), lambda b, c: (li, 0, 0)),
                  pl.BlockSpec((None, CONV_K, 1024), lambda b, c: (li, 0, 0)),
                  pl.BlockSpec((None, 1, 1024), lambda b, c: (li, 0, 0)),
                  pl.BlockSpec((None, 1, 512), lambda b, c: (li, 0, 0))],
        out_specs=pl.BlockSpec((L, D), row),
        scratch_shapes=[pltpu.VMEM((L + POOL_CARRY, 512), F32),
                        pltpu.VMEM((L + CONV_CARRY, 1024), F32),
                        pltpu.VMEM((HEADS, 128, 128), F32),
                        pltpu.VMEM((HEADS, 1, 128), F32),
                        pltpu.VMEM((HEADS, 1, LANES), F32)],
        compiler_params=_params(("arbitrary", "arbitrary")),
        name="even_mixer",
    )(p, x2d, wgc, wgr, gbr, gbc, pool_w, pool_scale.reshape(-1, 1, 512), conv_w,
      conv_b.reshape(-1, 1, 1024), ml_norm.reshape(-1, 1, 512))


def _odd_mixer_kernel(p_ref, x_ref, wr_ref, w2_ref, gb_ref, gnorm_ref, mix_ref, s_st):
    L = SEQ_CHUNK
    c = pl.program_id(1)

    @pl.when(c == 0)
    def _():
        s_st[...] = jnp.zeros_like(s_st)

    xb = x_ref[...].astype(BF16)
    glr = _dot(xb, wr_ref[...].astype(BF16))
    z = _dot(glr.astype(BF16), w2_ref[...].astype(BF16)) + gb_ref[...]
    lg = _log_sigmoid(z) / GLA_TAU

    ri = lax.broadcasted_iota(I32, (L, L), 0)
    ci = lax.broadcasted_iota(I32, (L, L), 1)
    one = lambda m: jnp.where(m, 1.0, 0.0).astype(BF16)
    causal = ci <= ri
    blk0 = (ri // GLA_SUB) * GLA_SUB
    m_in = one(causal & (ci >= blk0))
    m_all = one(causal)
    lg3 = _split3(lg)
    cum = lambda m01: _dot(m01, lg3[0]) + _dot(m01, lg3[1]) + _dot(m01, lg3[2])
    b_in = cum(m_in)
    b_all = cum(m_all)

    levels = []
    for hs in (64, 32, 16):
        ps = 2 * hs
        pair0 = (ri // ps) * ps
        refrow = pair0 + hs - 1
        upper = (ri - pair0) >= hs
        dq = cum(one(upper & (ci > refrow) & causal))
        dk = cum(one((~upper) & (ci > ri) & (ci <= refrow)))
        rcol = lax.broadcasted_iota(I32, (L, HEAD_DK), 0)
        up_c = ((rcol % ps) >= hs)
        same_pair = (ri // ps) == (ci // ps)
        levels.append((dq, dk, up_c, same_pair, ps))

    rows16 = lax.broadcasted_iota(I32, (GLA_SUB, HEAD_DK), 0)
    lane16 = lax.broadcasted_iota(I32, (GLA_SUB, LANES), 1)
    eye = ri == ci
    qscale = HEAD_DK ** -0.5
    for h in range(HEADS):
        lo = h * 128
        qh = p_ref[:, lo:lo + 128] * qscale
        kh = p_ref[:, 512 + lo:512 + lo + 128]
        vb = p_ref[:, 1024 + h * GLA_DV:1024 + (h + 1) * GLA_DV].astype(BF16)
        rg = p_ref[:, 2048 + h * GLA_DV:2048 + (h + 1) * GLA_DV]
        bh = b_all[:, lo:lo + 128]
        bin_h = b_in[:, lo:lo + 128]

        o = _dot((qh * jnp.exp(bh)).astype(BF16), s_st[h].astype(BF16))

        amat = jnp.zeros((L, L), F32)
        for dq, dk, up_c, same_pair, ps in levels:
            qt = jnp.where(up_c, qh * jnp.exp(dq[:, lo:lo + 128]), 0.0)
            kt = jnp.where(up_c, 0.0, kh * jnp.exp(dk[:, lo:lo + 128]))
            a_lev = _dot_nt(qt.astype(BF16), kt.astype(BF16))
            amat = amat + (a_lev if ps == L else jnp.where(same_pair, a_lev, 0.0))

        blocks = []
        for blk in range(L // GLA_SUB):
            r0 = blk * GLA_SUB
            qb16 = qh[r0:r0 + GLA_SUB, :]
            kb16 = kh[r0:r0 + GLA_SUB, :]
            bb16 = bin_h[r0:r0 + GLA_SUB, :]
            ablk = jnp.zeros((GLA_SUB, LANES), F32)
            for s in range(GLA_SUB):
                e = jnp.where(rows16 >= s, bb16 - bb16[s:s + 1, :], NEG)
                pr = qb16 * kb16[s:s + 1, :] * jnp.exp(e)
                a = jnp.sum(pr, axis=1, keepdims=True)
                ablk = jnp.where(lane16 == (r0 + s), a, ablk)
            blocks.append(ablk)
        amat = amat + jnp.concatenate(blocks, axis=0)

        o = o + _dot(amat.astype(BF16), vb)

        bl = bh[L - 1:L, :]
        kdec = kh * jnp.exp(bl - bh)
        ecol = jnp.sum(jnp.where(eye, jnp.exp(bl), 0.0), axis=1, keepdims=True)
        s_st[h] = ecol * s_st[h] + _dot(kdec.T.astype(BF16), vb)

        mu = jnp.mean(o, axis=1, keepdims=True)
        oc = o - mu
        var = jnp.mean(oc * oc, axis=1, keepdims=True)
        on = oc * lax.rsqrt(var + LN_EPS) * gnorm_ref[:, h * GLA_DV:(h + 1) * GLA_DV]
        mix_ref[:, h * GLA_DV:(h + 1) * GLA_DV] = (on * (rg * _sigmoid(rg))).astype(BF16)


def _odd_mixer(p, x2d, B, S, w_in, li, gla_w2, gla_b, gla_norm):
    T, D = x2d.shape
    L = SEQ_CHUNK
    nc = S // L
    wr = jnp.pad(w_in[li, :, 3072:3072 + GLA_RANK], ((0, 0), (0, LANES - GLA_RANK)))
    w2 = jnp.pad(gla_w2[li], ((0, LANES - GLA_RANK), (0, 0)))
    row = lambda b, c: (b * nc + c, 0)
    full2 = lambda b, c: (0, 0)
    return pl.pallas_call(
        _odd_mixer_kernel,
        out_shape=jax.ShapeDtypeStruct((T, D), BF16),
        grid=(B, nc),
        in_specs=[pl.BlockSpec((L, 3072), row),
                  pl.BlockSpec((L, D), row),
                  pl.BlockSpec((D, LANES), full2),
                  pl.BlockSpec((LANES, 512), full2),
                  pl.BlockSpec((None, 1, 512), lambda b, c: (li, 0, 0)),
                  pl.BlockSpec((None, 1, 1024), lambda b, c: (li, 0, 0))],
        out_specs=pl.BlockSpec((L, D), row),
        scratch_shapes=[pltpu.VMEM((HEADS, HEAD_DK, GLA_DV), F32)],
        compiler_params=_params(("arbitrary", "arbitrary")),
        name="odd_mixer",
    )(p, x2d, wr, w2, gla_b.reshape(-1, 1, 512), gla_norm.reshape(-1, 1, 1024))


def _post_kernel(mix_ref, x_ref, wout_ref, g_ref, b_ref, rw_ref, rb_ref,
                 x1_ref, x1r_ref, eid_ref, rank_ref, gate_ref, cnt_ref,
                 wb_ref, carry_ref):
    tm = mix_ref.shape[0]
    i = pl.program_id(0)

    @pl.when(i == 0)
    def _():
        wb_ref[...] = wout_ref[...].astype(BF16)
        carry_ref[...] = jnp.zeros_like(carry_ref)

    y = _dot(mix_ref[...], wb_ref[...]) + ALPHA * x_ref[...]
    mu = jnp.mean(y, axis=1, keepdims=True)
    yc = y - mu
    var = jnp.mean(yc * yc, axis=1, keepdims=True)
    x1 = yc * lax.rsqrt(var + LN_EPS) * g_ref[...] + b_ref[...]
    x1_ref[...] = x1
    for c in range(SUBLANES):
        x1r_ref[pl.ds(c, tm, stride=SUBLANES), :] = x1[:, c * LANES:(c + 1) * LANES]

    hi = x1.astype(BF16)
    lo = (x1 - hi.astype(F32)).astype(BF16)
    rw = rw_ref[...]
    whi = rw.astype(BF16)
    wlo = (rw - whi.astype(F32)).astype(BF16)
    logits = _dot(hi, whi) + _dot(lo, whi) + _dot(hi, wlo) + rb_ref[...]

    lane = lax.broadcasted_iota(I32, (tm, LANES), 1)
    lane_f = lane.astype(F32)
    l = logits
    vals, ids, ohs = [], [], []
    for _ in range(TOP_K):
        m = jnp.max(l, axis=1, keepdims=True)
        idx = jnp.min(jnp.where(l == m, lane_f, float(LANES)), axis=1, keepdims=True)
        oh = lane_f == idx
        vals.append(m)
        ids.append(idx)
        ohs.append(oh)
        l = jnp.where(oh, -jnp.inf, l)
    es = [jnp.exp(v - vals[0]) for v in vals]
    tot = es[0] + es[1] + es[2] + es[3]

    onehot = jnp.zeros((tm, LANES), F32)
    for oh in ohs:
        onehot = onehot + jnp.where(oh, 1.0, 0.0)
    ri = lax.broadcasted_iota(I32, (tm, tm), 0)
    ci = lax.broadcasted_iota(I32, (tm, tm), 1)
    strict = jnp.where(ci < ri, 1.0, 0.0).astype(BF16)
    before = _dot(strict, onehot.astype(BF16)) + carry_ref[...]
    carry_ref[...] = carry_ref[...] + jnp.sum(onehot, axis=0, keepdims=True)
    cnt_ref[...] = carry_ref[...]

    eid = jnp.zeros((tm, LANES), F32)
    rank = jnp.zeros((tm, LANES), F32)
    gate = jnp.zeros((tm, LANES), F32)
    for k in range(TOP_K):
        rk = jnp.sum(jnp.where(ohs[k], before, 0.0), axis=1, keepdims=True)
        sel = lane == k
        eid = jnp.where(sel, ids[k], eid)
        rank = jnp.where(sel, rk, rank)
        gate = jnp.where(sel, es[k] / tot, gate)
    eid_ref[...] = eid.astype(I32)
    rank_ref[...] = rank.astype(I32)
    gate_ref[...] = gate


def _post(mix, x2d, w_out, li, ln_g, ln_b, router_w, router_b, layer):
    T, D = x2d.shape
    tm = min(256, T)
    rw = jnp.pad(router_w[layer], ((0, 0), (0, LANES - N_EXPERTS)))
    rb = jnp.pad(router_b[layer], (0, LANES - N_EXPERTS), constant_values=NEG).reshape(1, LANES)
    row = lambda i: (i, 0)
    full2 = lambda i: (0, 0)
    lane_out = jax.ShapeDtypeStruct((T, LANES), I32)
    return pl.pallas_call(
        _post_kernel,
        out_shape=(jax.ShapeDtypeStruct((T, D), F32),
                   jax.ShapeDtypeStruct((T * SUBLANES, LANES), F32),
                   lane_out, lane_out,
                   jax.ShapeDtypeStruct((T, LANES), F32),
                   jax.ShapeDtypeStruct((1, LANES), F32)),
        grid=(T // tm,),
        in_specs=[pl.BlockSpec((tm, D), row),
                  pl.BlockSpec((tm, D), row),
                  pl.BlockSpec((None, D, D), lambda i: (li, 0, 0)),
                  pl.BlockSpec((None, 1, D), lambda i: (layer, 0, 0)),
                  pl.BlockSpec((None, 1, D), lambda i: (layer, 0, 0)),
                  pl.BlockSpec((D, LANES), full2),
                  pl.BlockSpec((1, LANES), full2)],
        out_specs=(pl.BlockSpec((tm, D), row),
                   pl.BlockSpec((tm * SUBLANES, LANES), row),
                   pl.BlockSpec((tm, LANES), row),
                   pl.BlockSpec((tm, LANES), row),
                   pl.BlockSpec((tm, LANES), row),
                   pl.BlockSpec((1, LANES), full2)),
        scratch_shapes=[pltpu.VMEM((D, D), BF16), pltpu.VMEM((1, LANES), F32)],
        compiler_params=_params(("arbitrary",)),
        name="out_proj_ln_router",
    )(mix, x2d, w_out, ln_g.reshape(-1, 1, D), ln_b.reshape(-1, 1, D), rw, rb)


def _pos_kernel(eid_ref, rank_ref, off_ref, pos_ref):
    tm = eid_ref.shape[0]
    lane = lax.broadcasted_iota(I32, (tm, LANES), 1)
    lane_f = lane.astype(F32)
    eid = eid_ref[...].astype(F32)
    off = off_ref[...]
    pos = jnp.zeros((tm, LANES), F32)
    for k in range(TOP_K):
        ek = jnp.sum(jnp.where(lane == k, eid, 0.0), axis=1, keepdims=True)
        ok = jnp.sum(jnp.where(lane_f == ek, off, 0.0), axis=1, keepdims=True)
        pos = jnp.where(lane == k, ok, pos)
    pos_ref[...] = pos.astype(I32) + rank_ref[...]


def _positions(eid, rank, off_row):
    T = eid.shape[0]
    tm = min(1024, T)
    row = lambda i: (i, 0)
    return pl.pallas_call(
        _pos_kernel,
        out_shape=jax.ShapeDtypeStruct((T, LANES), I32),
        grid=(T // tm,),
        in_specs=[pl.BlockSpec((tm, LANES), row), pl.BlockSpec((tm, LANES), row),
                  pl.BlockSpec((1, LANES), lambda i: (0, 0))],
        out_specs=pl.BlockSpec((tm, LANES), row),
        compiler_params=_params(("arbitrary",)),
        name="route_positions",
    )(eid, rank, off_row)


def _dispatch_kernel(pos_ref, cnt_ref, off_ref, nv_ref, x_ref, xs_ref, stage, zbuf, sems, zsem, *, n_groups):
    G = x_ref.shape[0] // SUBLANES
    gi = pl.program_id(0)
    slot = gi % 2

    def row_copy(tt, k):
        src = stage.at[slot, pl.ds(pl.multiple_of(tt * SUBLANES, SUBLANES), SUBLANES), :]
        p = pos_ref[(gi * G + tt) * TOP_K + k]
        dst = xs_ref.at[pl.ds(pl.multiple_of(p * SUBLANES, SUBLANES), SUBLANES), :]
        return pltpu.make_async_copy(src, dst, sems.at[slot])

    def group_wait(s):
        n = G * TOP_K * SUBLANES
        pltpu.make_async_copy(xs_ref.at[pl.ds(0, n), :], xs_ref.at[pl.ds(0, n), :], sems.at[s]).wait()

    @pl.when(gi >= 2)
    def _():
        group_wait(slot)

    stage[slot] = x_ref[...]

    def tok(tt, carry):
        for k in range(TOP_K):
            row_copy(tt, k).start(priority=k % 2)
        return carry

    lax.fori_loop(0, G, tok, 0)

    @pl.when(gi == n_groups - 1)
    def _():
        group_wait(slot)
        if n_groups >= 2:
            group_wait(1 - slot)

        zbuf[...] = jnp.zeros_like(zbuf)

        def pad_expert(e, carry):
            cnt = cnt_ref[e]
            npad = (FFN_TILE - cnt % FFN_TILE) % FFN_TILE
            base = off_ref[e] + cnt

            def zcopy(r):
                dst = xs_ref.at[pl.ds(pl.multiple_of((base + r) * SUBLANES, SUBLANES), SUBLANES), :]
                return pltpu.make_async_copy(zbuf.at[pl.ds(0, SUBLANES), :], dst, zsem)

            def start(r, c2):
                zcopy(r).start()
                return c2

            def wait(r, c2):
                zcopy(r).wait()
                return c2

            lax.fori_loop(0, npad, start, 0)
            lax.fori_loop(0, npad, wait, 0)
            return carry

        lax.fori_loop(0, N_EXPERTS, pad_expert, 0)

        tile_rows = FFN_TILE * SUBLANES
        n_tiles = xs_ref.shape[0] // tile_rows

        def tcopy(j):
            dst = xs_ref.at[pl.ds(pl.multiple_of(j * tile_rows, tile_rows), tile_rows), :]
            return pltpu.make_async_copy(zbuf, dst, zsem)

        def tstart(j, c2):
            tcopy(j).start()
            return c2

        def twait(j, c2):
            tcopy(j).wait()
            return c2

        lax.fori_loop(nv_ref[0], n_tiles, tstart, 0)
        lax.fori_loop(nv_ref[0], n_tiles, twait, 0)


def _dispatch(pos_flat, counts, offs, n_valid, x1r, n_rows):
    T = x1r.shape[0] // SUBLANES
    G = min(DISPATCH_GROUP, T)
    return pl.pallas_call(
        functools.partial(_dispatch_kernel, n_groups=T // G),
        out_shape=jax.ShapeDtypeStruct((n_rows * SUBLANES, LANES), F32),
        grid_spec=pltpu.PrefetchScalarGridSpec(
            num_scalar_prefetch=4,
            grid=(T // G,),
            in_specs=[pl.BlockSpec((G * SUBLANES, LANES), lambda i, *_: (i, 0))],
            out_specs=pl.BlockSpec(memory_space=pl.ANY),
            scratch_shapes=[pltpu.VMEM((2, G * SUBLANES, LANES), F32),
                            pltpu.VMEM((FFN_TILE * SUBLANES, LANES), F32),
                            pltpu.SemaphoreType.DMA((2,)),
                            pltpu.SemaphoreType.DMA(())]),
        compiler_params=_params(("arbitrary",)),
        name="dispatch_rows",
    )(pos_flat, counts, offs, n_valid, x1r)


def _ffn_kernel(te_ref, nv_ref, grp_ref, nxt_ref, xs_ref, wgu_hbm, wd_hbm, bg_ref, bl_ref, bd_ref, y_ref,
                xb_ref, wg_s, wl_s, wd_s, wgu_buf, wd_buf, wsems, *, layer):
    tm = FFN_TILE
    j = pl.program_id(0)
    valid = j < nv_ref[0]
    first = jnp.logical_or(j == 0, te_ref[j] != te_ref[jnp.maximum(j - 1, 0)])
    slot = grp_ref[j] % 2

    def weight_copies(e, s):
        return (pltpu.make_async_copy(wgu_hbm.at[layer, e], wgu_buf.at[s], wsems.at[0, s]),
                pltpu.make_async_copy(wd_hbm.at[layer, e], wd_buf.at[s], wsems.at[1, s]))

    @pl.when(j == 0)
    def _():
        for cp in weight_copies(te_ref[0], 0):
            cp.start()

    @pl.when(jnp.logical_and(valid, first))
    def _():
        for cp in weight_copies(te_ref[j], slot):
            cp.wait()

        @pl.when(nxt_ref[j] >= 0)
        def _():
            for cp in weight_copies(nxt_ref[j], 1 - slot):
                cp.start()

        pi = lax.broadcasted_iota(I32, (256, 256), 0)
        pj = lax.broadcasted_iota(I32, (256, 256), 1)
        src = jnp.where(pj < 128, 2 * pj, 2 * (pj - 128) + 1)
        perm = jnp.where(pi == src, 1.0, 0.0).astype(BF16)
        for blk in range(8):
            r = _dot(wgu_buf[slot, :, blk * 256:(blk + 1) * 256].astype(BF16), perm)
            wg_s[:, blk * 128:(blk + 1) * 128] = r[:, 0:128].astype(BF16)
            wl_s[:, blk * 128:(blk + 1) * 128] = r[:, 128:256].astype(BF16)
        wd_s[...] = wd_buf[slot].astype(BF16)

    @pl.when(valid)
    def _():
        for c in range(SUBLANES):
            xb_ref[:, c * LANES:(c + 1) * LANES] = xs_ref[pl.ds(c, tm, stride=SUBLANES), :].astype(BF16)
        xb = xb_ref[...]
        hg = _dot(xb, wg_s[...]) + bg_ref[...]
        hl = _dot(xb, wl_s[...]) + bl_ref[...]
        glu = jnp.minimum(hg, SWIGLU_LIMIT)
        lin = jnp.clip(hl, -SWIGLU_LIMIT, SWIGLU_LIMIT)
        act = glu * _sigmoid(SWIGLU_ALPHA * glu) * (lin + 1.0)
        y = _dot(act.astype(BF16), wd_s[...]) + bd_ref[...]
        for c in range(SUBLANES):
            y_ref[pl.ds(c, tm, stride=SUBLANES), :] = y[:, c * LANES:(c + 1) * LANES]

    @pl.when(jnp.logical_not(valid))
    def _():
        y_ref[...] = jnp.zeros_like(y_ref)


def _ffn(tile_expert, n_valid, tile_group, next_expert, xs, w_gate_up, w_down, bg, bl, b_down, layer, n_tiles):
    tm = FFN_TILE
    D = w_down.shape[-1]
    tile = lambda j, te, nv, grp, nxt: (jnp.minimum(j, nv[0] - 1), 0)
    out_tile = lambda j, te, nv, grp, nxt: (j, 0)
    bias = lambda j, te, nv, grp, nxt: (layer, te[j], 0, 0)
    return pl.pallas_call(
        functools.partial(_ffn_kernel, layer=layer),
        out_shape=jax.ShapeDtypeStruct(xs.shape, F32),
        grid_spec=pltpu.PrefetchScalarGridSpec(
            num_scalar_prefetch=4,
            grid=(n_tiles,),
            in_specs=[pl.BlockSpec((tm * SUBLANES, LANES), tile),
                      pl.BlockSpec(memory_space=pl.ANY),
                      pl.BlockSpec(memory_space=pl.ANY),
                      pl.BlockSpec((None, None, 1, D), bias),
                      pl.BlockSpec((None, None, 1, D), bias),
                      pl.BlockSpec((None, None, 1, D), bias)],
            out_specs=pl.BlockSpec((tm * SUBLANES, LANES), out_tile),
            scratch_shapes=[pltpu.VMEM((tm, D), BF16),
                            pltpu.VMEM((D, D), BF16),
                            pltpu.VMEM((D, D), BF16),
                            pltpu.VMEM((D, D), BF16),
                            pltpu.VMEM((2, D, 2 * D), F32),
                            pltpu.VMEM((2, D, D), F32),
                            pltpu.SemaphoreType.DMA((2, 2))]),
        compiler_params=_params(("arbitrary",)),
        name="expert_ffn",
    )(tile_expert, n_valid, tile_group, next_expert, xs, w_gate_up, w_down, bg, bl, b_down)


def _combine_kernel(pos_ref, y_ref, gate_ref, x1_ref, g_ref, b_ref, o_ref, ybuf, zs, sems):
    tc = COMBINE_TILE
    i = pl.program_id(0)
    n = pl.num_programs(0)

    def issue(tile, slot):
        def tok(tt, carry):
            for k in range(TOP_K):
                p = pos_ref[(tile * tc + tt) * TOP_K + k]
                src = y_ref.at[pl.ds(pl.multiple_of(p * SUBLANES, SUBLANES), SUBLANES), :]
                dst = ybuf.at[slot, pl.ds(pl.multiple_of((k * tc + tt) * SUBLANES, SUBLANES), SUBLANES), :]
                pltpu.make_async_copy(src, dst, sems.at[slot]).start(priority=k % 2)
            return carry

        lax.fori_loop(0, tc, tok, 0)

    @pl.when(i == 0)
    def _():
        issue(0, 0)

    @pl.when(i + 1 < n)
    def _():
        issue(i + 1, (i + 1) % 2)

    slot = i % 2
    rows = TOP_K * tc * SUBLANES
    pltpu.make_async_copy(y_ref.at[pl.ds(0, rows), :], ybuf.at[slot], sems.at[slot]).wait()

    gate = gate_ref[...]
    s1 = jnp.zeros((tc, 1), F32)
    for c in range(SUBLANES):
        z = ALPHA * x1_ref[:, c * LANES:(c + 1) * LANES]
        for k in range(TOP_K):
            z = z + gate[:, k:k + 1] * ybuf[slot, pl.ds(k * tc * SUBLANES + c, tc, stride=SUBLANES), :]
        zs[:, c * LANES:(c + 1) * LANES] = z
        s1 = s1 + jnp.sum(z, axis=1, keepdims=True)
    zv = zs[...]
    mu = s1 / zv.shape[1]
    zc = zv - mu
    var = jnp.mean(zc * zc, axis=1, keepdims=True)
    o_ref[...] = zc * lax.rsqrt(var + LN_EPS) * g_ref[...] + b_ref[...]


def _combine(pos_flat, y, gate, x1, ln_g, ln_b, layer):
    T, D = x1.shape
    tc = COMBINE_TILE
    row = lambda i, pos: (i, 0)
    return pl.pallas_call(
        _combine_kernel,
        out_shape=jax.ShapeDtypeStruct((T, D), F32),
        grid_spec=pltpu.PrefetchScalarGridSpec(
            num_scalar_prefetch=1,
            grid=(T // tc,),
            in_specs=[pl.BlockSpec(memory_space=pl.ANY),
                      pl.BlockSpec((tc, LANES), row),
                      pl.BlockSpec((tc, D), row),
                      pl.BlockSpec((None, 1, D), lambda i, pos: (layer, 0, 0)),
                      pl.BlockSpec((None, 1, D), lambda i, pos: (layer, 0, 0))],
            out_specs=pl.BlockSpec((tc, D), row),
            scratch_shapes=[pltpu.VMEM((2, TOP_K * tc * SUBLANES, LANES), F32),
                            pltpu.VMEM((tc, D), F32),
                            pltpu.SemaphoreType.DMA((2,))]),
        compiler_params=_params(("arbitrary",)),
        name="combine_ln",
    )(pos_flat, y, gate, x1, ln_g.reshape(-1, 1, D), ln_b.reshape(-1, 1, D))


def _moe(x1, x1r, eid, rank, gate, cnt, layer, w_gate_up, b_gate_up, w_down, b_down, ln_g, ln_b):
    T, D = x1.shape
    tm = FFN_TILE
    n_tiles = (T * TOP_K) // tm + N_EXPERTS
    n_rows = n_tiles * tm
    counts = cnt[0, :N_EXPERTS].astype(I32)
    tiles_e = (counts + tm - 1) // tm
    tile_end = jnp.cumsum(tiles_e)
    offs = (tile_end - tiles_e) * tm
    n_valid = tile_end[-1]
    jj = jnp.minimum(jnp.arange(n_tiles, dtype=I32), n_valid - 1)
    tile_expert = jnp.minimum(jnp.sum(tile_end[None, :] <= jj[:, None], axis=1), N_EXPERTS - 1).astype(I32)
    off_row = jnp.pad(offs, (0, LANES - N_EXPERTS)).astype(F32).reshape(1, LANES)
    present = tiles_e > 0
    group_of_e = jnp.cumsum(present.astype(I32)) - 1
    eidx = jnp.arange(N_EXPERTS, dtype=I32)
    later = jnp.where(present[None, :] & (eidx[None, :] > eidx[:, None]), eidx[None, :], N_EXPERTS)
    next_e = jnp.min(later, axis=1)
    next_e = jnp.where(next_e == N_EXPERTS, -1, next_e).astype(I32)
    tile_group = group_of_e[tile_expert].astype(I32)
    next_expert = next_e[tile_expert]

    pos = _positions(eid, rank, off_row)
    pos_flat = pos[:, :TOP_K].reshape(-1)
    nv = n_valid.reshape(1).astype(I32)
    xs = _dispatch(pos_flat, counts, offs.astype(I32), nv, x1r, n_rows)
    bgu = b_gate_up.reshape(DEPTH, N_EXPERTS, D, 2)
    bg = bgu[..., 0].reshape(DEPTH, N_EXPERTS, 1, D)
    bl = bgu[..., 1].reshape(DEPTH, N_EXPERTS, 1, D)
    y = _ffn(tile_expert, nv, tile_group, next_expert, xs, w_gate_up, w_down, bg, bl,
             b_down.reshape(DEPTH, N_EXPERTS, 1, D), layer, n_tiles)
    return _combine(pos_flat, y, gate, x1, ln_g, ln_b, layer)


def kernel(x, even_w_in, pool_w, pool_scale, conv_w, conv_b, i_bias, f_bias, ml_norm, even_w_out, odd_w_in, gla_w2, gla_b, gla_norm, odd_w_out, ln1_g, ln1_b, ln2_g, ln2_b, router_w, router_b, w_gate_up, b_gate_up, w_down, b_down):
    B, S, D = x.shape
    xt = x.reshape(B * S, D)
    for layer in range(DEPTH):
        li = layer // 2
        if layer % 2 == 0:
            p = _project(xt, even_w_in, li, 2560)
            mix = _even_mixer(p, xt, B, S, even_w_in, li, pool_w, pool_scale, conv_w, conv_b,
                              i_bias, f_bias, ml_norm)
            w_out = even_w_out
        else:
            p = _project(xt, odd_w_in, li, 3072)
            mix = _odd_mixer(p, xt, B, S, odd_w_in, li, gla_w2, gla_b, gla_norm)
            w_out = odd_w_out
        x1, x1r, eid, rank, gate, cnt = _post(mix, xt, w_out, li, ln1_g, ln1_b, router_w, router_b, layer)
        xt = _moe(x1, x1r, eid, rank, gate, cnt, layer, w_gate_up, b_gate_up, w_down, b_down,
                  ln2_g, ln2_b)
    return xt.reshape(B, S, D)
```

```python
import functools

import jax
import jax.numpy as jnp
from jax import lax
from jax.experimental import pallas as pl
from jax.experimental.pallas import tpu as pltpu

F32 = jnp.float32
BF16 = jnp.bfloat16
I32 = jnp.int32

LANES = 128
SUBLANES = 8
VMEM_LIMIT = 56 * 1024 * 1024

DEPTH = 4
ALPHA = (2 * DEPTH) ** 0.25
LN_EPS = 1e-5
NEG = -1e30

POOL_WINDOWS = (2, 4, 8, 16)
POOL_CARRY = 16
CONV_K = 4
CONV_CARRY = 8
HEADS = 4
HEAD_DK = 128
GLA_DV = 256
GLA_RANK = 16
GLA_TAU = 16.0
GLA_SUB = 16
N_EXPERTS = 32
TOP_K = 4
SWIGLU_LIMIT = 7.0
SWIGLU_ALPHA = 1.702

SEQ_CHUNK = 128
POST_TILE = 512
POST_SUB = 256
FFN_TILE = 256
COMBINE_TILE = 128
DST_BITS = 17
INVERT_STEP = 512


def _dot(a, b):
    return jnp.dot(a, b, preferred_element_type=F32)


def _dot_nt(a, b):
    return lax.dot_general(a, b, (((1,), (1,)), ((), ())), preferred_element_type=F32)


def _split3(a):
    a1 = a.astype(BF16)
    r1 = a - a1.astype(F32)
    a2 = r1.astype(BF16)
    r2 = r1 - a2.astype(F32)
    return a1, a2, r2.astype(BF16)


def _dot01_left(m01, a):
    a1, a2, a3 = _split3(a)
    return _dot(m01, a1) + _dot(m01, a2) + _dot(m01, a3)


def _dot01_right(a, m01):
    a1, a2, a3 = _split3(a)
    return _dot(a1, m01) + _dot(a2, m01) + _dot(a3, m01)


def _log_sigmoid(x):
    return jnp.minimum(x, 0.0) - jnp.log1p(jnp.exp(-jnp.abs(x)))


def _sigmoid(x):
    return 1.0 / (1.0 + jnp.exp(-x))


def _params(sem):
    return pltpu.CompilerParams(dimension_semantics=sem, vmem_limit_bytes=VMEM_LIMIT)


def _proj_kernel(x_ref, w_hbm, o_ref, wb_ref, stage, sems, *, li, n_cols, tn):
    @pl.when(pl.program_id(0) == 0)
    def _():
        n_chunks = n_cols // tn

        def chunk_copy(n):
            return pltpu.make_async_copy(w_hbm.at[li, :, pl.ds(n * tn, tn)], stage.at[n % 2], sems.at[n % 2])

        chunk_copy(0).start()
        for n in range(n_chunks):
            if n + 1 < n_chunks:
                chunk_copy(n + 1).start()
            chunk_copy(n).wait()
            wb_ref[:, n * tn:(n + 1) * tn] = stage[n % 2].astype(BF16)

    o_ref[...] = _dot(x_ref[...].astype(BF16), wb_ref[...])


def _project(x2d, w3d, li, n_cols, tn=512):
    T, D = x2d.shape
    tm = min(512, T)
    return pl.pallas_call(
        functools.partial(_proj_kernel, li=li, n_cols=n_cols, tn=tn),
        out_shape=jax.ShapeDtypeStruct((T, n_cols), F32),
        grid=(T // tm,),
        in_specs=[pl.BlockSpec((tm, D), lambda i: (i, 0)),
                  pl.BlockSpec(memory_space=pl.ANY)],
        out_specs=pl.BlockSpec((tm, n_cols), lambda i: (i, 0)),
        scratch_shapes=[pltpu.VMEM((D, n_cols), BF16),
                        pltpu.VMEM((2, D, tn), F32),
                        pltpu.SemaphoreType.DMA((2,))],
        compiler_params=_params(("arbitrary",)),
        name="in_proj",
    )(x2d, w3d)


def _even_mixer_kernel(p_ref, x_ref, wgc_ref, wgr_ref, gbr_ref, gbc_ref, poolw_ref, pscale_ref,
                       convw_ref, convb_ref, mlnorm_ref, mix_ref,
                       ubuf, qkbuf, c_st, n_st, m_st, tri):
    L = SEQ_CHUNK
    c = pl.program_id(1)

    @pl.when(c == 0)
    def _():
        ri0 = lax.broadcasted_iota(I32, (L, L), 0)
        ci0 = lax.broadcasted_iota(I32, (L, L), 1)
        tri[0] = jnp.where(ci0 <= ri0, 1.0, 0.0).astype(BF16)
        tri[1] = jnp.where(ri0 <= ci0, 1.0, 0.0).astype(BF16)
        ubuf[0:POOL_CARRY, :] = jnp.zeros((POOL_CARRY, 512), F32)
        qkbuf[0:CONV_CARRY, :] = jnp.zeros((CONV_CARRY, 1024), F32)
        c_st[...] = jnp.zeros_like(c_st)
        n_st[...] = jnp.zeros_like(n_st)
        m_st[...] = jnp.zeros_like(m_st)

    ubuf[POOL_CARRY:POOL_CARRY + L, :] = p_ref[:, 0:512]
    pos1 = lax.broadcasted_iota(I32, (L, 1), 0) + c * L + 1
    for g, w in enumerate(POOL_WINDOWS):
        lo = g * 128
        acc = ubuf[pl.ds(POOL_CARRY, L), lo:lo + 128]
        cur = acc
        for j in range(1, w):
            acc = acc + ubuf[pl.ds(POOL_CARRY - j, L), lo:lo + 128]
        cnt = jnp.minimum(pos1, w).astype(F32)
        d = acc / cnt - cur
        yp = _dot(d.astype(BF16), poolw_ref[g].astype(BF16)) * pscale_ref[:, lo:lo + 128]
        mix_ref[:, lo:lo + 128] = yp.astype(BF16)
    ubuf[0:POOL_CARRY, :] = ubuf[L:L + POOL_CARRY, :]

    qkbuf[CONV_CARRY:CONV_CARRY + L, :] = p_ref[:, 512:1536]
    acc = jnp.zeros((L, 1024), F32) + convb_ref[...]
    for j in range(CONV_K):
        acc = acc + convw_ref[j:j + 1, :] * qkbuf[pl.ds(CONV_CARRY - (CONV_K - 1) + j, L), :]
    qk = acc * _sigmoid(acc)
    qkbuf[0:CONV_CARRY, :] = qkbuf[L:L + CONV_CARRY, :]

    xb = x_ref[...].astype(BF16)
    gcol = _dot(xb, wgc_ref[...].astype(BF16)) + gbr_ref[...]
    grow = _dot_nt(wgr_ref[...].astype(BF16), xb) + gbc_ref[:, 0:1]
    ri = lax.broadcasted_iota(I32, (L, L), 0)
    ci = lax.broadcasted_iota(I32, (L, L), 1)
    causal = ci <= ri
    b_col = _dot01_left(tri[0], _log_sigmoid(gcol))
    b_row = _dot01_right(_log_sigmoid(grow), tri[1])

    kscale = HEAD_DK ** -0.5
    for h in range(HEADS):
        lo = h * 128
        qh = qk[:, lo:lo + 128]
        kh = qk[:, 512 + lo:512 + lo + 128] * kscale
        vh = p_ref[:, 1536 + lo:1536 + lo + 128]
        og = p_ref[:, 2048 + lo:2048 + lo + 128]
        b_c = b_col[:, 4 + h:5 + h]
        b_r = b_row[4 + h:5 + h, :]
        ig_c = gcol[:, h:h + 1]
        ig_r = grow[h:h + 1, :]
        g = b_r[:, L - 1:L]
        m_prev = m_st[h][:, 0:1]
        qb = qh.astype(BF16)
        kb = kh.astype(BF16)
        vb = vh.astype(BF16)

        dmat = jnp.where(causal, b_c - b_r + ig_r, NEG)
        m_inter = b_c + m_prev
        m_t = jnp.maximum(m_inter, jnp.max(dmat, axis=1, keepdims=True))
        sts = _dot_nt(qb, kb) * jnp.exp(dmat - m_t)
        sc = jnp.exp(m_inter - m_t)
        num = _dot(sts.astype(BF16), vb) + sc * _dot(qb, c_st[h].astype(BF16))
        qn = jnp.sum(qh * n_st[h], axis=1, keepdims=True)
        den = jnp.sum(sts, axis=1, keepdims=True) + sc * qn
        hh = num / jnp.maximum(jnp.abs(den), jnp.exp(-m_t))

        a_c = g - b_c + ig_c
        a_r = g - b_r + ig_r
        m_new = jnp.maximum(g + m_prev, jnp.max(a_r, axis=1, keepdims=True))
        decay = jnp.exp(g + m_prev - m_new)
        wk = jnp.exp(a_c - m_new) * kh
        c_st[h] = decay * c_st[h] + _dot(wk.T.astype(BF16), vb)
        n_st[h] = decay * n_st[h] + jnp.sum(wk, axis=0, keepdims=True)
        m_st[h] = jnp.broadcast_to(m_new, (1, LANES))

        mu = jnp.mean(hh, axis=1, keepdims=True)
        hc = hh - mu
        var = jnp.mean(hc * hc, axis=1, keepdims=True)
        hn = hc * lax.rsqrt(var + LN_EPS) * mlnorm_ref[:, lo:lo + 128]
        mix_ref[:, 512 + lo:512 + lo + 128] = (hn * _sigmoid(og)).astype(BF16)


def _even_mixer(p, x2d, B, S, w_in, li, pool_w, pool_scale, conv_w, conv_b, i_bias, f_bias, ml_norm):
    T, D = x2d.shape
    L = SEQ_CHUNK
    nc = S // L
    wg = w_in[li, :, 2560:2568]
    wgc = jnp.pad(wg, ((0, 0), (0, LANES - 8)))
    wgr = wg.T
    gb = jnp.concatenate([i_bias[li], f_bias[li]])
    gbr = jnp.pad(gb, (0, LANES - 8)).reshape(1, LANES)
    gbc = jnp.broadcast_to(gb.reshape(8, 1), (8, LANES))
    row = lambda b, c: (b * nc + c, 0)
    full2 = lambda b, c: (0, 0)
    return pl.pallas_call(
        _even_mixer_kernel,
        out_shape=jax.ShapeDtypeStruct((T, D), BF16),
        grid=(B, nc),
        in_specs=[pl.BlockSpec((L, 2560), row),
                  pl.BlockSpec((L, D), row),
                  pl.BlockSpec((D, LANES), full2),
                  pl.BlockSpec((8, D), full2),
                  pl.BlockSpec((1, LANES), full2),
                  pl.BlockSpec((8, LANES), full2),
                  pl.BlockSpec((None, 4, 128, 128), lambda b, c: (li, 0, 0, 0)),
                  pl.BlockSpec((None, 1, 512), lambda b, c: (li, 0, 0)),
                  pl.BlockSpec((None, CONV_K, 1024), lambda b, c: (li, 0, 0)),
                  pl.BlockSpec((None, 1, 1024), lambda b, c: (li, 0, 0)),
                  pl.BlockSpec((None, 1, 512), lambda b, c: (li, 0, 0))],
        out_specs=pl.BlockSpec((L, D), row),
        scratch_shapes=[pltpu.VMEM((L + POOL_CARRY, 512), F32),
                        pltpu.VMEM((L + CONV_CARRY, 1024), F32),
                        pltpu.VMEM((HEADS, 128, 128), F32),
                        pltpu.VMEM((HEADS, 1, 128), F32),
                        pltpu.VMEM((HEADS, 1, LANES), F32),
                        pltpu.VMEM((2, L, L), BF16)],
        compiler_params=_params(("arbitrary", "arbitrary")),
        name="even_mixer",
    )(p, x2d, wgc, wgr, gbr, gbc, pool_w, pool_scale.reshape(-1, 1, 512), conv_w,
      conv_b.reshape(-1, 1, 1024), ml_norm.reshape(-1, 1, 512))


def _odd_mixer_kernel(p_ref, x_ref, wr_ref, w2_ref, gb_ref, gnorm_ref, mix_ref, s_st):
    L = SEQ_CHUNK
    c = pl.program_id(1)

    @pl.when(c == 0)
    def _():
        s_st[...] = jnp.zeros_like(s_st)

    xb = x_ref[...].astype(BF16)
    glr = _dot(xb, wr_ref[...].astype(BF16))
    z = _dot(glr.astype(BF16), w2_ref[...].astype(BF16)) + gb_ref[...]
    lg = _log_sigmoid(z) / GLA_TAU

    ri = lax.broadcasted_iota(I32, (L, L), 0)
    ci = lax.broadcasted_iota(I32, (L, L), 1)
    one = lambda m: jnp.where(m, 1.0, 0.0).astype(BF16)
    causal = ci <= ri
    blk0 = (ri // GLA_SUB) * GLA_SUB
    m_in = one(causal & (ci >= blk0))
    m_all = one(causal)
    lg3 = _split3(lg)
    cum = lambda m01: _dot(m01, lg3[0]) + _dot(m01, lg3[1]) + _dot(m01, lg3[2])
    b_in = cum(m_in)
    b_all = cum(m_all)

    levels = []
    for hs in (64, 32, 16):
        ps = 2 * hs
        pair0 = (ri // ps) * ps
        refrow = pair0 + hs - 1
        upper = (ri - pair0) >= hs
        dq = cum(one(upper & (ci > refrow) & causal))
        dk = cum(one((~upper) & (ci > ri) & (ci <= refrow)))
        rcol = lax.broadcasted_iota(I32, (L, HEAD_DK), 0)
        up_c = ((rcol % ps) >= hs)
        same_pair = (ri // ps) == (ci // ps)
        levels.append((dq, dk, up_c, same_pair, ps))

    rows16 = lax.broadcasted_iota(I32, (GLA_SUB, HEAD_DK), 0)
    lane16 = lax.broadcasted_iota(I32, (GLA_SUB, LANES), 1)
    eye = ri == ci
    qscale = HEAD_DK ** -0.5
    for h in range(HEADS):
        lo = h * 128
        qh = p_ref[:, lo:lo + 128] * qscale
        kh = p_ref[:, 512 + lo:512 + lo + 128]
        vb = p_ref[:, 1024 + h * GLA_DV:1024 + (h + 1) * GLA_DV].astype(BF16)
        rg = p_ref[:, 2048 + h * GLA_DV:2048 + (h + 1) * GLA_DV]
        bh = b_all[:, lo:lo + 128]
        bin_h = b_in[:, lo:lo + 128]

        o = _dot((qh * jnp.exp(bh)).astype(BF16), s_st[h].astype(BF16))

        amat = jnp.zeros((L, L), F32)
        for dq, dk, up_c, same_pair, ps in levels:
            qt = jnp.where(up_c, qh * jnp.exp(dq[:, lo:lo + 128]), 0.0)
            kt = jnp.where(up_c, 0.0, kh * jnp.exp(dk[:, lo:lo + 128]))
            a_lev = _dot_nt(qt.astype(BF16), kt.astype(BF16))
            amat = amat + (a_lev if ps == L else jnp.where(same_pair, a_lev, 0.0))

        blocks = []
        for blk in range(L // GLA_SUB):
            r0 = blk * GLA_SUB
            qb16 = qh[r0:r0 + GLA_SUB, :]
            kb16 = kh[r0:r0 + GLA_SUB, :]
            bb16 = bin_h[r0:r0 + GLA_SUB, :]
            ablk = jnp.zeros((GLA_SUB, LANES), F32)
            for s in range(GLA_SUB):
                e = jnp.where(rows16 >= s, bb16 - bb16[s:s + 1, :], NEG)
                pr = qb16 * kb16[s:s + 1, :] * jnp.exp(e)
                a = jnp.sum(pr, axis=1, keepdims=True)
                ablk = jnp.where(lane16 == (r0 + s), a, ablk)
            blocks.append(ablk)
        amat = amat + jnp.concatenate(blocks, axis=0)

        o = o + _dot(amat.astype(BF16), vb)

        bl = bh[L - 1:L, :]
        kdec = kh * jnp.exp(bl - bh)
        ecol = jnp.sum(jnp.where(eye, jnp.exp(bl), 0.0), axis=1, keepdims=True)
        s_st[h] = ecol * s_st[h] + _dot(kdec.T.astype(BF16), vb)

        mu = jnp.mean(o, axis=1, keepdims=True)
        oc = o - mu
        var = jnp.mean(oc * oc, axis=1, keepdims=True)
        on = oc * lax.rsqrt(var + LN_EPS) * gnorm_ref[:, h * GLA_DV:(h + 1) * GLA_DV]
        mix_ref[:, h * GLA_DV:(h + 1) * GLA_DV] = (on * (rg * _sigmoid(rg))).astype(BF16)


def _odd_mixer(p, x2d, B, S, w_in, li, gla_w2, gla_b, gla_norm):
    T, D = x2d.shape
    L = SEQ_CHUNK
    nc = S // L
    wr = jnp.pad(w_in[li, :, 3072:3072 + GLA_RANK], ((0, 0), (0, LANES - GLA_RANK)))
    w2 = jnp.pad(gla_w2[li], ((0, LANES - GLA_RANK), (0, 0)))
    row = lambda b, c: (b * nc + c, 0)
    full2 = lambda b, c: (0, 0)
    return pl.pallas_call(
        _odd_mixer_kernel,
        out_shape=jax.ShapeDtypeStruct((T, D), BF16),
        grid=(B, nc),
        in_specs=[pl.BlockSpec((L, 3072), row),
                  pl.BlockSpec((L, D), row),
                  pl.BlockSpec((D, LANES), full2),
                  pl.BlockSpec((LANES, 512), full2),
                  pl.BlockSpec((None, 1, 512), lambda b, c: (li, 0, 0)),
                  pl.BlockSpec((None, 1, 1024), lambda b, c: (li, 0, 0))],
        out_specs=pl.BlockSpec((L, D), row),
        scratch_shapes=[pltpu.VMEM((HEADS, HEAD_DK, GLA_DV), F32)],
        compiler_params=_params(("arbitrary", "arbitrary")),
        name="odd_mixer",
    )(p, x2d, wr, w2, gla_b.reshape(-1, 1, 512), gla_norm.reshape(-1, 1, 1024))


def _post_kernel(mix_ref, x_ref, wout_ref, g_ref, b_ref, rw_ref, rb_ref,
                 x1_ref, x1r_ref, eid_ref, rank_ref, gate_ref, cnt_ref,
                 wb_ref, carry_ref):
    tm = mix_ref.shape[0]
    sub = min(POST_SUB, tm)
    i = pl.program_id(0)

    @pl.when(i == 0)
    def _():
        wb_ref[...] = wout_ref[...].astype(BF16)
        carry_ref[...] = jnp.zeros_like(carry_ref)

    rw = rw_ref[...]
    whi = rw.astype(BF16)
    wlo = (rw - whi.astype(F32)).astype(BF16)
    lane = lax.broadcasted_iota(I32, (sub, LANES), 1)
    lane_f = lane.astype(F32)
    ri = lax.broadcasted_iota(I32, (sub, sub), 0)
    ci = lax.broadcasted_iota(I32, (sub, sub), 1)
    strict = jnp.where(ci < ri, 1.0, 0.0).astype(BF16)
    carry = carry_ref[...]

    for r0 in range(0, tm, sub):
        rows = pl.ds(r0, sub)
        y = _dot(mix_ref[rows, :], wb_ref[...]) + ALPHA * x_ref[rows, :]
        mu = jnp.mean(y, axis=1, keepdims=True)
        yc = y - mu
        var = jnp.mean(yc * yc, axis=1, keepdims=True)
        x1 = yc * lax.rsqrt(var + LN_EPS) * g_ref[...] + b_ref[...]
        x1_ref[rows, :] = x1
        for c in range(SUBLANES):
            x1r_ref[pl.ds(r0 * SUBLANES + c, sub, stride=SUBLANES), :] = x1[:, c * LANES:(c + 1) * LANES]

        hi = x1.astype(BF16)
        lo = (x1 - hi.astype(F32)).astype(BF16)
        logits = _dot(hi, whi) + _dot(lo, whi) + _dot(hi, wlo) + rb_ref[...]

        l = logits
        vals, ids, ohs = [], [], []
        for _ in range(TOP_K):
            m = jnp.max(l, axis=1, keepdims=True)
            idx = jnp.min(jnp.where(l == m, lane_f, float(LANES)), axis=1, keepdims=True)
            oh = lane_f == idx
            vals.append(m)
            ids.append(idx)
            ohs.append(oh)
            l = jnp.where(oh, -jnp.inf, l)
        es = [jnp.exp(v - vals[0]) for v in vals]
        tot = es[0] + es[1] + es[2] + es[3]

        onehot = jnp.zeros((sub, LANES), F32)
        for oh in ohs:
            onehot = onehot + jnp.where(oh, 1.0, 0.0)
        before = _dot(strict, onehot.astype(BF16)) + carry
        carry = carry + jnp.sum(onehot, axis=0, keepdims=True)

        eid = jnp.zeros((sub, LANES), F32)
        rank = jnp.zeros((sub, LANES), F32)
        gate = jnp.zeros((sub, LANES), F32)
        for k in range(TOP_K):
            rk = jnp.sum(jnp.where(ohs[k], before, 0.0), axis=1, keepdims=True)
            sel = lane == k
            eid = jnp.where(sel, ids[k], eid)
            rank = jnp.where(sel, rk, rank)
            gate = jnp.where(sel, es[k] / tot, gate)
        eid_ref[rows, :] = eid.astype(I32)
        rank_ref[rows, :] = rank.astype(I32)
        gate_ref[rows, :] = gate

    carry_ref[...] = carry
    cnt_ref[...] = carry


def _post(mix, x2d, w_out, li, ln_g, ln_b, router_w, router_b, layer):
    T, D = x2d.shape
    tm = min(POST_TILE, T)
    rw = jnp.pad(router_w[layer], ((0, 0), (0, LANES - N_EXPERTS)))
    rb = jnp.pad(router_b[layer], (0, LANES - N_EXPERTS), constant_values=NEG).reshape(1, LANES)
    row = lambda i: (i, 0)
    full2 = lambda i: (0, 0)
    lane_out = jax.ShapeDtypeStruct((T, LANES), I32)
    return pl.pallas_call(
        _post_kernel,
        out_shape=(jax.ShapeDtypeStruct((T, D), F32),
                   jax.ShapeDtypeStruct((T * SUBLANES, LANES), F32),
                   lane_out, lane_out,
                   jax.ShapeDtypeStruct((T, LANES), F32),
                   jax.ShapeDtypeStruct((1, LANES), F32)),
        grid=(T // tm,),
        in_specs=[pl.BlockSpec((tm, D), row),
                  pl.BlockSpec((tm, D), row),
                  pl.BlockSpec((None, D, D), lambda i: (li, 0, 0)),
                  pl.BlockSpec((None, 1, D), lambda i: (layer, 0, 0)),
                  pl.BlockSpec((None, 1, D), lambda i: (layer, 0, 0)),
                  pl.BlockSpec((D, LANES), full2),
                  pl.BlockSpec((1, LANES), full2)],
        out_specs=(pl.BlockSpec((tm, D), row),
                   pl.BlockSpec((tm * SUBLANES, LANES), row),
                   pl.BlockSpec((tm, LANES), row),
                   pl.BlockSpec((tm, LANES), row),
                   pl.BlockSpec((tm, LANES), row),
                   pl.BlockSpec((1, LANES), full2)),
        scratch_shapes=[pltpu.VMEM((D, D), BF16), pltpu.VMEM((1, LANES), F32)],
        compiler_params=_params(("arbitrary",)),
        name="out_proj_ln_router",
    )(mix, x2d, w_out, ln_g.reshape(-1, 1, D), ln_b.reshape(-1, 1, D), rw, rb)


def _pos_kernel(eid_ref, rank_ref, off_ref, pos_ref):
    tm = eid_ref.shape[0]
    lane = lax.broadcasted_iota(I32, (tm, LANES), 1)
    lane_f = lane.astype(F32)
    eid = eid_ref[...].astype(F32)
    off = off_ref[...]
    pos = jnp.zeros((tm, LANES), F32)
    for k in range(TOP_K):
        ek = jnp.sum(jnp.where(lane == k, eid, 0.0), axis=1, keepdims=True)
        ok = jnp.sum(jnp.where(lane_f == ek, off, 0.0), axis=1, keepdims=True)
        pos = jnp.where(lane == k, ok, pos)
    pos_ref[...] = pos.astype(I32) + rank_ref[...]


def _positions(eid, rank, off_row):
    T = eid.shape[0]
    tm = min(1024, T)
    row = lambda i: (i, 0)
    return pl.pallas_call(
        _pos_kernel,
        out_shape=jax.ShapeDtypeStruct((T, LANES), I32),
        grid=(T // tm,),
        in_specs=[pl.BlockSpec((tm, LANES), row), pl.BlockSpec((tm, LANES), row),
                  pl.BlockSpec((1, LANES), lambda i: (0, 0))],
        out_specs=pl.BlockSpec((tm, LANES), row),
        compiler_params=_params(("arbitrary",)),
        name="route_positions",
    )(eid, rank, off_row)


def _invert_kernel(pos_ref, cnt_ref, off_ref, nv_ref, dst_ref, *, n_assign):
    n_rows = dst_ref.shape[0]
    n_tok = n_assign // TOP_K
    tc_bits = COMBINE_TILE.bit_length() - 1
    step = pl.program_id(0)
    per_step = min(INVERT_STEP, n_tok)

    def place(i, carry):
        t = step * per_step + i
        hi = lax.shift_left(t, DST_BITS) | lax.shift_left(lax.shift_right_logical(t, tc_bits) * TOP_K, tc_bits) \
            | (t & (COMBINE_TILE - 1))
        for k in range(TOP_K):
            dst_ref[pos_ref[t * TOP_K + k]] = hi + (k << tc_bits)
        return carry

    lax.fori_loop(0, per_step, place, 0, unroll=4)

    @pl.when(step == pl.num_programs(0) - 1)
    def _():
        spare_tok = (n_tok - 1) << DST_BITS

        def fill(lo, hi, nxt):
            def body(p, carry):
                dst_ref[p] = spare_tok + nxt + (p - lo)
                return carry

            lax.fori_loop(lo, hi, body, 0)
            return nxt + (hi - lo)

        def pad_expert(e, nxt):
            cnt = cnt_ref[e]
            lo = off_ref[e] + cnt
            hi = off_ref[e] + ((cnt + FFN_TILE - 1) // FFN_TILE) * FFN_TILE
            return fill(lo, hi, nxt)

        nxt = lax.fori_loop(0, N_EXPERTS, pad_expert, jnp.int32(n_assign))
        fill(nv_ref[0] * FFN_TILE, n_rows, nxt)


def _invert(pos_flat, counts, offs, n_valid, n_rows):
    n_assign = pos_flat.shape[0]
    n_tok = n_assign // TOP_K
    smem = pl.BlockSpec(memory_space=pltpu.SMEM)
    return pl.pallas_call(
        functools.partial(_invert_kernel, n_assign=n_assign),
        out_shape=jax.ShapeDtypeStruct((n_rows,), I32),
        grid=(max(n_tok // INVERT_STEP, 1),),
        in_specs=[smem, smem, smem, smem],
        out_specs=smem,
        compiler_params=_params(("arbitrary",)),
        name="invert_routing",
    )(pos_flat, counts, offs, n_valid)


def _ffn_kernel(te_ref, nv_ref, grp_ref, nxt_ref, dst_ref, x_hbm, wgu_hbm, wd_hbm, bg_ref, bl_ref, bd_ref,
                y_hbm, xbuf, ystage, xb_ref, wg_s, wl_s, wd_s, wgu_buf, wd_buf, wsems, gsems, ssems,
                *, layer, n_tiles):
    tm = FFN_TILE
    tile_rows = tm * SUBLANES
    j = pl.program_id(0)
    slot = j % 2
    valid = j < nv_ref[0]
    first = jnp.logical_or(j == 0, te_ref[j] != te_ref[jnp.maximum(j - 1, 0)])
    wslot = grp_ref[j] % 2

    def gather_copy(tile, s, r):
        t = lax.shift_right_logical(dst_ref[tile * tm + r], DST_BITS)
        src = x_hbm.at[pl.ds(pl.multiple_of(t * SUBLANES, SUBLANES), SUBLANES), :]
        dst = xbuf.at[s, pl.ds(pl.multiple_of(r * SUBLANES, SUBLANES), SUBLANES), :]
        return pltpu.make_async_copy(src, dst, gsems.at[s])

    def scatter_copy(tile, s, r):
        d = dst_ref[tile * tm + r] & ((1 << DST_BITS) - 1)
        src = ystage.at[s, pl.ds(pl.multiple_of(r * SUBLANES, SUBLANES), SUBLANES), :]
        dst = y_hbm.at[pl.ds(pl.multiple_of(d * SUBLANES, SUBLANES), SUBLANES), :]
        return pltpu.make_async_copy(src, dst, ssems.at[s])

    def gather_wait(s):
        pltpu.make_async_copy(x_hbm.at[pl.ds(0, tile_rows), :], xbuf.at[s], gsems.at[s]).wait()

    def scatter_wait(s):
        pltpu.make_async_copy(ystage.at[s], y_hbm.at[pl.ds(0, tile_rows), :], ssems.at[s]).wait()

    def issue_loop(make, tile, s):
        def body(r, carry):
            make(tile, s, r).start()
            return carry

        lax.fori_loop(0, tm, body, 0)

    next_tile = jnp.minimum(j + 1, n_tiles - 1)

    def issue_span(lo, hi, with_scatter):
        for r in range(lo, hi):
            gather_copy(next_tile, 1 - slot, r).start(priority=r % 2)
            if with_scatter:
                scatter_copy(j - 1, 1 - slot, r).start(priority=r % 2)

    def weight_copies(e, s):
        return (pltpu.make_async_copy(wgu_hbm.at[layer, e], wgu_buf.at[s], wsems.at[0, s]),
                pltpu.make_async_copy(wd_hbm.at[layer, e], wd_buf.at[s], wsems.at[1, s]))

    @pl.when(j == 0)
    def _():
        for cp in weight_copies(te_ref[0], 0):
            cp.start()
        issue_loop(gather_copy, 0, 0)

    gather_wait(slot)

    @pl.when(j >= 2)
    def _():
        scatter_wait(slot)

    @pl.when(jnp.logical_and(valid, first))
    def _():
        for cp in weight_copies(te_ref[j], wslot):
            cp.wait()

        @pl.when(nxt_ref[j] >= 0)
        def _():
            for cp in weight_copies(nxt_ref[j], 1 - wslot):
                cp.start()

        pi = lax.broadcasted_iota(I32, (256, 256), 0)
        pj = lax.broadcasted_iota(I32, (256, 256), 1)
        src = jnp.where(pj < 128, 2 * pj, 2 * (pj - 128) + 1)
        perm = jnp.where(pi == src, 1.0, 0.0).astype(BF16)
        for blk in range(8):
            r = _dot(wgu_buf[wslot, :, blk * 256:(blk + 1) * 256].astype(BF16), perm)
            wg_s[:, blk * 128:(blk + 1) * 128] = r[:, 0:128].astype(BF16)
            wl_s[:, blk * 128:(blk + 1) * 128] = r[:, 128:256].astype(BF16)
        wd_s[...] = wd_buf[wslot].astype(BF16)

    def compute(with_scatter):
        q = tm // 4
        for c in range(SUBLANES):
            xb_ref[:, c * LANES:(c + 1) * LANES] = xbuf[slot, pl.ds(c, tm, stride=SUBLANES), :].astype(BF16)
        xb = xb_ref[...]
        issue_span(0, q, with_scatter)
        hg = _dot(xb, wg_s[...]) + bg_ref[...]
        issue_span(q, 2 * q, with_scatter)
        hl = _dot(xb, wl_s[...]) + bl_ref[...]
        issue_span(2 * q, 3 * q, with_scatter)
        glu = jnp.minimum(hg, SWIGLU_LIMIT)
        lin = jnp.clip(hl, -SWIGLU_LIMIT, SWIGLU_LIMIT)
        act = glu * _sigmoid(SWIGLU_ALPHA * glu) * (lin + 1.0)
        y = _dot(act.astype(BF16), wd_s[...]) + bd_ref[...]
        issue_span(3 * q, tm, with_scatter)
        for c in range(SUBLANES):
            ystage[slot, pl.ds(c, tm, stride=SUBLANES), :] = y[:, c * LANES:(c + 1) * LANES]

    @pl.when(j == 0)
    def _():
        compute(False)

    @pl.when(jnp.logical_and(j > 0, valid))
    def _():
        compute(True)

    @pl.when(jnp.logical_and(j > 0, jnp.logical_not(valid)))
    def _():
        ystage[slot] = jnp.zeros((tile_rows, LANES), F32)
        issue_span(0, tm, True)

    @pl.when(j == n_tiles - 1)
    def _():
        gather_wait(1 - slot)
        if n_tiles >= 2:
            scatter_wait(1 - slot)
        issue_loop(scatter_copy, j, slot)
        scatter_wait(slot)


def _ffn(tile_expert, n_valid, tile_group, next_expert, dst, x1r, w_gate_up, w_down, bg, bl, b_down, layer,
         n_tiles, n_out_rows):
    tm = FFN_TILE
    D = w_down.shape[-1]
    bias = lambda j, te, nv, grp, nxt, dst: (layer, te[j], 0, 0)
    return pl.pallas_call(
        functools.partial(_ffn_kernel, layer=layer, n_tiles=n_tiles),
        out_shape=jax.ShapeDtypeStruct((n_out_rows * SUBLANES, LANES), F32),
        grid_spec=pltpu.PrefetchScalarGridSpec(
            num_scalar_prefetch=5,
            grid=(n_tiles,),
            in_specs=[pl.BlockSpec(memory_space=pl.ANY),
                      pl.BlockSpec(memory_space=pl.ANY),
                      pl.BlockSpec(memory_space=pl.ANY),
                      pl.BlockSpec((None, None, 1, D), bias),
                      pl.BlockSpec((None, None, 1, D), bias),
                      pl.BlockSpec((None, None, 1, D), bias)],
            out_specs=pl.BlockSpec(memory_space=pl.ANY),
            scratch_shapes=[pltpu.VMEM((2, tm * SUBLANES, LANES), F32),
                            pltpu.VMEM((2, tm * SUBLANES, LANES), F32),
                            pltpu.VMEM((tm, D), BF16),
                            pltpu.VMEM((D, D), BF16),
                            pltpu.VMEM((D, D), BF16),
                            pltpu.VMEM((D, D), BF16),
                            pltpu.VMEM((2, D, 2 * D), F32),
                            pltpu.VMEM((2, D, D), F32),
                            pltpu.SemaphoreType.DMA((2, 2)),
                            pltpu.SemaphoreType.DMA((2,)),
                            pltpu.SemaphoreType.DMA((2,))]),
        compiler_params=_params(("arbitrary",)),
        name="expert_ffn",
    )(tile_expert, n_valid, tile_group, next_expert, dst, x1r, w_gate_up, w_down, bg, bl, b_down)


def _combine_kernel(y_ref, gate_ref, x1_ref, g_ref, b_ref, o_ref, zs):
    tc = COMBINE_TILE
    gate = gate_ref[...]
    s1 = jnp.zeros((tc, 1), F32)
    for c in range(SUBLANES):
        z = ALPHA * x1_ref[:, c * LANES:(c + 1) * LANES]
        for k in range(TOP_K):
            z = z + gate[:, k:k + 1] * y_ref[pl.ds(k * tc * SUBLANES + c, tc, stride=SUBLANES), :]
        zs[:, c * LANES:(c + 1) * LANES] = z
        s1 = s1 + jnp.sum(z, axis=1, keepdims=True)
    zv = zs[...]
    mu = s1 / zv.shape[1]
    zc = zv - mu
    var = jnp.mean(zc * zc, axis=1, keepdims=True)
    o_ref[...] = zc * lax.rsqrt(var + LN_EPS) * g_ref[...] + b_ref[...]


def _combine(y4, gate, x1, ln_g, ln_b, layer):
    T, D = x1.shape
    tc = COMBINE_TILE
    row = lambda i: (i, 0)
    return pl.pallas_call(
        _combine_kernel,
        out_shape=jax.ShapeDtypeStruct((T, D), F32),
        grid=(T // tc,),
        in_specs=[pl.BlockSpec((TOP_K * tc * SUBLANES, LANES), row),
                  pl.BlockSpec((tc, LANES), row),
                  pl.BlockSpec((tc, D), row),
                  pl.BlockSpec((None, 1, D), lambda i: (layer, 0, 0)),
                  pl.BlockSpec((None, 1, D), lambda i: (layer, 0, 0))],
        out_specs=pl.BlockSpec((tc, D), row),
        scratch_shapes=[pltpu.VMEM((tc, D), F32)],
        compiler_params=_params(("arbitrary",)),
        name="combine_ln",
    )(y4, gate, x1, ln_g.reshape(-1, 1, D), ln_b.reshape(-1, 1, D))


def _moe(x1, x1r, eid, rank, gate, cnt, layer, w_gate_up, bg, bl, w_down, b_down, ln_g, ln_b):
    T, D = x1.shape
    tm = FFN_TILE
    n_tiles = (T * TOP_K) // tm + N_EXPERTS
    n_rows = n_tiles * tm
    assert n_rows <= 1 << DST_BITS and T <= 1 << (31 - DST_BITS)
    counts = cnt[0, :N_EXPERTS].astype(I32)
    tiles_e = (counts + tm - 1) // tm
    tile_end = jnp.cumsum(tiles_e)
    offs = (tile_end - tiles_e) * tm
    n_valid = tile_end[-1]
    jj = jnp.minimum(jnp.arange(n_tiles, dtype=I32), n_valid - 1)
    tile_expert = jnp.minimum(jnp.sum(tile_end[None, :] <= jj[:, None], axis=1), N_EXPERTS - 1).astype(I32)
    off_row = jnp.pad(offs, (0, LANES - N_EXPERTS)).astype(F32).reshape(1, LANES)
    present = tiles_e > 0
    group_of_e = jnp.cumsum(present.astype(I32)) - 1
    eidx = jnp.arange(N_EXPERTS, dtype=I32)
    later = jnp.where(present[None, :] & (eidx[None, :] > eidx[:, None]), eidx[None, :], N_EXPERTS)
    next_e = jnp.min(later, axis=1)
    next_e = jnp.where(next_e == N_EXPERTS, -1, next_e).astype(I32)
    tile_group = group_of_e[tile_expert].astype(I32)
    next_expert = next_e[tile_expert]

    pos = _positions(eid, rank, off_row)
    pos_flat = pos[:, :TOP_K].reshape(-1)
    nv = n_valid.reshape(1).astype(I32)
    dst = _invert(pos_flat, counts, offs.astype(I32), nv, n_rows)
    y4 = _ffn(tile_expert, nv, tile_group, next_expert, dst, x1r, w_gate_up, w_down, bg, bl,
              b_down.reshape(DEPTH, N_EXPERTS, 1, D), layer, n_tiles, n_rows)
    return _combine(y4, gate, x1, ln_g, ln_b, layer)


def kernel(x, even_w_in, pool_w, pool_scale, conv_w, conv_b, i_bias, f_bias, ml_norm, even_w_out, odd_w_in, gla_w2, gla_b, gla_norm, odd_w_out, ln1_g, ln1_b, ln2_g, ln2_b, router_w, router_b, w_gate_up, b_gate_up, w_down, b_down):
    B, S, D = x.shape
    xt = x.reshape(B * S, D)
    bgu = b_gate_up.reshape(DEPTH, N_EXPERTS, D, 2)
    bg = bgu[..., 0].reshape(DEPTH, N_EXPERTS, 1, D)
    bl = bgu[..., 1].reshape(DEPTH, N_EXPERTS, 1, D)
    for layer in range(DEPTH):
        li = layer // 2
        if layer % 2 == 0:
            p = _project(xt, even_w_in, li, 2560)
            mix = _even_mixer(p, xt, B, S, even_w_in, li, pool_w, pool_scale, conv_w, conv_b,
                              i_bias, f_bias, ml_norm)
            w_out = even_w_out
        else:
            p = _project(xt, odd_w_in, li, 3072)
            mix = _odd_mixer(p, xt, B, S, odd_w_in, li, gla_w2, gla_b, gla_norm)
            w_out = odd_w_out
        x1, x1r, eid, rank, gate, cnt = _post(mix, xt, w_out, li, ln1_g, ln1_b, router_w, router_b, layer)
        xt = _moe(x1, x1r, eid, rank, gate, cnt, layer, w_gate_up, bg, bl, w_down, b_down,
                  ln2_g, ln2_b)
    return xt.reshape(B, S, D)
```

```python
import functools

import jax
import jax.numpy as jnp
from jax import lax
from jax.experimental import pallas as pl
from jax.experimental.pallas import tpu as pltpu

F32 = jnp.float32
BF16 = jnp.bfloat16
I32 = jnp.int32

LANES = 128
SUBLANES = 8
VMEM_LIMIT = 56 * 1024 * 1024

DEPTH = 4
ALPHA = (2 * DEPTH) ** 0.25
LN_EPS = 1e-5
NEG = -1e30

POOL_WINDOWS = (2, 4, 8, 16)
POOL_CARRY = 16
CONV_K = 4
CONV_CARRY = 8
HEADS = 4
HEAD_DK = 128
GLA_DV = 256
GLA_RANK = 16
GLA_TAU = 16.0
GLA_SUB = 16
N_EXPERTS = 32
TOP_K = 4
SWIGLU_LIMIT = 7.0
SWIGLU_ALPHA = 1.702

SEQ_CHUNK = 128
POST_TILE = 512
POST_SUB = 256
FFN_TILE = 256
COMBINE_TILE = 128
DISPATCH_GROUP = 128


def _dot(a, b):
    return jnp.dot(a, b, preferred_element_type=F32)


def _dot_nt(a, b):
    return lax.dot_general(a, b, (((1,), (1,)), ((), ())), preferred_element_type=F32)


def _split3(a):
    a1 = a.astype(BF16)
    r1 = a - a1.astype(F32)
    a2 = r1.astype(BF16)
    r2 = r1 - a2.astype(F32)
    return a1, a2, r2.astype(BF16)


def _dot01_left(m01, a):
    a1, a2, a3 = _split3(a)
    return _dot(m01, a1) + _dot(m01, a2) + _dot(m01, a3)


def _dot01_right(a, m01):
    a1, a2, a3 = _split3(a)
    return _dot(a1, m01) + _dot(a2, m01) + _dot(a3, m01)


def _log_sigmoid(x):
    return jnp.minimum(x, 0.0) - jnp.log1p(jnp.exp(-jnp.abs(x)))


def _sigmoid(x):
    return 1.0 / (1.0 + jnp.exp(-x))


def _params(sem):
    return pltpu.CompilerParams(dimension_semantics=sem, vmem_limit_bytes=VMEM_LIMIT)


def _proj_kernel(x_ref, w_hbm, o_ref, wb_ref, stage, sems, *, li, n_cols, tn):
    @pl.when(pl.program_id(0) == 0)
    def _():
        n_chunks = n_cols // tn

        def chunk_copy(n):
            return pltpu.make_async_copy(w_hbm.at[li, :, pl.ds(n * tn, tn)], stage.at[n % 2], sems.at[n % 2])

        chunk_copy(0).start()
        for n in range(n_chunks):
            if n + 1 < n_chunks:
                chunk_copy(n + 1).start()
            chunk_copy(n).wait()
            wb_ref[:, n * tn:(n + 1) * tn] = stage[n % 2].astype(BF16)

    o_ref[...] = _dot(x_ref[...].astype(BF16), wb_ref[...])


def _project(x2d, w3d, li, n_cols, tn=512):
    T, D = x2d.shape
    tm = min(512, T)
    return pl.pallas_call(
        functools.partial(_proj_kernel, li=li, n_cols=n_cols, tn=tn),
        out_shape=jax.ShapeDtypeStruct((T, n_cols), F32),
        grid=(T // tm,),
        in_specs=[pl.BlockSpec((tm, D), lambda i: (i, 0)),
                  pl.BlockSpec(memory_space=pl.ANY)],
        out_specs=pl.BlockSpec((tm, n_cols), lambda i: (i, 0)),
        scratch_shapes=[pltpu.VMEM((D, n_cols), BF16),
                        pltpu.VMEM((2, D, tn), F32),
                        pltpu.SemaphoreType.DMA((2,))],
        compiler_params=_params(("arbitrary",)),
        name="in_proj",
    )(x2d, w3d)


def _even_mixer_kernel(p_ref, x_ref, wgc_ref, wgr_ref, gbr_ref, gbc_ref, poolw_ref, pscale_ref,
                       convw_ref, convb_ref, mlnorm_ref, mix_ref,
                       ubuf, qkbuf, c_st, n_st, m_st, tri):
    L = SEQ_CHUNK
    c = pl.program_id(1)

    @pl.when(c == 0)
    def _():
        ri0 = lax.broadcasted_iota(I32, (L, L), 0)
        ci0 = lax.broadcasted_iota(I32, (L, L), 1)
        tri[0] = jnp.where(ci0 <= ri0, 1.0, 0.0).astype(BF16)
        tri[1] = jnp.where(ri0 <= ci0, 1.0, 0.0).astype(BF16)
        ubuf[0:POOL_CARRY, :] = jnp.zeros((POOL_CARRY, 512), F32)
        qkbuf[0:CONV_CARRY, :] = jnp.zeros((CONV_CARRY, 1024), F32)
        c_st[...] = jnp.zeros_like(c_st)
        n_st[...] = jnp.zeros_like(n_st)
        m_st[...] = jnp.zeros_like(m_st)

    ubuf[POOL_CARRY:POOL_CARRY + L, :] = p_ref[:, 0:512]
    pos1 = lax.broadcasted_iota(I32, (L, 1), 0) + c * L + 1
    for g, w in enumerate(POOL_WINDOWS):
        lo = g * 128
        acc = ubuf[pl.ds(POOL_CARRY, L), lo:lo + 128]
        cur = acc
        for j in range(1, w):
            acc = acc + ubuf[pl.ds(POOL_CARRY - j, L), lo:lo + 128]
        cnt = jnp.minimum(pos1, w).astype(F32)
        d = acc / cnt - cur
        yp = _dot(d.astype(BF16), poolw_ref[g].astype(BF16)) * pscale_ref[:, lo:lo + 128]
        mix_ref[:, lo:lo + 128] = yp.astype(BF16)
    ubuf[0:POOL_CARRY, :] = ubuf[L:L + POOL_CARRY, :]

    qkbuf[CONV_CARRY:CONV_CARRY + L, :] = p_ref[:, 512:1536]
    acc = jnp.zeros((L, 1024), F32) + convb_ref[...]
    for j in range(CONV_K):
        acc = acc + convw_ref[j:j + 1, :] * qkbuf[pl.ds(CONV_CARRY - (CONV_K - 1) + j, L), :]
    qk = acc * _sigmoid(acc)
    qkbuf[0:CONV_CARRY, :] = qkbuf[L:L + CONV_CARRY, :]

    xb = x_ref[...].astype(BF16)
    gcol = _dot(xb, wgc_ref[...].astype(BF16)) + gbr_ref[...]
    grow = _dot_nt(wgr_ref[...].astype(BF16), xb) + gbc_ref[:, 0:1]
    ri = lax.broadcasted_iota(I32, (L, L), 0)
    ci = lax.broadcasted_iota(I32, (L, L), 1)
    causal = ci <= ri
    b_col = _dot01_left(tri[0], _log_sigmoid(gcol))
    b_row = _dot01_right(_log_sigmoid(grow), tri[1])

    kscale = HEAD_DK ** -0.5
    for h in range(HEADS):
        lo = h * 128
        qh = qk[:, lo:lo + 128]
        kh = qk[:, 512 + lo:512 + lo + 128] * kscale
        vh = p_ref[:, 1536 + lo:1536 + lo + 128]
        og = p_ref[:, 2048 + lo:2048 + lo + 128]
        b_c = b_col[:, 4 + h:5 + h]
        b_r = b_row[4 + h:5 + h, :]
        ig_c = gcol[:, h:h + 1]
        ig_r = grow[h:h + 1, :]
        g = b_r[:, L - 1:L]
        m_prev = m_st[h][:, 0:1]
        qb = qh.astype(BF16)
        kb = kh.astype(BF16)
        vb = vh.astype(BF16)

        dmat = jnp.where(causal, b_c - b_r + ig_r, NEG)
        m_inter = b_c + m_prev
        m_t = jnp.maximum(m_inter, jnp.max(dmat, axis=1, keepdims=True))
        sts = _dot_nt(qb, kb) * jnp.exp(dmat - m_t)
        sc = jnp.exp(m_inter - m_t)
        num = _dot(sts.astype(BF16), vb) + sc * _dot(qb, c_st[h].astype(BF16))
        qn = jnp.sum(qh * n_st[h], axis=1, keepdims=True)
        den = jnp.sum(sts, axis=1, keepdims=True) + sc * qn
        hh = num / jnp.maximum(jnp.abs(den), jnp.exp(-m_t))

        a_c = g - b_c + ig_c
        a_r = g - b_r + ig_r
        m_new = jnp.maximum(g + m_prev, jnp.max(a_r, axis=1, keepdims=True))
        decay = jnp.exp(g + m_prev - m_new)
        wk = jnp.exp(a_c - m_new) * kh
        c_st[h] = decay * c_st[h] + _dot(wk.T.astype(BF16), vb)
        n_st[h] = decay * n_st[h] + jnp.sum(wk, axis=0, keepdims=True)
        m_st[h] = jnp.broadcast_to(m_new, (1, LANES))

        mu = jnp.mean(hh, axis=1, keepdims=True)
        hc = hh - mu
        var = jnp.mean(hc * hc, axis=1, keepdims=True)
        hn = hc * lax.rsqrt(var + LN_EPS) * mlnorm_ref[:, lo:lo + 128]
        mix_ref[:, 512 + lo:512 + lo + 128] = (hn * _sigmoid(og)).astype(BF16)


def _even_mixer(p, x2d, B, S, w_in, li, pool_w, pool_scale, conv_w, conv_b, i_bias, f_bias, ml_norm):
    T, D = x2d.shape
    L = SEQ_CHUNK
    nc = S // L
    wg = w_in[li, :, 2560:2568]
    wgc = jnp.pad(wg, ((0, 0), (0, LANES - 8)))
    wgr = wg.T
    gb = jnp.concatenate([i_bias[li], f_bias[li]])
    gbr = jnp.pad(gb, (0, LANES - 8)).reshape(1, LANES)
    gbc = jnp.broadcast_to(gb.reshape(8, 1), (8, LANES))
    row = lambda b, c: (b * nc + c, 0)
    full2 = lambda b, c: (0, 0)
    return pl.pallas_call(
        _even_mixer_kernel,
        out_shape=jax.ShapeDtypeStruct((T, D), BF16),
        grid=(B, nc),
        in_specs=[pl.BlockSpec((L, 2560), row),
                  pl.BlockSpec((L, D), row),
                  pl.BlockSpec((D, LANES), full2),
                  pl.BlockSpec((8, D), full2),
                  pl.BlockSpec((1, LANES), full2),
                  pl.BlockSpec((8, LANES), full2),
                  pl.BlockSpec((None, 4, 128, 128), lambda b, c: (li, 0, 0, 0)),
                  pl.BlockSpec((None, 1, 512), lambda b, c: (li, 0, 0)),
                  pl.BlockSpec((None, CONV_K, 1024), lambda b, c: (li, 0, 0)),
                  pl.BlockSpec((None, 1, 1024), lambda b, c: (li, 0, 0)),
                  pl.BlockSpec((None, 1, 512), lambda b, c: (li, 0, 0))],
        out_specs=pl.BlockSpec((L, D), row),
        scratch_shapes=[pltpu.VMEM((L + POOL_CARRY, 512), F32),
                        pltpu.VMEM((L + CONV_CARRY, 1024), F32),
                        pltpu.VMEM((HEADS, 128, 128), F32),
                        pltpu.VMEM((HEADS, 1, 128), F32),
                        pltpu.VMEM((HEADS, 1, LANES), F32),
                        pltpu.VMEM((2, L, L), BF16)],
        compiler_params=_params(("arbitrary", "arbitrary")),
        name="even_mixer",
    )(p, x2d, wgc, wgr, gbr, gbc, pool_w, pool_scale.reshape(-1, 1, 512), conv_w,
      conv_b.reshape(-1, 1, 1024), ml_norm.reshape(-1, 1, 512))


def _odd_mixer_kernel(p_ref, x_ref, wr_ref, w2_ref, gb_ref, gnorm_ref, mix_ref, s_st):
    L = SEQ_CHUNK
    c = pl.program_id(1)

    @pl.when(c == 0)
    def _():
        s_st[...] = jnp.zeros_like(s_st)

    xb = x_ref[...].astype(BF16)
    glr = _dot(xb, wr_ref[...].astype(BF16))
    z = _dot(glr.astype(BF16), w2_ref[...].astype(BF16)) + gb_ref[...]
    lg = _log_sigmoid(z) / GLA_TAU

    ri = lax.broadcasted_iota(I32, (L, L), 0)
    ci = lax.broadcasted_iota(I32, (L, L), 1)
    one = lambda m: jnp.where(m, 1.0, 0.0).astype(BF16)
    causal = ci <= ri
    blk0 = (ri // GLA_SUB) * GLA_SUB
    m_in = one(causal & (ci >= blk0))
    m_all = one(causal)
    lg3 = _split3(lg)
    cum = lambda m01: _dot(m01, lg3[0]) + _dot(m01, lg3[1]) + _dot(m01, lg3[2])
    b_in = cum(m_in)
    b_all = cum(m_all)

    levels = []
    for hs in (64, 32, 16):
        ps = 2 * hs
        pair0 = (ri // ps) * ps
        refrow = pair0 + hs - 1
        upper = (ri - pair0) >= hs
        dq = cum(one(upper & (ci > refrow) & causal))
        dk = cum(one((~upper) & (ci > ri) & (ci <= refrow)))
        rcol = lax.broadcasted_iota(I32, (L, HEAD_DK), 0)
        up_c = ((rcol % ps) >= hs)
        same_pair = (ri // ps) == (ci // ps)
        levels.append((dq, dk, up_c, same_pair, ps))

    rows16 = lax.broadcasted_iota(I32, (GLA_SUB, HEAD_DK), 0)
    lane16 = lax.broadcasted_iota(I32, (GLA_SUB, LANES), 1)
    eye = ri == ci
    qscale = HEAD_DK ** -0.5
    for h in range(HEADS):
        lo = h * 128
        qh = p_ref[:, lo:lo + 128] * qscale
        kh = p_ref[:, 512 + lo:512 + lo + 128]
        vb = p_ref[:, 1024 + h * GLA_DV:1024 + (h + 1) * GLA_DV].astype(BF16)
        rg = p_ref[:, 2048 + h * GLA_DV:2048 + (h + 1) * GLA_DV]
        bh = b_all[:, lo:lo + 128]
        bin_h = b_in[:, lo:lo + 128]

        o = _dot((qh * jnp.exp(bh)).astype(BF16), s_st[h].astype(BF16))

        amat = jnp.zeros((L, L), F32)
        for dq, dk, up_c, same_pair, ps in levels:
            qt = jnp.where(up_c, qh * jnp.exp(dq[:, lo:lo + 128]), 0.0)
            kt = jnp.where(up_c, 0.0, kh * jnp.exp(dk[:, lo:lo + 128]))
            a_lev = _dot_nt(qt.astype(BF16), kt.astype(BF16))
            amat = amat + (a_lev if ps == L else jnp.where(same_pair, a_lev, 0.0))

        blocks = []
        for blk in range(L // GLA_SUB):
            r0 = blk * GLA_SUB
            qb16 = qh[r0:r0 + GLA_SUB, :]
            kb16 = kh[r0:r0 + GLA_SUB, :]
            bb16 = bin_h[r0:r0 + GLA_SUB, :]
            ablk = jnp.zeros((GLA_SUB, LANES), F32)
            for s in range(GLA_SUB):
                e = jnp.where(rows16 >= s, bb16 - bb16[s:s + 1, :], NEG)
                pr = qb16 * kb16[s:s + 1, :] * jnp.exp(e)
                a = jnp.sum(pr, axis=1, keepdims=True)
                ablk = jnp.where(lane16 == (r0 + s), a, ablk)
            blocks.append(ablk)
        amat = amat + jnp.concatenate(blocks, axis=0)

        o = o + _dot(amat.astype(BF16), vb)

        bl = bh[L - 1:L, :]
        kdec = kh * jnp.exp(bl - bh)
        ecol = jnp.sum(jnp.where(eye, jnp.exp(bl), 0.0), axis=1, keepdims=True)
        s_st[h] = ecol * s_st[h] + _dot(kdec.T.astype(BF16), vb)

        mu = jnp.mean(o, axis=1, keepdims=True)
        oc = o - mu
        var = jnp.mean(oc * oc, axis=1, keepdims=True)
        on = oc * lax.rsqrt(var + LN_EPS) * gnorm_ref[:, h * GLA_DV:(h + 1) * GLA_DV]
        mix_ref[:, h * GLA_DV:(h + 1) * GLA_DV] = (on * (rg * _sigmoid(rg))).astype(BF16)


def _odd_mixer(p, x2d, B, S, w_in, li, gla_w2, gla_b, gla_norm):
    T, D = x2d.shape
    L = SEQ_CHUNK
    nc = S // L
    wr = jnp.pad(w_in[li, :, 3072:3072 + GLA_RANK], ((0, 0), (0, LANES - GLA_RANK)))
    w2 = jnp.pad(gla_w2[li], ((0, LANES - GLA_RANK), (0, 0)))
    row = lambda b, c: (b * nc + c, 0)
    full2 = lambda b, c: (0, 0)
    return pl.pallas_call(
        _odd_mixer_kernel,
        out_shape=jax.ShapeDtypeStruct((T, D), BF16),
        grid=(B, nc),
        in_specs=[pl.BlockSpec((L, 3072), row),
                  pl.BlockSpec((L, D), row),
                  pl.BlockSpec((D, LANES), full2),
                  pl.BlockSpec((LANES, 512), full2),
                  pl.BlockSpec((None, 1, 512), lambda b, c: (li, 0, 0)),
                  pl.BlockSpec((None, 1, 1024), lambda b, c: (li, 0, 0))],
        out_specs=pl.BlockSpec((L, D), row),
        scratch_shapes=[pltpu.VMEM((HEADS, HEAD_DK, GLA_DV), F32)],
        compiler_params=_params(("arbitrary", "arbitrary")),
        name="odd_mixer",
    )(p, x2d, wr, w2, gla_b.reshape(-1, 1, 512), gla_norm.reshape(-1, 1, 1024))


def _post_kernel(mix_ref, x_ref, wout_ref, g_ref, b_ref, rw_ref, rb_ref,
                 x1_ref, x1r_ref, eid_ref, rank_ref, gate_ref, cnt_ref,
                 wb_ref, carry_ref):
    tm = mix_ref.shape[0]
    sub = min(POST_SUB, tm)
    i = pl.program_id(0)

    @pl.when(i == 0)
    def _():
        wb_ref[...] = wout_ref[...].astype(BF16)
        carry_ref[...] = jnp.zeros_like(carry_ref)

    rw = rw_ref[...]
    whi = rw.astype(BF16)
    wlo = (rw - whi.astype(F32)).astype(BF16)
    lane = lax.broadcasted_iota(I32, (sub, LANES), 1)
    lane_f = lane.astype(F32)
    ri = lax.broadcasted_iota(I32, (sub, sub), 0)
    ci = lax.broadcasted_iota(I32, (sub, sub), 1)
    strict = jnp.where(ci < ri, 1.0, 0.0).astype(BF16)
    carry = carry_ref[...]

    for r0 in range(0, tm, sub):
        rows = pl.ds(r0, sub)
        y = _dot(mix_ref[rows, :], wb_ref[...]) + ALPHA * x_ref[rows, :]
        mu = jnp.mean(y, axis=1, keepdims=True)
        yc = y - mu
        var = jnp.mean(yc * yc, axis=1, keepdims=True)
        x1 = yc * lax.rsqrt(var + LN_EPS) * g_ref[...] + b_ref[...]
        x1_ref[rows, :] = x1
        for c in range(SUBLANES):
            x1r_ref[pl.ds(r0 * SUBLANES + c, sub, stride=SUBLANES), :] = x1[:, c * LANES:(c + 1) * LANES]

        hi = x1.astype(BF16)
        lo = (x1 - hi.astype(F32)).astype(BF16)
        logits = _dot(hi, whi) + _dot(lo, whi) + _dot(hi, wlo) + rb_ref[...]

        l = logits
        vals, ids, ohs = [], [], []
        for _ in range(TOP_K):
            m = jnp.max(l, axis=1, keepdims=True)
            idx = jnp.min(jnp.where(l == m, lane_f, float(LANES)), axis=1, keepdims=True)
            oh = lane_f == idx
            vals.append(m)
            ids.append(idx)
            ohs.append(oh)
            l = jnp.where(oh, -jnp.inf, l)
        es = [jnp.exp(v - vals[0]) for v in vals]
        tot = es[0] + es[1] + es[2] + es[3]

        onehot = jnp.zeros((sub, LANES), F32)
        for oh in ohs:
            onehot = onehot + jnp.where(oh, 1.0, 0.0)
        before = _dot(strict, onehot.astype(BF16)) + carry
        carry = carry + jnp.sum(onehot, axis=0, keepdims=True)

        eid = jnp.zeros((sub, LANES), F32)
        rank = jnp.zeros((sub, LANES), F32)
        gate = jnp.zeros((sub, LANES), F32)
        for k in range(TOP_K):
            rk = jnp.sum(jnp.where(ohs[k], before, 0.0), axis=1, keepdims=True)
            sel = lane == k
            eid = jnp.where(sel, ids[k], eid)
            rank = jnp.where(sel, rk, rank)
            gate = jnp.where(sel, es[k] / tot, gate)
        eid_ref[rows, :] = eid.astype(I32)
        rank_ref[rows, :] = rank.astype(I32)
        gate_ref[rows, :] = gate

    carry_ref[...] = carry
    cnt_ref[...] = carry


def _post(mix, x2d, w_out, li, ln_g, ln_b, router_w, router_b, layer):
    T, D = x2d.shape
    tm = min(POST_TILE, T)
    rw = jnp.pad(router_w[layer], ((0, 0), (0, LANES - N_EXPERTS)))
    rb = jnp.pad(router_b[layer], (0, LANES - N_EXPERTS), constant_values=NEG).reshape(1, LANES)
    row = lambda i: (i, 0)
    full2 = lambda i: (0, 0)
    lane_out = jax.ShapeDtypeStruct((T, LANES), I32)
    return pl.pallas_call(
        _post_kernel,
        out_shape=(jax.ShapeDtypeStruct((T, D), F32),
                   jax.ShapeDtypeStruct((T * SUBLANES, LANES), F32),
                   lane_out, lane_out,
                   jax.ShapeDtypeStruct((T, LANES), F32),
                   jax.ShapeDtypeStruct((1, LANES), F32)),
        grid=(T // tm,),
        in_specs=[pl.BlockSpec((tm, D), row),
                  pl.BlockSpec((tm, D), row),
                  pl.BlockSpec((None, D, D), lambda i: (li, 0, 0)),
                  pl.BlockSpec((None, 1, D), lambda i: (layer, 0, 0)),
                  pl.BlockSpec((None, 1, D), lambda i: (layer, 0, 0)),
                  pl.BlockSpec((D, LANES), full2),
                  pl.BlockSpec((1, LANES), full2)],
        out_specs=(pl.BlockSpec((tm, D), row),
                   pl.BlockSpec((tm * SUBLANES, LANES), row),
                   pl.BlockSpec((tm, LANES), row),
                   pl.BlockSpec((tm, LANES), row),
                   pl.BlockSpec((tm, LANES), row),
                   pl.BlockSpec((1, LANES), full2)),
        scratch_shapes=[pltpu.VMEM((D, D), BF16), pltpu.VMEM((1, LANES), F32)],
        compiler_params=_params(("arbitrary",)),
        name="out_proj_ln_router",
    )(mix, x2d, w_out, ln_g.reshape(-1, 1, D), ln_b.reshape(-1, 1, D), rw, rb)


def _pos_kernel(eid_ref, rank_ref, off_ref, pos_ref):
    tm = eid_ref.shape[0]
    lane = lax.broadcasted_iota(I32, (tm, LANES), 1)
    lane_f = lane.astype(F32)
    eid = eid_ref[...].astype(F32)
    off = off_ref[...]
    pos = jnp.zeros((tm, LANES), F32)
    for k in range(TOP_K):
        ek = jnp.sum(jnp.where(lane == k, eid, 0.0), axis=1, keepdims=True)
        ok = jnp.sum(jnp.where(lane_f == ek, off, 0.0), axis=1, keepdims=True)
        pos = jnp.where(lane == k, ok, pos)
    pos_ref[...] = pos.astype(I32) + rank_ref[...]


def _positions(eid, rank, off_row):
    T = eid.shape[0]
    tm = min(1024, T)
    row = lambda i: (i, 0)
    return pl.pallas_call(
        _pos_kernel,
        out_shape=jax.ShapeDtypeStruct((T, LANES), I32),
        grid=(T // tm,),
        in_specs=[pl.BlockSpec((tm, LANES), row), pl.BlockSpec((tm, LANES), row),
                  pl.BlockSpec((1, LANES), lambda i: (0, 0))],
        out_specs=pl.BlockSpec((tm, LANES), row),
        compiler_params=_params(("arbitrary",)),
        name="route_positions",
    )(eid, rank, off_row)


def _dispatch_kernel(pos_ref, cnt_ref, off_ref, nv_ref, x_ref, xs_ref, stage, zbuf, sems, zsem, *, n_groups):
    G = x_ref.shape[0] // SUBLANES
    gi = pl.program_id(0)
    slot = gi % 2

    def row_copy(tt, k):
        src = stage.at[slot, pl.ds(pl.multiple_of(tt * SUBLANES, SUBLANES), SUBLANES), :]
        p = pos_ref[(gi * G + tt) * TOP_K + k]
        dst = xs_ref.at[pl.ds(pl.multiple_of(p * SUBLANES, SUBLANES), SUBLANES), :]
        return pltpu.make_async_copy(src, dst, sems.at[slot])

    def group_wait(s):
        n = G * TOP_K * SUBLANES
        pltpu.make_async_copy(xs_ref.at[pl.ds(0, n), :], xs_ref.at[pl.ds(0, n), :], sems.at[s]).wait()

    @pl.when(gi >= 2)
    def _():
        group_wait(slot)

    stage[slot] = x_ref[...]

    def tok(tt, carry):
        for k in range(TOP_K):
            row_copy(tt, k).start(priority=k % 2)
        return carry

    lax.fori_loop(0, G, tok, 0)

    @pl.when(gi == n_groups - 1)
    def _():
        group_wait(slot)
        if n_groups >= 2:
            group_wait(1 - slot)

        zbuf[...] = jnp.zeros_like(zbuf)

        def pad_expert(e, carry):
            cnt = cnt_ref[e]
            npad = (FFN_TILE - cnt % FFN_TILE) % FFN_TILE
            base = off_ref[e] + cnt

            def zcopy(r):
                dst = xs_ref.at[pl.ds(pl.multiple_of((base + r) * SUBLANES, SUBLANES), SUBLANES), :]
                return pltpu.make_async_copy(zbuf.at[pl.ds(0, SUBLANES), :], dst, zsem)

            def start(r, c2):
                zcopy(r).start()
                return c2

            def wait(r, c2):
                zcopy(r).wait()
                return c2

            lax.fori_loop(0, npad, start, 0)
            lax.fori_loop(0, npad, wait, 0)
            return carry

        lax.fori_loop(0, N_EXPERTS, pad_expert, 0)

        tile_rows = FFN_TILE * SUBLANES
        n_tiles = xs_ref.shape[0] // tile_rows

        def tcopy(j):
            dst = xs_ref.at[pl.ds(pl.multiple_of(j * tile_rows, tile_rows), tile_rows), :]
            return pltpu.make_async_copy(zbuf, dst, zsem)

        def tstart(j, c2):
            tcopy(j).start()
            return c2

        def twait(j, c2):
            tcopy(j).wait()
            return c2

        lax.fori_loop(nv_ref[0], n_tiles, tstart, 0)
        lax.fori_loop(nv_ref[0], n_tiles, twait, 0)


def _dispatch(pos_flat, counts, offs, n_valid, x1r, n_rows):
    T = x1r.shape[0] // SUBLANES
    G = min(DISPATCH_GROUP, T)
    return pl.pallas_call(
        functools.partial(_dispatch_kernel, n_groups=T // G),
        out_shape=jax.ShapeDtypeStruct((n_rows * SUBLANES, LANES), F32),
        grid_spec=pltpu.PrefetchScalarGridSpec(
            num_scalar_prefetch=4,
            grid=(T // G,),
            in_specs=[pl.BlockSpec((G * SUBLANES, LANES), lambda i, *_: (i, 0))],
            out_specs=pl.BlockSpec(memory_space=pl.ANY),
            scratch_shapes=[pltpu.VMEM((2, G * SUBLANES, LANES), F32),
                            pltpu.VMEM((FFN_TILE * SUBLANES, LANES), F32),
                            pltpu.SemaphoreType.DMA((2,)),
                            pltpu.SemaphoreType.DMA(())]),
        compiler_params=_params(("arbitrary",)),
        name="dispatch_rows",
    )(pos_flat, counts, offs, n_valid, x1r)


def _ffn_kernel(te_ref, nv_ref, grp_ref, nxt_ref, xs_ref, wgu_hbm, wd_hbm, bg_ref, bl_ref, bd_ref, y_ref,
                xb_ref, wg_s, wl_s, wd_s, wgu_buf, wd_buf, wsems, *, layer):
    tm = FFN_TILE
    j = pl.program_id(0)
    valid = j < nv_ref[0]
    first = jnp.logical_or(j == 0, te_ref[j] != te_ref[jnp.maximum(j - 1, 0)])
    slot = grp_ref[j] % 2

    def weight_copies(e, s):
        return (pltpu.make_async_copy(wgu_hbm.at[layer, e], wgu_buf.at[s], wsems.at[0, s]),
                pltpu.make_async_copy(wd_hbm.at[layer, e], wd_buf.at[s], wsems.at[1, s]))

    @pl.when(j == 0)
    def _():
        for cp in weight_copies(te_ref[0], 0):
            cp.start()

    @pl.when(jnp.logical_and(valid, first))
    def _():
        for cp in weight_copies(te_ref[j], slot):
            cp.wait()

        @pl.when(nxt_ref[j] >= 0)
        def _():
            for cp in weight_copies(nxt_ref[j], 1 - slot):
                cp.start()

        pi = lax.broadcasted_iota(I32, (256, 256), 0)
        pj = lax.broadcasted_iota(I32, (256, 256), 1)
        src = jnp.where(pj < 128, 2 * pj, 2 * (pj - 128) + 1)
        perm = jnp.where(pi == src, 1.0, 0.0).astype(BF16)
        for blk in range(8):
            r = _dot(wgu_buf[slot, :, blk * 256:(blk + 1) * 256].astype(BF16), perm)
            wg_s[:, blk * 128:(blk + 1) * 128] = r[:, 0:128].astype(BF16)
            wl_s[:, blk * 128:(blk + 1) * 128] = r[:, 128:256].astype(BF16)
        wd_s[...] = wd_buf[slot].astype(BF16)

    @pl.when(valid)
    def _():
        for c in range(SUBLANES):
            xb_ref[:, c * LANES:(c + 1) * LANES] = xs_ref[pl.ds(c, tm, stride=SUBLANES), :].astype(BF16)
        xb = xb_ref[...]
        hg = _dot(xb, wg_s[...]) + bg_ref[...]
        hl = _dot(xb, wl_s[...]) + bl_ref[...]
        glu = jnp.minimum(hg, SWIGLU_LIMIT)
        lin = jnp.clip(hl, -SWIGLU_LIMIT, SWIGLU_LIMIT)
        act = glu * _sigmoid(SWIGLU_ALPHA * glu) * (lin + 1.0)
        y = _dot(act.astype(BF16), wd_s[...]) + bd_ref[...]
        for c in range(SUBLANES):
            y_ref[pl.ds(c, tm, stride=SUBLANES), :] = y[:, c * LANES:(c + 1) * LANES]

    @pl.when(jnp.logical_not(valid))
    def _():
        y_ref[...] = jnp.zeros_like(y_ref)


def _ffn(tile_expert, n_valid, tile_group, next_expert, xs, w_gate_up, w_down, bg, bl, b_down, layer, n_tiles):
    tm = FFN_TILE
    D = w_down.shape[-1]
    tile = lambda j, te, nv, grp, nxt: (jnp.minimum(j, nv[0] - 1), 0)
    out_tile = lambda j, te, nv, grp, nxt: (j, 0)
    bias = lambda j, te, nv, grp, nxt: (layer, te[j], 0, 0)
    return pl.pallas_call(
        functools.partial(_ffn_kernel, layer=layer),
        out_shape=jax.ShapeDtypeStruct(xs.shape, F32),
        grid_spec=pltpu.PrefetchScalarGridSpec(
            num_scalar_prefetch=4,
            grid=(n_tiles,),
            in_specs=[pl.BlockSpec((tm * SUBLANES, LANES), tile),
                      pl.BlockSpec(memory_space=pl.ANY),
                      pl.BlockSpec(memory_space=pl.ANY),
                      pl.BlockSpec((None, None, 1, D), bias),
                      pl.BlockSpec((None, None, 1, D), bias),
                      pl.BlockSpec((None, None, 1, D), bias)],
            out_specs=pl.BlockSpec((tm * SUBLANES, LANES), out_tile),
            scratch_shapes=[pltpu.VMEM((tm, D), BF16),
                            pltpu.VMEM((D, D), BF16),
                            pltpu.VMEM((D, D), BF16),
                            pltpu.VMEM((D, D), BF16),
                            pltpu.VMEM((2, D, 2 * D), F32),
                            pltpu.VMEM((2, D, D), F32),
                            pltpu.SemaphoreType.DMA((2, 2))]),
        compiler_params=_params(("arbitrary",)),
        name="expert_ffn",
    )(tile_expert, n_valid, tile_group, next_expert, xs, w_gate_up, w_down, bg, bl, b_down)


def _combine_kernel(pos_ref, y_ref, gate_ref, x1_ref, g_ref, b_ref, o_ref, ybuf0, ybuf1, zs, sems):
    tc = COMBINE_TILE
    i = pl.program_id(0)
    n = pl.num_programs(0)
    bufs = (ybuf0, ybuf1)
    rows = TOP_K * tc * SUBLANES
    per_chunk = tc // SUBLANES

    def row_copy(tile_base, s, tt, k):
        p = pos_ref[tile_base + tt * TOP_K + k]
        src = y_ref.at[pl.ds(pl.multiple_of(p * SUBLANES, SUBLANES), SUBLANES), :]
        dst = bufs[s].at[pl.ds(pl.multiple_of((k * tc + tt) * SUBLANES, SUBLANES), SUBLANES), :]
        return pltpu.make_async_copy(src, dst, sems.at[s])

    def issue_loop(tile_base, s):
        def tok(tt, carry):
            for k in range(TOP_K):
                row_copy(tile_base, s, tt, k).start(priority=k % 2)
            return carry

        lax.fori_loop(0, tc, tok, 0)

    def slot_wait(s):
        pltpu.make_async_copy(y_ref.at[pl.ds(0, rows), :], bufs[s], sems.at[s]).wait()

    def step(s):
        slot_wait(s)
        next_base = jnp.minimum(i + 1, n - 1) * (tc * TOP_K)
        gate = gate_ref[...]
        s1 = jnp.zeros((tc, 1), F32)
        for c in range(SUBLANES):
            for tt in range(c * per_chunk, (c + 1) * per_chunk):
                for k in range(TOP_K):
                    row_copy(next_base, 1 - s, tt, k).start(priority=k % 2)
            z = ALPHA * x1_ref[:, c * LANES:(c + 1) * LANES]
            for k in range(TOP_K):
                z = z + gate[:, k:k + 1] * bufs[s][pl.ds(k * tc * SUBLANES + c, tc, stride=SUBLANES), :]
            zs[:, c * LANES:(c + 1) * LANES] = z
            s1 = s1 + jnp.sum(z, axis=1, keepdims=True)
        zv = zs[...]
        mu = s1 / zv.shape[1]
        zc = zv - mu
        var = jnp.mean(zc * zc, axis=1, keepdims=True)
        o_ref[...] = zc * lax.rsqrt(var + LN_EPS) * g_ref[...] + b_ref[...]

        @pl.when(i == n - 1)
        def _():
            slot_wait(1 - s)

    @pl.when(i == 0)
    def _():
        issue_loop(0, 0)

    @pl.when(i % 2 == 0)
    def _():
        step(0)

    @pl.when(i % 2 == 1)
    def _():
        step(1)


def _combine(pos_flat, y, gate, x1, ln_g, ln_b, layer):
    T, D = x1.shape
    tc = COMBINE_TILE
    row = lambda i, pos: (i, 0)
    return pl.pallas_call(
        _combine_kernel,
        out_shape=jax.ShapeDtypeStruct((T, D), F32),
        grid_spec=pltpu.PrefetchScalarGridSpec(
            num_scalar_prefetch=1,
            grid=(T // tc,),
            in_specs=[pl.BlockSpec(memory_space=pl.ANY),
                      pl.BlockSpec((tc, LANES), row),
                      pl.BlockSpec((tc, D), row),
                      pl.BlockSpec((None, 1, D), lambda i, pos: (layer, 0, 0)),
                      pl.BlockSpec((None, 1, D), lambda i, pos: (layer, 0, 0))],
            out_specs=pl.BlockSpec((tc, D), row),
            scratch_shapes=[pltpu.VMEM((TOP_K * tc * SUBLANES, LANES), F32),
                            pltpu.VMEM((TOP_K * tc * SUBLANES, LANES), F32),
                            pltpu.VMEM((tc, D), F32),
                            pltpu.SemaphoreType.DMA((2,))]),
        compiler_params=_params(("arbitrary",)),
        name="combine_ln",
    )(pos_flat, y, gate, x1, ln_g.reshape(-1, 1, D), ln_b.reshape(-1, 1, D))


def _moe(x1, x1r, eid, rank, gate, cnt, layer, w_gate_up, bg, bl, w_down, b_down, ln_g, ln_b):
    T, D = x1.shape
    tm = FFN_TILE
    n_tiles = (T * TOP_K) // tm + N_EXPERTS
    n_rows = n_tiles * tm
    counts = cnt[0, :N_EXPERTS].astype(I32)
    tiles_e = (counts + tm - 1) // tm
    tile_end = jnp.cumsum(tiles_e)
    offs = (tile_end - tiles_e) * tm
    n_valid = tile_end[-1]
    jj = jnp.minimum(jnp.arange(n_tiles, dtype=I32), n_valid - 1)
    tile_expert = jnp.minimum(jnp.sum(tile_end[None, :] <= jj[:, None], axis=1), N_EXPERTS - 1).astype(I32)
    off_row = jnp.pad(offs, (0, LANES - N_EXPERTS)).astype(F32).reshape(1, LANES)
    present = tiles_e > 0
    group_of_e = jnp.cumsum(present.astype(I32)) - 1
    eidx = jnp.arange(N_EXPERTS, dtype=I32)
    later = jnp.where(present[None, :] & (eidx[None, :] > eidx[:, None]), eidx[None, :], N_EXPERTS)
    next_e = jnp.min(later, axis=1)
    next_e = jnp.where(next_e == N_EXPERTS, -1, next_e).astype(I32)
    tile_group = group_of_e[tile_expert].astype(I32)
    next_expert = next_e[tile_expert]

    pos = _positions(eid, rank, off_row)
    pos_flat = pos[:, :TOP_K].reshape(-1)
    nv = n_valid.reshape(1).astype(I32)
    xs = _dispatch(pos_flat, counts, offs.astype(I32), nv, x1r, n_rows)
    y = _ffn(tile_expert, nv, tile_group, next_expert, xs, w_gate_up, w_down, bg, bl,
             b_down.reshape(DEPTH, N_EXPERTS, 1, D), layer, n_tiles)
    return _combine(pos_flat, y, gate, x1, ln_g, ln_b, layer)


def kernel(x, even_w_in, pool_w, pool_scale, conv_w, conv_b, i_bias, f_bias, ml_norm, even_w_out, odd_w_in, gla_w2, gla_b, gla_norm, odd_w_out, ln1_g, ln1_b, ln2_g, ln2_b, router_w, router_b, w_gate_up, b_gate_up, w_down, b_down):
    B, S, D = x.shape
    xt = x.reshape(B * S, D)
    bgu = b_gate_up.reshape(DEPTH, N_EXPERTS, D, 2)
    bg = bgu[..., 0].reshape(DEPTH, N_EXPERTS, 1, D)
    bl = bgu[..., 1].reshape(DEPTH, N_EXPERTS, 1, D)
    for layer in range(DEPTH):
        li = layer // 2
        if layer % 2 == 0:
            p = _project(xt, even_w_in, li, 2560)
            mix = _even_mixer(p, xt, B, S, even_w_in, li, pool_w, pool_scale, conv_w, conv_b,
                              i_bias, f_bias, ml_norm)
            w_out = even_w_out
        else:
            p = _project(xt, odd_w_in, li, 3072)
            mix = _odd_mixer(p, xt, B, S, odd_w_in, li, gla_w2, gla_b, gla_norm)
            w_out = odd_w_out
        x1, x1r, eid, rank, gate, cnt = _post(mix, xt, w_out, li, ln1_g, ln1_b, router_w, router_b, layer)
        xt = _moe(x1, x1r, eid, rank, gate, cnt, layer, w_gate_up, bg, bl, w_down, b_down,
                  ln2_g, ln2_b)
    return xt.reshape(B, S, D)
```

```python
import functools

import jax
import jax.numpy as jnp
from jax import lax
from jax.experimental import pallas as pl
from jax.experimental.pallas import tpu as pltpu

F32 = jnp.float32
BF16 = jnp.bfloat16
I32 = jnp.int32

LANES = 128
SUBLANES = 8
VMEM_LIMIT = 56 * 1024 * 1024

DEPTH = 4
ALPHA = (2 * DEPTH) ** 0.25
LN_EPS = 1e-5
NEG = -1e30

POOL_WINDOWS = (2, 4, 8, 16)
POOL_CARRY = 16
CONV_K = 4
CONV_CARRY = 8
HEADS = 4
HEAD_DK = 128
GLA_DV = 256
GLA_RANK = 16
GLA_TAU = 16.0
GLA_SUB = 16
N_EXPERTS = 32
TOP_K = 4
SWIGLU_LIMIT = 7.0
SWIGLU_ALPHA = 1.702

SEQ_CHUNK = 128
POST_TILE = 512
POST_SUB = 128
FFN_TILE = 256
COMBINE_TILE = 128
DISPATCH_GROUP = 128


def _dot(a, b):
    return jnp.dot(a, b, preferred_element_type=F32)


def _dot_nt(a, b):
    return lax.dot_general(a, b, (((1,), (1,)), ((), ())), preferred_element_type=F32)


def _split3(a):
    a1 = a.astype(BF16)
    r1 = a - a1.astype(F32)
    a2 = r1.astype(BF16)
    r2 = r1 - a2.astype(F32)
    return a1, a2, r2.astype(BF16)


def _dot01_left(m01, a):
    a1, a2, a3 = _split3(a)
    return _dot(m01, a1) + _dot(m01, a2) + _dot(m01, a3)


def _dot01_right(a, m01):
    a1, a2, a3 = _split3(a)
    return _dot(a1, m01) + _dot(a2, m01) + _dot(a3, m01)


def _log_sigmoid(x):
    return jnp.minimum(x, 0.0) - jnp.log1p(jnp.exp(-jnp.abs(x)))


def _sigmoid(x):
    return 1.0 / (1.0 + jnp.exp(-x))


def _params(sem):
    return pltpu.CompilerParams(dimension_semantics=sem, vmem_limit_bytes=VMEM_LIMIT)


def _proj_kernel(x_ref, w_hbm, o_ref, wb_ref, stage, sems, *, li, n_cols, tn):
    @pl.when(pl.program_id(0) == 0)
    def _():
        n_chunks = n_cols // tn

        def chunk_copy(n):
            return pltpu.make_async_copy(w_hbm.at[li, :, pl.ds(n * tn, tn)], stage.at[n % 2], sems.at[n % 2])

        chunk_copy(0).start()
        for n in range(n_chunks):
            if n + 1 < n_chunks:
                chunk_copy(n + 1).start()
            chunk_copy(n).wait()
            wb_ref[:, n * tn:(n + 1) * tn] = stage[n % 2].astype(BF16)

    o_ref[...] = _dot(x_ref[...].astype(BF16), wb_ref[...])


def _project(x2d, w3d, li, n_cols, tn=512):
    T, D = x2d.shape
    tm = min(512, T)
    return pl.pallas_call(
        functools.partial(_proj_kernel, li=li, n_cols=n_cols, tn=tn),
        out_shape=jax.ShapeDtypeStruct((T, n_cols), F32),
        grid=(T // tm,),
        in_specs=[pl.BlockSpec((tm, D), lambda i: (i, 0)),
                  pl.BlockSpec(memory_space=pl.ANY)],
        out_specs=pl.BlockSpec((tm, n_cols), lambda i: (i, 0)),
        scratch_shapes=[pltpu.VMEM((D, n_cols), BF16),
                        pltpu.VMEM((2, D, tn), F32),
                        pltpu.SemaphoreType.DMA((2,))],
        compiler_params=_params(("arbitrary",)),
        name="in_proj",
    )(x2d, w3d)


def _even_mixer_kernel(p_ref, x_ref, wgc_ref, wgr_ref, gbr_ref, gbc_ref, poolw_ref, pscale_ref,
                       convw_ref, convb_ref, mlnorm_ref, mix_ref,
                       ubuf, qkbuf, c_st, n_st, m_st, tri):
    L = SEQ_CHUNK
    c = pl.program_id(1)

    @pl.when(c == 0)
    def _():
        ri0 = lax.broadcasted_iota(I32, (L, L), 0)
        ci0 = lax.broadcasted_iota(I32, (L, L), 1)
        tri[0] = jnp.where(ci0 <= ri0, 1.0, 0.0).astype(BF16)
        tri[1] = jnp.where(ri0 <= ci0, 1.0, 0.0).astype(BF16)
        ubuf[0:POOL_CARRY, :] = jnp.zeros((POOL_CARRY, 512), F32)
        qkbuf[0:CONV_CARRY, :] = jnp.zeros((CONV_CARRY, 1024), F32)
        c_st[...] = jnp.zeros_like(c_st)
        n_st[...] = jnp.zeros_like(n_st)
        m_st[...] = jnp.zeros_like(m_st)

    ubuf[POOL_CARRY:POOL_CARRY + L, :] = p_ref[:, 0:512]
    pos1 = lax.broadcasted_iota(I32, (L, 1), 0) + c * L + 1
    for g, w in enumerate(POOL_WINDOWS):
        lo = g * 128
        acc = ubuf[pl.ds(POOL_CARRY, L), lo:lo + 128]
        cur = acc
        for j in range(1, w):
            acc = acc + ubuf[pl.ds(POOL_CARRY - j, L), lo:lo + 128]
        cnt = jnp.minimum(pos1, w).astype(F32)
        d = acc / cnt - cur
        yp = _dot(d.astype(BF16), poolw_ref[g].astype(BF16)) * pscale_ref[:, lo:lo + 128]
        mix_ref[:, lo:lo + 128] = yp.astype(BF16)
    ubuf[0:POOL_CARRY, :] = ubuf[L:L + POOL_CARRY, :]

    qkbuf[CONV_CARRY:CONV_CARRY + L, :] = p_ref[:, 512:1536]
    acc = jnp.zeros((L, 1024), F32) + convb_ref[...]
    for j in range(CONV_K):
        acc = acc + convw_ref[j:j + 1, :] * qkbuf[pl.ds(CONV_CARRY - (CONV_K - 1) + j, L), :]
    qk = acc * _sigmoid(acc)
    qkbuf[0:CONV_CARRY, :] = qkbuf[L:L + CONV_CARRY, :]

    xb = x_ref[...].astype(BF16)
    gcol = _dot(xb, wgc_ref[...].astype(BF16)) + gbr_ref[...]
    grow = _dot_nt(wgr_ref[...].astype(BF16), xb) + gbc_ref[:, 0:1]
    ri = lax.broadcasted_iota(I32, (L, L), 0)
    ci = lax.broadcasted_iota(I32, (L, L), 1)
    causal = ci <= ri
    b_col = _dot01_left(tri[0], _log_sigmoid(gcol))
    b_row = _dot01_right(_log_sigmoid(grow), tri[1])

    kscale = HEAD_DK ** -0.5
    for h in range(HEADS):
        lo = h * 128
        qh = qk[:, lo:lo + 128]
        kh = qk[:, 512 + lo:512 + lo + 128] * kscale
        vh = p_ref[:, 1536 + lo:1536 + lo + 128]
        og = p_ref[:, 2048 + lo:2048 + lo + 128]
        b_c = b_col[:, 4 + h:5 + h]
        b_r = b_row[4 + h:5 + h, :]
        ig_c = gcol[:, h:h + 1]
        ig_r = grow[h:h + 1, :]
        g = b_r[:, L - 1:L]
        m_prev = m_st[h][:, 0:1]
        qb = qh.astype(BF16)
        kb = kh.astype(BF16)
        vb = vh.astype(BF16)

        dmat = jnp.where(causal, b_c - b_r + ig_r, NEG)
        m_inter = b_c + m_prev
        m_t = jnp.maximum(m_inter, jnp.max(dmat, axis=1, keepdims=True))
        sts = _dot_nt(qb, kb) * jnp.exp(dmat - m_t)
        sc = jnp.exp(m_inter - m_t)
        num = _dot(sts.astype(BF16), vb) + sc * _dot(qb, c_st[h].astype(BF16))
        qn = jnp.sum(qh * n_st[h], axis=1, keepdims=True)
        den = jnp.sum(sts, axis=1, keepdims=True) + sc * qn
        hh = num / jnp.maximum(jnp.abs(den), jnp.exp(-m_t))

        a_c = g - b_c + ig_c
        a_r = g - b_r + ig_r
        m_new = jnp.maximum(g + m_prev, jnp.max(a_r, axis=1, keepdims=True))
        decay = jnp.exp(g + m_prev - m_new)
        wk = jnp.exp(a_c - m_new) * kh
        c_st[h] = decay * c_st[h] + _dot(wk.T.astype(BF16), vb)
        n_st[h] = decay * n_st[h] + jnp.sum(wk, axis=0, keepdims=True)
        m_st[h] = jnp.broadcast_to(m_new, (1, LANES))

        mu = jnp.mean(hh, axis=1, keepdims=True)
        hc = hh - mu
        var = jnp.mean(hc * hc, axis=1, keepdims=True)
        hn = hc * lax.rsqrt(var + LN_EPS) * mlnorm_ref[:, lo:lo + 128]
        mix_ref[:, 512 + lo:512 + lo + 128] = (hn * _sigmoid(og)).astype(BF16)


def _even_mixer(p, x2d, B, S, w_in, li, pool_w, pool_scale, conv_w, conv_b, i_bias, f_bias, ml_norm):
    T, D = x2d.shape
    L = SEQ_CHUNK
    nc = S // L
    wg = w_in[li, :, 2560:2568]
    wgc = jnp.pad(wg, ((0, 0), (0, LANES - 8)))
    wgr = wg.T
    gb = jnp.concatenate([i_bias[li], f_bias[li]])
    gbr = jnp.pad(gb, (0, LANES - 8)).reshape(1, LANES)
    gbc = jnp.broadcast_to(gb.reshape(8, 1), (8, LANES))
    row = lambda b, c: (b * nc + c, 0)
    full2 = lambda b, c: (0, 0)
    return pl.pallas_call(
        _even_mixer_kernel,
        out_shape=jax.ShapeDtypeStruct((T, D), BF16),
        grid=(B, nc),
        in_specs=[pl.BlockSpec((L, 2560), row),
                  pl.BlockSpec((L, D), row),
                  pl.BlockSpec((D, LANES), full2),
                  pl.BlockSpec((8, D), full2),
                  pl.BlockSpec((1, LANES), full2),
                  pl.BlockSpec((8, LANES), full2),
                  pl.BlockSpec((None, 4, 128, 128), lambda b, c: (li, 0, 0, 0)),
                  pl.BlockSpec((None, 1, 512), lambda b, c: (li, 0, 0)),
                  pl.BlockSpec((None, CONV_K, 1024), lambda b, c: (li, 0, 0)),
                  pl.BlockSpec((None, 1, 1024), lambda b, c: (li, 0, 0)),
                  pl.BlockSpec((None, 1, 512), lambda b, c: (li, 0, 0))],
        out_specs=pl.BlockSpec((L, D), row),
        scratch_shapes=[pltpu.VMEM((L + POOL_CARRY, 512), F32),
                        pltpu.VMEM((L + CONV_CARRY, 1024), F32),
                        pltpu.VMEM((HEADS, 128, 128), F32),
                        pltpu.VMEM((HEADS, 1, 128), F32),
                        pltpu.VMEM((HEADS, 1, LANES), F32),
                        pltpu.VMEM((2, L, L), BF16)],
        compiler_params=_params(("arbitrary", "arbitrary")),
        name="even_mixer",
    )(p, x2d, wgc, wgr, gbr, gbc, pool_w, pool_scale.reshape(-1, 1, 512), conv_w,
      conv_b.reshape(-1, 1, 1024), ml_norm.reshape(-1, 1, 512))


def _odd_mixer_kernel(p_ref, x_ref, wr_ref, w2_ref, gb_ref, gnorm_ref, mix_ref, s_st):
    L = SEQ_CHUNK
    c = pl.program_id(1)

    @pl.when(c == 0)
    def _():
        s_st[...] = jnp.zeros_like(s_st)

    xb = x_ref[...].astype(BF16)
    glr = _dot(xb, wr_ref[...].astype(BF16))
    z = _dot(glr.astype(BF16), w2_ref[...].astype(BF16)) + gb_ref[...]
    lg = _log_sigmoid(z) / GLA_TAU

    ri = lax.broadcasted_iota(I32, (L, L), 0)
    ci = lax.broadcasted_iota(I32, (L, L), 1)
    one = lambda m: jnp.where(m, 1.0, 0.0).astype(BF16)
    causal = ci <= ri
    blk0 = (ri // GLA_SUB) * GLA_SUB
    m_in = one(causal & (ci >= blk0))
    m_all = one(causal)
    lg3 = _split3(lg)
    cum = lambda m01: _dot(m01, lg3[0]) + _dot(m01, lg3[1]) + _dot(m01, lg3[2])
    b_in = cum(m_in)
    b_all = cum(m_all)

    levels = []
    for hs in (64, 32, 16):
        ps = 2 * hs
        pair0 = (ri // ps) * ps
        refrow = pair0 + hs - 1
        upper = (ri - pair0) >= hs
        dq = cum(one(upper & (ci > refrow) & causal))
        dk = cum(one((~upper) & (ci > ri) & (ci <= refrow)))
        rcol = lax.broadcasted_iota(I32, (L, HEAD_DK), 0)
        up_c = ((rcol % ps) >= hs)
        same_pair = (ri // ps) == (ci // ps)
        levels.append((dq, dk, up_c, same_pair, ps))

    rows16 = lax.broadcasted_iota(I32, (GLA_SUB, LANES), 0)
    lane16 = lax.broadcasted_iota(I32, (GLA_SUB, LANES), 1)
    eye = ri == ci
    qscale = HEAD_DK ** -0.5
    for h in range(HEADS):
        lo = h * 128
        qh = p_ref[:, lo:lo + 128] * qscale
        kh = p_ref[:, 512 + lo:512 + lo + 128]
        vb = p_ref[:, 1024 + h * GLA_DV:1024 + (h + 1) * GLA_DV].astype(BF16)
        rg = p_ref[:, 2048 + h * GLA_DV:2048 + (h + 1) * GLA_DV]
        bh = b_all[:, lo:lo + 128]
        bin_h = b_in[:, lo:lo + 128]

        o = _dot((qh * jnp.exp(bh)).astype(BF16), s_st[h].astype(BF16))

        amat = jnp.zeros((L, L), F32)
        for dq, dk, up_c, same_pair, ps in levels:
            qt = jnp.where(up_c, qh * jnp.exp(dq[:, lo:lo + 128]), 0.0)
            kt = jnp.where(up_c, 0.0, kh * jnp.exp(dk[:, lo:lo + 128]))
            a_lev = _dot_nt(qt.astype(BF16), kt.astype(BF16))
            amat = amat + (a_lev if ps == L else jnp.where(same_pair, a_lev, 0.0))

        blocks = []
        for blk in range(L // GLA_SUB):
            r0 = blk * GLA_SUB
            qb16 = qh[r0:r0 + GLA_SUB, :]
            kb16 = kh[r0:r0 + GLA_SUB, :]
            bb16 = bin_h[r0:r0 + GLA_SUB, :]
            ablk = jnp.zeros((GLA_SUB, LANES), F32)
            for s in range(GLA_SUB):
                pr = qb16 * kb16[s:s + 1, :] * jnp.exp(bb16 - bb16[s:s + 1, :])
                a = jnp.sum(pr, axis=1, keepdims=True)
                ablk = jnp.where(lane16 == (r0 + s), a, ablk)
            blocks.append(jnp.where(lane16 - r0 <= rows16, ablk, 0.0))
        amat = amat + jnp.concatenate(blocks, axis=0)

        o = o + _dot(amat.astype(BF16), vb)

        bl = bh[L - 1:L, :]
        kdec = kh * jnp.exp(bl - bh)
        ecol = jnp.sum(jnp.where(eye, jnp.exp(bl), 0.0), axis=1, keepdims=True)
        s_st[h] = ecol * s_st[h] + _dot(kdec.T.astype(BF16), vb)

        mu = jnp.mean(o, axis=1, keepdims=True)
        oc = o - mu
        var = jnp.mean(oc * oc, axis=1, keepdims=True)
        on = oc * lax.rsqrt(var + LN_EPS) * gnorm_ref[:, h * GLA_DV:(h + 1) * GLA_DV]
        mix_ref[:, h * GLA_DV:(h + 1) * GLA_DV] = (on * (rg * _sigmoid(rg))).astype(BF16)


def _odd_mixer(p, x2d, B, S, w_in, li, gla_w2, gla_b, gla_norm):
    T, D = x2d.shape
    L = SEQ_CHUNK
    nc = S // L
    wr = jnp.pad(w_in[li, :, 3072:3072 + GLA_RANK], ((0, 0), (0, LANES - GLA_RANK)))
    w2 = jnp.pad(gla_w2[li], ((0, LANES - GLA_RANK), (0, 0)))
    row = lambda b, c: (b * nc + c, 0)
    full2 = lambda b, c: (0, 0)
    return pl.pallas_call(
        _odd_mixer_kernel,
        out_shape=jax.ShapeDtypeStruct((T, D), BF16),
        grid=(B, nc),
        in_specs=[pl.BlockSpec((L, 3072), row),
                  pl.BlockSpec((L, D), row),
                  pl.BlockSpec((D, LANES), full2),
                  pl.BlockSpec((LANES, 512), full2),
                  pl.BlockSpec((None, 1, 512), lambda b, c: (li, 0, 0)),
                  pl.BlockSpec((None, 1, 1024), lambda b, c: (li, 0, 0))],
        out_specs=pl.BlockSpec((L, D), row),
        scratch_shapes=[pltpu.VMEM((HEADS, HEAD_DK, GLA_DV), F32)],
        compiler_params=_params(("arbitrary", "arbitrary")),
        name="odd_mixer",
    )(p, x2d, wr, w2, gla_b.reshape(-1, 1, 512), gla_norm.reshape(-1, 1, 1024))


def _post_kernel(mix_ref, x_ref, wout_ref, g_ref, b_ref, rw_ref, rb_ref,
                 x1_ref, x1r_ref, eid_ref, rank_ref, gate_ref, cnt_ref,
                 wb_ref, carry_ref):
    tm = mix_ref.shape[0]
    sub = min(POST_SUB, tm)
    i = pl.program_id(0)

    @pl.when(i == 0)
    def _():
        wb_ref[...] = wout_ref[...].astype(BF16)
        carry_ref[...] = jnp.zeros_like(carry_ref)

    rw = rw_ref[...]
    whi = rw.astype(BF16)
    wlo = (rw - whi.astype(F32)).astype(BF16)
    lane = lax.broadcasted_iota(I32, (sub, LANES), 1)
    lane_f = lane.astype(F32)
    ri = lax.broadcasted_iota(I32, (sub, sub), 0)
    ci = lax.broadcasted_iota(I32, (sub, sub), 1)
    strict = jnp.where(ci < ri, 1.0, 0.0).astype(BF16)
    carry = carry_ref[...]

    blocks = [pl.ds(r0, sub) for r0 in range(0, tm, sub)]
    nb = len(blocks)
    x1s = []
    for rows in blocks:
        y = _dot(mix_ref[rows, :], wb_ref[...]) + ALPHA * x_ref[rows, :]
        mu = jnp.mean(y, axis=1, keepdims=True)
        yc = y - mu
        var = jnp.mean(yc * yc, axis=1, keepdims=True)
        x1s.append(yc * lax.rsqrt(var + LN_EPS) * g_ref[...] + b_ref[...])
    for b, rows in enumerate(blocks):
        x1_ref[rows, :] = x1s[b]
        for c in range(SUBLANES):
            x1r_ref[pl.ds(b * sub * SUBLANES + c, sub, stride=SUBLANES), :] = x1s[b][:, c * LANES:(c + 1) * LANES]

    ls = []
    for x1 in x1s:
        hi = x1.astype(BF16)
        lo = (x1 - hi.astype(F32)).astype(BF16)
        ls.append(_dot(hi, whi) + _dot(lo, whi) + _dot(hi, wlo) + rb_ref[...])

    vals = [[] for _ in range(nb)]
    ids = [[] for _ in range(nb)]
    ohs = [[] for _ in range(nb)]
    for _ in range(TOP_K):
        ms = [jnp.max(l, axis=1, keepdims=True) for l in ls]
        idxs = [jnp.min(jnp.where(l == m, lane_f, float(LANES)), axis=1, keepdims=True) for l, m in zip(ls, ms)]
        for b in range(nb):
            oh = lane_f == idxs[b]
            vals[b].append(ms[b])
            ids[b].append(idxs[b])
            ohs[b].append(oh)
            ls[b] = jnp.where(oh, -jnp.inf, ls[b])

    for b, rows in enumerate(blocks):
        es = [jnp.exp(v - vals[b][0]) for v in vals[b]]
        tot = es[0] + es[1] + es[2] + es[3]
        onehot = jnp.zeros((sub, LANES), F32)
        for oh in ohs[b]:
            onehot = onehot + jnp.where(oh, 1.0, 0.0)
        before = _dot(strict, onehot.astype(BF16)) + carry
        carry = carry + jnp.sum(onehot, axis=0, keepdims=True)

        eid = jnp.zeros((sub, LANES), F32)
        rank = jnp.zeros((sub, LANES), F32)
        gate = jnp.zeros((sub, LANES), F32)
        for k in range(TOP_K):
            rk = jnp.sum(jnp.where(ohs[b][k], before, 0.0), axis=1, keepdims=True)
            sel = lane == k
            eid = jnp.where(sel, ids[b][k], eid)
            rank = jnp.where(sel, rk, rank)
            gate = jnp.where(sel, es[k] / tot, gate)
        eid_ref[rows, :] = eid.astype(I32)
        rank_ref[rows, :] = rank.astype(I32)
        gate_ref[rows, :] = gate

    carry_ref[...] = carry
    cnt_ref[...] = carry


def _post(mix, x2d, w_out, li, ln_g, ln_b, router_w, router_b, layer):
    T, D = x2d.shape
    tm = min(POST_TILE, T)
    rw = jnp.pad(router_w[layer], ((0, 0), (0, LANES - N_EXPERTS)))
    rb = jnp.pad(router_b[layer], (0, LANES - N_EXPERTS), constant_values=NEG).reshape(1, LANES)
    row = lambda i: (i, 0)
    full2 = lambda i: (0, 0)
    lane_out = jax.ShapeDtypeStruct((T, LANES), I32)
    return pl.pallas_call(
        _post_kernel,
        out_shape=(jax.ShapeDtypeStruct((T, D), F32),
                   jax.ShapeDtypeStruct((T * SUBLANES, LANES), F32),
                   lane_out, lane_out,
                   jax.ShapeDtypeStruct((T, LANES), F32),
                   jax.ShapeDtypeStruct((1, LANES), F32)),
        grid=(T // tm,),
        in_specs=[pl.BlockSpec((tm, D), row),
                  pl.BlockSpec((tm, D), row),
                  pl.BlockSpec((None, D, D), lambda i: (li, 0, 0)),
                  pl.BlockSpec((None, 1, D), lambda i: (layer, 0, 0)),
                  pl.BlockSpec((None, 1, D), lambda i: (layer, 0, 0)),
                  pl.BlockSpec((D, LANES), full2),
                  pl.BlockSpec((1, LANES), full2)],
        out_specs=(pl.BlockSpec((tm, D), row),
                   pl.BlockSpec((tm * SUBLANES, LANES), row),
                   pl.BlockSpec((tm, LANES), row),
                   pl.BlockSpec((tm, LANES), row),
                   pl.BlockSpec((tm, LANES), row),
                   pl.BlockSpec((1, LANES), full2)),
        scratch_shapes=[pltpu.VMEM((D, D), BF16), pltpu.VMEM((1, LANES), F32)],
        compiler_params=_params(("arbitrary",)),
        name="out_proj_ln_router",
    )(mix, x2d, w_out, ln_g.reshape(-1, 1, D), ln_b.reshape(-1, 1, D), rw, rb)


def _pos_kernel(eid_ref, rank_ref, off_ref, pos_ref):
    tm = eid_ref.shape[0]
    lane = lax.broadcasted_iota(I32, (tm, LANES), 1)
    lane_f = lane.astype(F32)
    eid = eid_ref[...].astype(F32)
    off = off_ref[...]
    pos = jnp.zeros((tm, LANES), F32)
    for k in range(TOP_K):
        ek = jnp.sum(jnp.where(lane == k, eid, 0.0), axis=1, keepdims=True)
        ok = jnp.sum(jnp.where(lane_f == ek, off, 0.0), axis=1, keepdims=True)
        pos = jnp.where(lane == k, ok, pos)
    pos_ref[...] = pos.astype(I32) + rank_ref[...]


def _positions(eid, rank, off_row):
    T = eid.shape[0]
    tm = min(1024, T)
    row = lambda i: (i, 0)
    return pl.pallas_call(
        _pos_kernel,
        out_shape=jax.ShapeDtypeStruct((T, LANES), I32),
        grid=(T // tm,),
        in_specs=[pl.BlockSpec((tm, LANES), row), pl.BlockSpec((tm, LANES), row),
                  pl.BlockSpec((1, LANES), lambda i: (0, 0))],
        out_specs=pl.BlockSpec((tm, LANES), row),
        compiler_params=_params(("arbitrary",)),
        name="route_positions",
    )(eid, rank, off_row)


def _dispatch_kernel(pos_ref, cnt_ref, off_ref, nv_ref, x_ref, xs_ref, stage, zbuf, sems, zsem, *, n_groups):
    G = x_ref.shape[0] // SUBLANES
    gi = pl.program_id(0)
    slot = gi % 2

    def row_copy(tt, k):
        src = stage.at[slot, pl.ds(pl.multiple_of(tt * SUBLANES, SUBLANES), SUBLANES), :]
        p = pos_ref[(gi * G + tt) * TOP_K + k]
        dst = xs_ref.at[pl.ds(pl.multiple_of(p * SUBLANES, SUBLANES), SUBLANES), :]
        return pltpu.make_async_copy(src, dst, sems.at[slot])

    def group_wait(s):
        n = G * TOP_K * SUBLANES
        pltpu.make_async_copy(xs_ref.at[pl.ds(0, n), :], xs_ref.at[pl.ds(0, n), :], sems.at[s]).wait()

    @pl.when(gi >= 2)
    def _():
        group_wait(slot)

    stage[slot] = x_ref[...]

    def tok(tt, carry):
        for k in range(TOP_K):
            row_copy(tt, k).start(priority=k % 2)
        return carry

    lax.fori_loop(0, G, tok, 0)

    @pl.when(gi == n_groups - 1)
    def _():
        group_wait(slot)
        if n_groups >= 2:
            group_wait(1 - slot)

        zbuf[...] = jnp.zeros_like(zbuf)

        def pad_expert(e, carry):
            cnt = cnt_ref[e]
            npad = (FFN_TILE - cnt % FFN_TILE) % FFN_TILE
            base = off_ref[e] + cnt

            def zcopy(r):
                dst = xs_ref.at[pl.ds(pl.multiple_of((base + r) * SUBLANES, SUBLANES), SUBLANES), :]
                return pltpu.make_async_copy(zbuf.at[pl.ds(0, SUBLANES), :], dst, zsem)

            def start(r, c2):
                zcopy(r).start()
                return c2

            def wait(r, c2):
                zcopy(r).wait()
                return c2

            lax.fori_loop(0, npad, start, 0)
            lax.fori_loop(0, npad, wait, 0)
            return carry

        lax.fori_loop(0, N_EXPERTS, pad_expert, 0)

        tile_rows = FFN_TILE * SUBLANES
        n_tiles = xs_ref.shape[0] // tile_rows

        def tcopy(j):
            dst = xs_ref.at[pl.ds(pl.multiple_of(j * tile_rows, tile_rows), tile_rows), :]
            return pltpu.make_async_copy(zbuf, dst, zsem)

        def tstart(j, c2):
            tcopy(j).start()
            return c2

        def twait(j, c2):
            tcopy(j).wait()
            return c2

        lax.fori_loop(nv_ref[0], n_tiles, tstart, 0)
        lax.fori_loop(nv_ref[0], n_tiles, twait, 0)


def _dispatch(pos_flat, counts, offs, n_valid, x1r, n_rows):
    T = x1r.shape[0] // SUBLANES
    G = min(DISPATCH_GROUP, T)
    return pl.pallas_call(
        functools.partial(_dispatch_kernel, n_groups=T // G),
        out_shape=jax.ShapeDtypeStruct((n_rows * SUBLANES, LANES), F32),
        grid_spec=pltpu.PrefetchScalarGridSpec(
            num_scalar_prefetch=4,
            grid=(T // G,),
            in_specs=[pl.BlockSpec((G * SUBLANES, LANES), lambda i, *_: (i, 0))],
            out_specs=pl.BlockSpec(memory_space=pl.ANY),
            scratch_shapes=[pltpu.VMEM((2, G * SUBLANES, LANES), F32),
                            pltpu.VMEM((FFN_TILE * SUBLANES, LANES), F32),
                            pltpu.SemaphoreType.DMA((2,)),
                            pltpu.SemaphoreType.DMA(())]),
        compiler_params=_params(("arbitrary",)),
        name="dispatch_rows",
    )(pos_flat, counts, offs, n_valid, x1r)


def _ffn_kernel(te_ref, nv_ref, grp_ref, nxt_ref, xs_ref, wgu_hbm, wd_hbm, bg_ref, bl_ref, bd_ref, y_ref,
                xb_ref, wg_s, wl_s, wd_s, wgu_buf, wd_buf, wsems, *, layer):
    tm = FFN_TILE
    j = pl.program_id(0)
    valid = j < nv_ref[0]
    first = jnp.logical_or(j == 0, te_ref[j] != te_ref[jnp.maximum(j - 1, 0)])
    slot = grp_ref[j] % 2

    def weight_copies(e, s):
        return (pltpu.make_async_copy(wgu_hbm.at[layer, e], wgu_buf.at[s], wsems.at[0, s]),
                pltpu.make_async_copy(wd_hbm.at[layer, e], wd_buf.at[s], wsems.at[1, s]))

    @pl.when(j == 0)
    def _():
        for cp in weight_copies(te_ref[0], 0):
            cp.start()

    @pl.when(jnp.logical_and(valid, first))
    def _():
        for cp in weight_copies(te_ref[j], slot):
            cp.wait()

        @pl.when(nxt_ref[j] >= 0)
        def _():
            for cp in weight_copies(nxt_ref[j], 1 - slot):
                cp.start()

        pi = lax.broadcasted_iota(I32, (256, 256), 0)
        pj = lax.broadcasted_iota(I32, (256, 256), 1)
        src = jnp.where(pj < 128, 2 * pj, 2 * (pj - 128) + 1)
        perm = jnp.where(pi == src, 1.0, 0.0).astype(BF16)
        for blk in range(8):
            r = _dot(wgu_buf[slot, :, blk * 256:(blk + 1) * 256].astype(BF16), perm)
            wg_s[:, blk * 128:(blk + 1) * 128] = r[:, 0:128].astype(BF16)
            wl_s[:, blk * 128:(blk + 1) * 128] = r[:, 128:256].astype(BF16)
        wd_s[...] = wd_buf[slot].astype(BF16)

    @pl.when(valid)
    def _():
        for c in range(SUBLANES):
            xb_ref[:, c * LANES:(c + 1) * LANES] = xs_ref[pl.ds(c, tm, stride=SUBLANES), :].astype(BF16)
        xb = xb_ref[...]
        hg = _dot(xb, wg_s[...]) + bg_ref[...]
        hl = _dot(xb, wl_s[...]) + bl_ref[...]
        glu = jnp.minimum(hg, SWIGLU_LIMIT)
        lin = jnp.clip(hl, -SWIGLU_LIMIT, SWIGLU_LIMIT)
        act = glu * _sigmoid(SWIGLU_ALPHA * glu) * (lin + 1.0)
        y = _dot(act.astype(BF16), wd_s[...]) + bd_ref[...]
        for c in range(SUBLANES):
            y_ref[pl.ds(c, tm, stride=SUBLANES), :] = y[:, c * LANES:(c + 1) * LANES]

    @pl.when(jnp.logical_not(valid))
    def _():
        y_ref[...] = jnp.zeros_like(y_ref)


def _ffn(tile_expert, n_valid, tile_group, next_expert, xs, w_gate_up, w_down, bg, bl, b_down, layer, n_tiles):
    tm = FFN_TILE
    D = w_down.shape[-1]
    tile = lambda j, te, nv, grp, nxt: (jnp.minimum(j, nv[0] - 1), 0)
    out_tile = lambda j, te, nv, grp, nxt: (j, 0)
    bias = lambda j, te, nv, grp, nxt: (layer, te[j], 0, 0)
    return pl.pallas_call(
        functools.partial(_ffn_kernel, layer=layer),
        out_shape=jax.ShapeDtypeStruct(xs.shape, F32),
        grid_spec=pltpu.PrefetchScalarGridSpec(
            num_scalar_prefetch=4,
            grid=(n_tiles,),
            in_specs=[pl.BlockSpec((tm * SUBLANES, LANES), tile),
                      pl.BlockSpec(memory_space=pl.ANY),
                      pl.BlockSpec(memory_space=pl.ANY),
                      pl.BlockSpec((None, None, 1, D), bias),
                      pl.BlockSpec((None, None, 1, D), bias),
                      pl.BlockSpec((None, None, 1, D), bias)],
            out_specs=pl.BlockSpec((tm * SUBLANES, LANES), out_tile),
            scratch_shapes=[pltpu.VMEM((tm, D), BF16),
                            pltpu.VMEM((D, D), BF16),
                            pltpu.VMEM((D, D), BF16),
                            pltpu.VMEM((D, D), BF16),
                            pltpu.VMEM((2, D, 2 * D), F32),
                            pltpu.VMEM((2, D, D), F32),
                            pltpu.SemaphoreType.DMA((2, 2))]),
        compiler_params=_params(("arbitrary",)),
        name="expert_ffn",
    )(tile_expert, n_valid, tile_group, next_expert, xs, w_gate_up, w_down, bg, bl, b_down)


def _combine_kernel(pos_ref, y_ref, gate_ref, x1_ref, g_ref, b_ref, o_ref, ybuf0, ybuf1, zs, sems):
    tc = COMBINE_TILE
    i = pl.program_id(0)
    n = pl.num_programs(0)
    bufs = (ybuf0, ybuf1)
    rows = TOP_K * tc * SUBLANES
    per_chunk = tc // SUBLANES

    def row_copy(tile_base, s, tt, k):
        p = pos_ref[tile_base + tt * TOP_K + k]
        src = y_ref.at[pl.ds(pl.multiple_of(p * SUBLANES, SUBLANES), SUBLANES), :]
        dst = bufs[s].at[pl.ds(pl.multiple_of((k * tc + tt) * SUBLANES, SUBLANES), SUBLANES), :]
        return pltpu.make_async_copy(src, dst, sems.at[s])

    def issue_loop(tile_base, s):
        def tok(tt, carry):
            for k in range(TOP_K):
                row_copy(tile_base, s, tt, k).start(priority=k % 2)
            return carry

        lax.fori_loop(0, tc, tok, 0)

    def slot_wait(s):
        pltpu.make_async_copy(y_ref.at[pl.ds(0, rows), :], bufs[s], sems.at[s]).wait()

    def step(s):
        slot_wait(s)
        next_base = jnp.minimum(i + 1, n - 1) * (tc * TOP_K)
        gate = gate_ref[...]
        s1 = jnp.zeros((tc, 1), F32)
        for c in range(SUBLANES):
            for tt in range(c * per_chunk, (c + 1) * per_chunk):
                for k in range(TOP_K):
                    row_copy(next_base, 1 - s, tt, k).start(priority=k % 2)
            z = ALPHA * x1_ref[:, c * LANES:(c + 1) * LANES]
            for k in range(TOP_K):
                z = z + gate[:, k:k + 1] * bufs[s][pl.ds(k * tc * SUBLANES + c, tc, stride=SUBLANES), :]
            zs[:, c * LANES:(c + 1) * LANES] = z
            s1 = s1 + jnp.sum(z, axis=1, keepdims=True)
        zv = zs[...]
        mu = s1 / zv.shape[1]
        zc = zv - mu
        var = jnp.mean(zc * zc, axis=1, keepdims=True)
        o_ref[...] = zc * lax.rsqrt(var + LN_EPS) * g_ref[...] + b_ref[...]

        @pl.when(i == n - 1)
        def _():
            slot_wait(1 - s)

    @pl.when(i == 0)
    def _():
        issue_loop(0, 0)

    @pl.when(i % 2 == 0)
    def _():
        step(0)

    @pl.when(i % 2 == 1)
    def _():
        step(1)


def _combine(pos_flat, y, gate, x1, ln_g, ln_b, layer):
    T, D = x1.shape
    tc = COMBINE_TILE
    row = lambda i, pos: (i, 0)
    return pl.pallas_call(
        _combine_kernel,
        out_shape=jax.ShapeDtypeStruct((T, D), F32),
        grid_spec=pltpu.PrefetchScalarGridSpec(
            num_scalar_prefetch=1,
            grid=(T // tc,),
            in_specs=[pl.BlockSpec(memory_space=pl.ANY),
                      pl.BlockSpec((tc, LANES), row),
                      pl.BlockSpec((tc, D), row),
                      pl.BlockSpec((None, 1, D), lambda i, pos: (layer, 0, 0)),
                      pl.BlockSpec((None, 1, D), lambda i, pos: (layer, 0, 0))],
            out_specs=pl.BlockSpec((tc, D), row),
            scratch_shapes=[pltpu.VMEM((TOP_K * tc * SUBLANES, LANES), F32),
                            pltpu.VMEM((TOP_K * tc * SUBLANES, LANES), F32),
                            pltpu.VMEM((tc, D), F32),
                            pltpu.SemaphoreType.DMA((2,))]),
        compiler_params=_params(("arbitrary",)),
        name="combine_ln",
    )(pos_flat, y, gate, x1, ln_g.reshape(-1, 1, D), ln_b.reshape(-1, 1, D))


def _moe(x1, x1r, eid, rank, gate, cnt, layer, w_gate_up, bg, bl, w_down, b_down, ln_g, ln_b):
    T, D = x1.shape
    tm = FFN_TILE
    n_tiles = (T * TOP_K) // tm + N_EXPERTS
    n_rows = n_tiles * tm
    counts = cnt[0, :N_EXPERTS].astype(I32)
    tiles_e = (counts + tm - 1) // tm
    tile_end = jnp.cumsum(tiles_e)
    offs = (tile_end - tiles_e) * tm
    n_valid = tile_end[-1]
    jj = jnp.minimum(jnp.arange(n_tiles, dtype=I32), n_valid - 1)
    tile_expert = jnp.minimum(jnp.sum(tile_end[None, :] <= jj[:, None], axis=1), N_EXPERTS - 1).astype(I32)
    off_row = jnp.pad(offs, (0, LANES - N_EXPERTS)).astype(F32).reshape(1, LANES)
    present = tiles_e > 0
    group_of_e = jnp.cumsum(present.astype(I32)) - 1
    eidx = jnp.arange(N_EXPERTS, dtype=I32)
    later = jnp.where(present[None, :] & (eidx[None, :] > eidx[:, None]), eidx[None, :], N_EXPERTS)
    next_e = jnp.min(later, axis=1)
    next_e = jnp.where(next_e == N_EXPERTS, -1, next_e).astype(I32)
    tile_group = group_of_e[tile_expert].astype(I32)
    next_expert = next_e[tile_expert]

    pos = _positions(eid, rank, off_row)
    pos_flat = pos[:, :TOP_K].reshape(-1)
    nv = n_valid.reshape(1).astype(I32)
    xs = _dispatch(pos_flat, counts, offs.astype(I32), nv, x1r, n_rows)
    y = _ffn(tile_expert, nv, tile_group, next_expert, xs, w_gate_up, w_down, bg, bl,
             b_down.reshape(DEPTH, N_EXPERTS, 1, D), layer, n_tiles)
    return _combine(pos_flat, y, gate, x1, ln_g, ln_b, layer)


def kernel(x, even_w_in, pool_w, pool_scale, conv_w, conv_b, i_bias, f_bias, ml_norm, even_w_out, odd_w_in, gla_w2, gla_b, gla_norm, odd_w_out, ln1_g, ln1_b, ln2_g, ln2_b, router_w, router_b, w_gate_up, b_gate_up, w_down, b_down):
    B, S, D = x.shape
    xt = x.reshape(B * S, D)
    bgu = b_gate_up.reshape(DEPTH, N_EXPERTS, D, 2)
    bg = bgu[..., 0].reshape(DEPTH, N_EXPERTS, 1, D)
    bl = bgu[..., 1].reshape(DEPTH, N_EXPERTS, 1, D)
    for layer in range(DEPTH):
        li = layer // 2
        if layer % 2 == 0:
            p = _project(xt, even_w_in, li, 2560)
            mix = _even_mixer(p, xt, B, S, even_w_in, li, pool_w, pool_scale, conv_w, conv_b,
                              i_bias, f_bias, ml_norm)
            w_out = even_w_out
        else:
            p = _project(xt, odd_w_in, li, 3072)
            mix = _odd_mixer(p, xt, B, S, odd_w_in, li, gla_w2, gla_b, gla_norm)
            w_out = odd_w_out
        x1, x1r, eid, rank, gate, cnt = _post(mix, xt, w_out, li, ln1_g, ln1_b, router_w, router_b, layer)
        xt = _moe(x1, x1r, eid, rank, gate, cnt, layer, w_gate_up, bg, bl, w_down, b_down,
                  ln2_g, ln2_b)
    return xt.reshape(B, S, D)
```

```python
import functools

import jax
import jax.numpy as jnp
from jax import lax
from jax.experimental import pallas as pl
from jax.experimental.pallas import tpu as pltpu

F32 = jnp.float32
BF16 = jnp.bfloat16
I32 = jnp.int32

LANES = 128
SUBLANES = 8
VMEM_LIMIT = 56 * 1024 * 1024

DEPTH = 4
ALPHA = (2 * DEPTH) ** 0.25
LN_EPS = 1e-5
NEG = -1e30
LOG2_E = 1.4426950408889634

POOL_WINDOWS = (2, 4, 8, 16)
POOL_CARRY = 16
CONV_K = 4
CONV_CARRY = 8
HEADS = 4
HEAD_DK = 128
GLA_DV = 256
GLA_RANK = 16
GLA_TAU = 16.0
GLA_SUB = 16
N_EXPERTS = 32
TOP_K = 4
SWIGLU_LIMIT = 7.0
SWIGLU_ALPHA = 1.702

SEQ_CHUNK = 128
POST_TILE = 512
POST_SUB = 128
FFN_TILE = 256
COMBINE_TILE = 128
DISPATCH_GROUP = 128


def _dot(a, b):
    return jnp.dot(a, b, preferred_element_type=F32)


def _dot_nt(a, b):
    return lax.dot_general(a, b, (((1,), (1,)), ((), ())), preferred_element_type=F32)


def _split3(a):
    a1 = a.astype(BF16)
    r1 = a - a1.astype(F32)
    a2 = r1.astype(BF16)
    r2 = r1 - a2.astype(F32)
    return a1, a2, r2.astype(BF16)


def _dot01_left(m01, a):
    a1, a2, a3 = _split3(a)
    return _dot(m01, a1) + _dot(m01, a2) + _dot(m01, a3)


def _dot01_right(a, m01):
    a1, a2, a3 = _split3(a)
    return _dot(a1, m01) + _dot(a2, m01) + _dot(a3, m01)


def _log_sigmoid(x):
    return jnp.minimum(x, 0.0) - jnp.log1p(jnp.exp(-jnp.abs(x)))


def _sigmoid(x):
    return 1.0 / (1.0 + jnp.exp(-x))


def _params(sem):
    return pltpu.CompilerParams(dimension_semantics=sem, vmem_limit_bytes=VMEM_LIMIT)


def _proj_kernel(x_ref, w_hbm, o_ref, wb_ref, stage, sems, *, li, n_cols, tn):
    @pl.when(pl.program_id(0) == 0)
    def _():
        n_chunks = n_cols // tn

        def chunk_copy(n):
            return pltpu.make_async_copy(w_hbm.at[li, :, pl.ds(n * tn, tn)], stage.at[n % 2], sems.at[n % 2])

        chunk_copy(0).start()
        for n in range(n_chunks):
            if n + 1 < n_chunks:
                chunk_copy(n + 1).start()
            chunk_copy(n).wait()
            wb_ref[:, n * tn:(n + 1) * tn] = stage[n % 2].astype(BF16)

    o_ref[...] = _dot(x_ref[...].astype(BF16), wb_ref[...])


def _project(x2d, w3d, li, n_cols, tn=512):
    T, D = x2d.shape
    tm = min(512, T)
    return pl.pallas_call(
        functools.partial(_proj_kernel, li=li, n_cols=n_cols, tn=tn),
        out_shape=jax.ShapeDtypeStruct((T, n_cols), F32),
        grid=(T // tm,),
        in_specs=[pl.BlockSpec((tm, D), lambda i: (i, 0)),
                  pl.BlockSpec(memory_space=pl.ANY)],
        out_specs=pl.BlockSpec((tm, n_cols), lambda i: (i, 0)),
        scratch_shapes=[pltpu.VMEM((D, n_cols), BF16),
                        pltpu.VMEM((2, D, tn), F32),
                        pltpu.SemaphoreType.DMA((2,))],
        compiler_params=_params(("arbitrary",)),
        name="in_proj",
    )(x2d, w3d)


def _even_mixer_kernel(p_ref, x_ref, wgc_ref, wgr_ref, gbr_ref, gbc_ref, poolw_ref, pscale_ref,
                       convw_ref, convb_ref, mlnorm_ref, mix_ref,
                       ubuf, qkbuf, c_st, n_st, m_st, tri):
    L = SEQ_CHUNK
    c = pl.program_id(1)

    @pl.when(c == 0)
    def _():
        ri0 = lax.broadcasted_iota(I32, (L, L), 0)
        ci0 = lax.broadcasted_iota(I32, (L, L), 1)
        tri[0] = jnp.where(ci0 <= ri0, 1.0, 0.0).astype(BF16)
        tri[1] = jnp.where(ri0 <= ci0, 1.0, 0.0).astype(BF16)
        ubuf[0:POOL_CARRY, :] = jnp.zeros((POOL_CARRY, 512), F32)
        qkbuf[0:CONV_CARRY, :] = jnp.zeros((CONV_CARRY, 1024), F32)
        c_st[...] = jnp.zeros_like(c_st)
        n_st[...] = jnp.zeros_like(n_st)
        m_st[...] = jnp.zeros_like(m_st)

    ubuf[POOL_CARRY:POOL_CARRY + L, :] = p_ref[:, 0:512]
    pos1 = lax.broadcasted_iota(I32, (L, 1), 0) + c * L + 1
    for g, w in enumerate(POOL_WINDOWS):
        lo = g * 128
        acc = ubuf[pl.ds(POOL_CARRY, L), lo:lo + 128]
        cur = acc
        for j in range(1, w):
            acc = acc + ubuf[pl.ds(POOL_CARRY - j, L), lo:lo + 128]
        inv_cnt = 1.0 / jnp.minimum(pos1, w).astype(F32)
        d = acc * inv_cnt - cur
        yp = _dot(d.astype(BF16), poolw_ref[g].astype(BF16)) * pscale_ref[:, lo:lo + 128]
        mix_ref[:, lo:lo + 128] = yp.astype(BF16)
    ubuf[0:POOL_CARRY, :] = ubuf[L:L + POOL_CARRY, :]

    qkbuf[CONV_CARRY:CONV_CARRY + L, :] = p_ref[:, 512:1536]
    acc = jnp.zeros((L, 1024), F32) + convb_ref[...]
    for j in range(CONV_K):
        acc = acc + convw_ref[j:j + 1, :] * qkbuf[pl.ds(CONV_CARRY - (CONV_K - 1) + j, L), :]
    qk = acc * _sigmoid(acc)
    qkbuf[0:CONV_CARRY, :] = qkbuf[L:L + CONV_CARRY, :]

    xb = x_ref[...].astype(BF16)
    gcol = _dot(xb, wgc_ref[...].astype(BF16)) + gbr_ref[...]
    grow = _dot_nt(wgr_ref[...].astype(BF16), xb) + gbc_ref[:, 0:1]
    ri = lax.broadcasted_iota(I32, (L, L), 0)
    ci = lax.broadcasted_iota(I32, (L, L), 1)
    causal = ci <= ri
    b_col = _dot01_left(tri[0], _log_sigmoid(gcol))
    b_row = _dot01_right(_log_sigmoid(grow), tri[1])

    kscale = HEAD_DK ** -0.5
    H = range(HEADS)
    qh = [qk[:, h * 128:(h + 1) * 128] for h in H]
    kh = [qk[:, 512 + h * 128:512 + (h + 1) * 128] * kscale for h in H]
    qb = [q.astype(BF16) for q in qh]
    kb = [k.astype(BF16) for k in kh]
    vb = [p_ref[:, 1536 + h * 128:1536 + (h + 1) * 128].astype(BF16) for h in H]
    b_c = [b_col[:, 4 + h:5 + h] for h in H]
    b_r = [b_row[4 + h:5 + h, :] for h in H]
    ig_c = [gcol[:, h:h + 1] for h in H]
    ig_r = [grow[h:h + 1, :] for h in H]
    g = [b_r[h][:, L - 1:L] for h in H]
    m_prev = [m_st[h][:, 0:1] for h in H]

    dmat = [jnp.where(causal, b_c[h] - b_r[h] + ig_r[h], NEG) for h in H]
    m_inter = [b_c[h] + m_prev[h] for h in H]
    dmax = [jnp.max(dmat[h], axis=1, keepdims=True) for h in H]
    qkt = [_dot_nt(qb[h], kb[h]) for h in H]
    qc = [_dot(qb[h], c_st[h].astype(BF16)) for h in H]
    qn = [jnp.sum(qh[h] * n_st[h], axis=1, keepdims=True) for h in H]
    m_t = [jnp.maximum(m_inter[h], dmax[h]) for h in H]
    sts = [qkt[h] * jnp.exp(dmat[h] - m_t[h]) for h in H]
    sc = [jnp.exp(m_inter[h] - m_t[h]) for h in H]
    sv = [_dot(sts[h].astype(BF16), vb[h]) for h in H]
    ssum = [jnp.sum(sts[h], axis=1, keepdims=True) for h in H]

    a_c = [g[h] - b_c[h] + ig_c[h] for h in H]
    a_r = [g[h] - b_r[h] + ig_r[h] for h in H]
    m_new = [jnp.maximum(g[h] + m_prev[h], jnp.max(a_r[h], axis=1, keepdims=True)) for h in H]
    decay = [jnp.exp(g[h] + m_prev[h] - m_new[h]) for h in H]
    wk = [jnp.exp(a_c[h] - m_new[h]) * kh[h] for h in H]
    wkt = [wk[h].T.astype(BF16) for h in H]
    kv = [_dot(wkt[h], vb[h]) for h in H]

    num = [sv[h] + sc[h] * qc[h] for h in H]
    den = [ssum[h] + sc[h] * qn[h] for h in H]
    hh = [num[h] * (1.0 / jnp.maximum(jnp.abs(den[h]), jnp.exp(-m_t[h]))) for h in H]
    mu = [jnp.mean(hh[h], axis=1, keepdims=True) for h in H]
    hc = [hh[h] - mu[h] for h in H]
    var = [jnp.mean(hc[h] * hc[h], axis=1, keepdims=True) for h in H]
    for h in H:
        lo = h * 128
        c_st[h] = decay[h] * c_st[h] + kv[h]
        n_st[h] = decay[h] * n_st[h] + jnp.sum(wk[h], axis=0, keepdims=True)
        m_st[h] = jnp.broadcast_to(m_new[h], (1, LANES))
        og = p_ref[:, 2048 + lo:2048 + lo + 128]
        hn = hc[h] * lax.rsqrt(var[h] + LN_EPS) * mlnorm_ref[:, lo:lo + 128]
        mix_ref[:, 512 + lo:512 + lo + 128] = (hn * _sigmoid(og)).astype(BF16)


def _even_mixer(p, x2d, B, S, w_in, li, pool_w, pool_scale, conv_w, conv_b, i_bias, f_bias, ml_norm):
    T, D = x2d.shape
    L = SEQ_CHUNK
    nc = S // L
    wg = w_in[li, :, 2560:2568]
    wgc = jnp.pad(wg, ((0, 0), (0, LANES - 8)))
    wgr = wg.T
    gb = jnp.concatenate([i_bias[li], f_bias[li]])
    gbr = jnp.pad(gb, (0, LANES - 8)).reshape(1, LANES)
    gbc = jnp.broadcast_to(gb.reshape(8, 1), (8, LANES))
    row = lambda b, c: (b * nc + c, 0)
    full2 = lambda b, c: (0, 0)
    return pl.pallas_call(
        _even_mixer_kernel,
        out_shape=jax.ShapeDtypeStruct((T, D), BF16),
        grid=(B, nc),
        in_specs=[pl.BlockSpec((L, 2560), row),
                  pl.BlockSpec((L, D), row),
                  pl.BlockSpec((D, LANES), full2),
                  pl.BlockSpec((8, D), full2),
                  pl.BlockSpec((1, LANES), full2),
                  pl.BlockSpec((8, LANES), full2),
                  pl.BlockSpec((None, 4, 128, 128), lambda b, c: (li, 0, 0, 0)),
                  pl.BlockSpec((None, 1, 512), lambda b, c: (li, 0, 0)),
                  pl.BlockSpec((None, CONV_K, 1024), lambda b, c: (li, 0, 0)),
                  pl.BlockSpec((None, 1, 1024), lambda b, c: (li, 0, 0)),
                  pl.BlockSpec((None, 1, 512), lambda b, c: (li, 0, 0))],
        out_specs=pl.BlockSpec((L, D), row),
        scratch_shapes=[pltpu.VMEM((L + POOL_CARRY, 512), F32),
                        pltpu.VMEM((L + CONV_CARRY, 1024), F32),
                        pltpu.VMEM((HEADS, 128, 128), F32),
                        pltpu.VMEM((HEADS, 1, 128), F32),
                        pltpu.VMEM((HEADS, 1, LANES), F32),
                        pltpu.VMEM((2, L, L), BF16)],
        compiler_params=_params(("arbitrary", "arbitrary")),
        name="even_mixer",
    )(p, x2d, wgc, wgr, gbr, gbc, pool_w, pool_scale.reshape(-1, 1, 512), conv_w,
      conv_b.reshape(-1, 1, 1024), ml_norm.reshape(-1, 1, 512))


def _odd_mixer_kernel(p_ref, x_ref, wr_ref, w2_ref, gb_ref, gnorm_ref, mix_ref, s_st):
    L = SEQ_CHUNK
    c = pl.program_id(1)

    @pl.when(c == 0)
    def _():
        s_st[...] = jnp.zeros_like(s_st)

    xb = x_ref[...].astype(BF16)
    glr = _dot(xb, wr_ref[...].astype(BF16))
    z = _dot(glr.astype(BF16), w2_ref[...].astype(BF16)) + gb_ref[...]
    lg = _log_sigmoid(z) * (1.0 / GLA_TAU)

    ri = lax.broadcasted_iota(I32, (L, L), 0)
    ci = lax.broadcasted_iota(I32, (L, L), 1)
    one = lambda m: jnp.where(m, 1.0, 0.0).astype(BF16)
    causal = ci <= ri
    blk0 = (ri // GLA_SUB) * GLA_SUB
    m_in = one(causal & (ci >= blk0))
    m_all = one(causal)
    lg3 = _split3(lg)
    cum = lambda m01: _dot(m01, lg3[0]) + _dot(m01, lg3[1]) + _dot(m01, lg3[2])
    b_in = cum(m_in)
    b_all = cum(m_all)

    levels = []
    for hs in (64, 32, 16):
        ps = 2 * hs
        pair0 = (ri // ps) * ps
        refrow = pair0 + hs - 1
        upper = (ri - pair0) >= hs
        dq = cum(one(upper & (ci > refrow) & causal))
        dk = cum(one((~upper) & (ci > ri) & (ci <= refrow)))
        rcol = lax.broadcasted_iota(I32, (L, HEAD_DK), 0)
        up_c = ((rcol % ps) >= hs)
        same_pair = (ri // ps) == (ci // ps)
        levels.append((dq, dk, up_c, same_pair, ps))

    rows16 = lax.broadcasted_iota(I32, (GLA_SUB, LANES), 0)
    lane16 = lax.broadcasted_iota(I32, (GLA_SUB, LANES), 1)
    eye = ri == ci
    qscale = HEAD_DK ** -0.5
    H = range(HEADS)
    qh = [p_ref[:, h * 128:(h + 1) * 128] * qscale for h in H]
    kh = [p_ref[:, 512 + h * 128:512 + (h + 1) * 128] for h in H]
    vb = [p_ref[:, 1024 + h * GLA_DV:1024 + (h + 1) * GLA_DV].astype(BF16) for h in H]
    bh = [b_all[:, h * 128:(h + 1) * 128] for h in H]
    bin_h = [b_in[:, h * 128:(h + 1) * 128] for h in H]

    o0 = [_dot((qh[h] * jnp.exp(bh[h])).astype(BF16), s_st[h].astype(BF16)) for h in H]

    amat = [jnp.zeros((L, L), F32) for _ in H]
    for dq, dk, up_c, same_pair, ps in levels:
        qt = [jnp.where(up_c, qh[h] * jnp.exp(dq[:, h * 128:(h + 1) * 128]), 0.0).astype(BF16) for h in H]
        kt = [jnp.where(up_c, 0.0, kh[h] * jnp.exp(dk[:, h * 128:(h + 1) * 128])).astype(BF16) for h in H]
        a_lev = [_dot_nt(qt[h], kt[h]) for h in H]
        amat = [amat[h] + (a_lev[h] if ps == L else jnp.where(same_pair, a_lev[h], 0.0)) for h in H]

    bl = [bh[h][L - 1:L, :] for h in H]
    kdec_t = [(kh[h] * jnp.exp(bl[h] - bh[h])).T.astype(BF16) for h in H]
    ecol = [jnp.sum(jnp.where(eye, jnp.exp(bl[h]), 0.0), axis=1, keepdims=True) for h in H]
    kv = [_dot(kdec_t[h], vb[h]) for h in H]

    blocks = [[] for _ in H]
    for blk in range(L // GLA_SUB):
        r0 = blk * GLA_SUB
        qb16 = [qh[h][r0:r0 + GLA_SUB, :] for h in H]
        kb16 = [kh[h][r0:r0 + GLA_SUB, :] for h in H]
        bb16 = [bin_h[h][r0:r0 + GLA_SUB, :] * LOG2_E for h in H]
        ablk = [jnp.zeros((GLA_SUB, LANES), F32) for _ in H]
        for s in range(GLA_SUB):
            for h in H:
                pr = qb16[h] * kb16[h][s:s + 1, :] * jnp.exp2(bb16[h] - bb16[h][s:s + 1, :])
                a = jnp.sum(pr, axis=1, keepdims=True)
                ablk[h] = jnp.where(lane16 == (r0 + s), a, ablk[h])
        for h in H:
            blocks[h].append(jnp.where(lane16 - r0 <= rows16, ablk[h], 0.0))
    amat = [amat[h] + jnp.concatenate(blocks[h], axis=0) for h in H]

    o = [o0[h] + _dot(amat[h].astype(BF16), vb[h]) for h in H]
    mu = [jnp.mean(o[h], axis=1, keepdims=True) for h in H]
    oc = [o[h] - mu[h] for h in H]
    var = [jnp.mean(oc[h] * oc[h], axis=1, keepdims=True) for h in H]
    for h in H:
        s_st[h] = ecol[h] * s_st[h] + kv[h]
        rg = p_ref[:, 2048 + h * GLA_DV:2048 + (h + 1) * GLA_DV]
        on = oc[h] * lax.rsqrt(var[h] + LN_EPS) * gnorm_ref[:, h * GLA_DV:(h + 1) * GLA_DV]
        mix_ref[:, h * GLA_DV:(h + 1) * GLA_DV] = (on * (rg * _sigmoid(rg))).astype(BF16)


def _odd_mixer(p, x2d, B, S, w_in, li, gla_w2, gla_b, gla_norm):
    T, D = x2d.shape
    L = SEQ_CHUNK
    nc = S // L
    wr = jnp.pad(w_in[li, :, 3072:3072 + GLA_RANK], ((0, 0), (0, LANES - GLA_RANK)))
    w2 = jnp.pad(gla_w2[li], ((0, LANES - GLA_RANK), (0, 0)))
    row = lambda b, c: (b * nc + c, 0)
    full2 = lambda b, c: (0, 0)
    return pl.pallas_call(
        _odd_mixer_kernel,
        out_shape=jax.ShapeDtypeStruct((T, D), BF16),
        grid=(B, nc),
        in_specs=[pl.BlockSpec((L, 3072), row),
                  pl.BlockSpec((L, D), row),
                  pl.BlockSpec((D, LANES), full2),
                  pl.BlockSpec((LANES, 512), full2),
                  pl.BlockSpec((None, 1, 512), lambda b, c: (li, 0, 0)),
                  pl.BlockSpec((None, 1, 1024), lambda b, c: (li, 0, 0))],
        out_specs=pl.BlockSpec((L, D), row),
        scratch_shapes=[pltpu.VMEM((HEADS, HEAD_DK, GLA_DV), F32)],
        compiler_params=_params(("arbitrary", "arbitrary")),
        name="odd_mixer",
    )(p, x2d, wr, w2, gla_b.reshape(-1, 1, 512), gla_norm.reshape(-1, 1, 1024))


def _post_kernel(mix_ref, x_ref, wout_ref, g_ref, b_ref, rw_ref, rb_ref,
                 x1_ref, x1r_ref, eid_ref, rank_ref, gate_ref, cnt_ref,
                 wb_ref, carry_ref):
    tm = mix_ref.shape[0]
    sub = min(POST_SUB, tm)
    i = pl.program_id(0)

    @pl.when(i == 0)
    def _():
        wb_ref[...] = wout_ref[...].astype(BF16)
        carry_ref[...] = jnp.zeros_like(carry_ref)

    rw = rw_ref[...]
    whi = rw.astype(BF16)
    wlo = (rw - whi.astype(F32)).astype(BF16)
    lane = lax.broadcasted_iota(I32, (sub, LANES), 1)
    lane_f = lane.astype(F32)
    ri = lax.broadcasted_iota(I32, (sub, sub), 0)
    ci = lax.broadcasted_iota(I32, (sub, sub), 1)
    strict = jnp.where(ci < ri, 1.0, 0.0).astype(BF16)
    carry = carry_ref[...]

    blocks = [pl.ds(r0, sub) for r0 in range(0, tm, sub)]
    nb = len(blocks)
    x1s = []
    for rows in blocks:
        y = _dot(mix_ref[rows, :], wb_ref[...]) + ALPHA * x_ref[rows, :]
        mu = jnp.mean(y, axis=1, keepdims=True)
        yc = y - mu
        var = jnp.mean(yc * yc, axis=1, keepdims=True)
        x1s.append(yc * lax.rsqrt(var + LN_EPS) * g_ref[...] + b_ref[...])
    for b, rows in enumerate(blocks):
        x1_ref[rows, :] = x1s[b]
        for c in range(SUBLANES):
            x1r_ref[pl.ds(b * sub * SUBLANES + c, sub, stride=SUBLANES), :] = x1s[b][:, c * LANES:(c + 1) * LANES]

    ls = []
    for x1 in x1s:
        hi = x1.astype(BF16)
        lo = (x1 - hi.astype(F32)).astype(BF16)
        ls.append(_dot(hi, whi) + _dot(lo, whi) + _dot(hi, wlo) + rb_ref[...])

    vals = [[] for _ in range(nb)]
    ids = [[] for _ in range(nb)]
    ohs = [[] for _ in range(nb)]
    for _ in range(TOP_K):
        ms = [jnp.max(l, axis=1, keepdims=True) for l in ls]
        idxs = [jnp.min(jnp.where(l == m, lane_f, float(LANES)), axis=1, keepdims=True) for l, m in zip(ls, ms)]
        for b in range(nb):
            oh = lane_f == idxs[b]
            vals[b].append(ms[b])
            ids[b].append(idxs[b])
            ohs[b].append(oh)
            ls[b] = jnp.where(oh, -jnp.inf, ls[b])

    for b, rows in enumerate(blocks):
        es = [jnp.exp(v - vals[b][0]) for v in vals[b]]
        tot = es[0] + es[1] + es[2] + es[3]
        onehot = jnp.zeros((sub, LANES), F32)
        for oh in ohs[b]:
            onehot = onehot + jnp.where(oh, 1.0, 0.0)
        before = _dot(strict, onehot.astype(BF16)) + carry
        carry = carry + jnp.sum(onehot, axis=0, keepdims=True)

        eid = jnp.zeros((sub, LANES), F32)
        rank = jnp.zeros((sub, LANES), F32)
        gate = jnp.zeros((sub, LANES), F32)
        for k in range(TOP_K):
            rk = jnp.sum(jnp.where(ohs[b][k], before, 0.0), axis=1, keepdims=True)
            sel = lane == k
            eid = jnp.where(sel, ids[b][k], eid)
            rank = jnp.where(sel, rk, rank)
            gate = jnp.where(sel, es[k] / tot, gate)
        eid_ref[rows, :] = eid.astype(I32)
        rank_ref[rows, :] = rank.astype(I32)
        gate_ref[rows, :] = gate

    carry_ref[...] = carry
    cnt_ref[...] = carry


def _post(mix, x2d, w_out, li, ln_g, ln_b, router_w, router_b, layer):
    T, D = x2d.shape
    tm = min(POST_TILE, T)
    rw = jnp.pad(router_w[layer], ((0, 0), (0, LANES - N_EXPERTS)))
    rb = jnp.pad(router_b[layer], (0, LANES - N_EXPERTS), constant_values=NEG).reshape(1, LANES)
    row = lambda i: (i, 0)
    full2 = lambda i: (0, 0)
    lane_out = jax.ShapeDtypeStruct((T, LANES), I32)
    return pl.pallas_call(
        _post_kernel,
        out_shape=(jax.ShapeDtypeStruct((T, D), F32),
                   jax.ShapeDtypeStruct((T * SUBLANES, LANES), F32),
                   lane_out, lane_out,
                   jax.ShapeDtypeStruct((T, LANES), F32),
                   jax.ShapeDtypeStruct((1, LANES), F32)),
        grid=(T // tm,),
        in_specs=[pl.BlockSpec((tm, D), row),
                  pl.BlockSpec((tm, D), row),
                  pl.BlockSpec((None, D, D), lambda i: (li, 0, 0)),
                  pl.BlockSpec((None, 1, D), lambda i: (layer, 0, 0)),
                  pl.BlockSpec((None, 1, D), lambda i: (layer, 0, 0)),
                  pl.BlockSpec((D, LANES), full2),
                  pl.BlockSpec((1, LANES), full2)],
        out_specs=(pl.BlockSpec((tm, D), row),
                   pl.BlockSpec((tm * SUBLANES, LANES), row),
                   pl.BlockSpec((tm, LANES), row),
                   pl.BlockSpec((tm, LANES), row),
                   pl.BlockSpec((tm, LANES), row),
                   pl.BlockSpec((1, LANES), full2)),
        scratch_shapes=[pltpu.VMEM((D, D), BF16), pltpu.VMEM((1, LANES), F32)],
        compiler_params=_params(("arbitrary",)),
        name="out_proj_ln_router",
    )(mix, x2d, w_out, ln_g.reshape(-1, 1, D), ln_b.reshape(-1, 1, D), rw, rb)


def _pos_kernel(eid_ref, rank_ref, off_ref, pos_ref):
    tm = eid_ref.shape[0]
    lane = lax.broadcasted_iota(I32, (tm, LANES), 1)
    lane_f = lane.astype(F32)
    eid = eid_ref[...].astype(F32)
    off = off_ref[...]
    pos = jnp.zeros((tm, LANES), F32)
    for k in range(TOP_K):
        ek = jnp.sum(jnp.where(lane == k, eid, 0.0), axis=1, keepdims=True)
        ok = jnp.sum(jnp.where(lane_f == ek, off, 0.0), axis=1, keepdims=True)
        pos = jnp.where(lane == k, ok, pos)
    pos_ref[...] = pos.astype(I32) + rank_ref[...]


def _positions(eid, rank, off_row):
    T = eid.shape[0]
    tm = min(1024, T)
    row = lambda i: (i, 0)
    return pl.pallas_call(
        _pos_kernel,
        out_shape=jax.ShapeDtypeStruct((T, LANES), I32),
        grid=(T // tm,),
        in_specs=[pl.BlockSpec((tm, LANES), row), pl.BlockSpec((tm, LANES), row),
                  pl.BlockSpec((1, LANES), lambda i: (0, 0))],
        out_specs=pl.BlockSpec((tm, LANES), row),
        compiler_params=_params(("arbitrary",)),
        name="route_positions",
    )(eid, rank, off_row)


def _dispatch_kernel(pos_ref, cnt_ref, off_ref, nv_ref, x_ref, xs_ref, stage, zbuf, sems, zsem, *, n_groups):
    G = x_ref.shape[0] // SUBLANES
    gi = pl.program_id(0)
    slot = gi % 2

    def row_copy(tt, k):
        src = stage.at[slot, pl.ds(pl.multiple_of(tt * SUBLANES, SUBLANES), SUBLANES), :]
        p = pos_ref[(gi * G + tt) * TOP_K + k]
        dst = xs_ref.at[pl.ds(pl.multiple_of(p * SUBLANES, SUBLANES), SUBLANES), :]
        return pltpu.make_async_copy(src, dst, sems.at[slot])

    def group_wait(s):
        n = G * TOP_K * SUBLANES
        pltpu.make_async_copy(xs_ref.at[pl.ds(0, n), :], xs_ref.at[pl.ds(0, n), :], sems.at[s]).wait()

    @pl.when(gi >= 2)
    def _():
        group_wait(slot)

    stage[slot] = x_ref[...]

    def tok(tt, carry):
        for k in range(TOP_K):
            row_copy(tt, k).start(priority=k % 2)
        return carry

    lax.fori_loop(0, G, tok, 0)

    @pl.when(gi == n_groups - 1)
    def _():
        group_wait(slot)
        if n_groups >= 2:
            group_wait(1 - slot)

        zbuf[...] = jnp.zeros_like(zbuf)

        def pad_expert(e, carry):
            cnt = cnt_ref[e]
            npad = (FFN_TILE - cnt % FFN_TILE) % FFN_TILE
            base = off_ref[e] + cnt

            def zcopy(r):
                dst = xs_ref.at[pl.ds(pl.multiple_of((base + r) * SUBLANES, SUBLANES), SUBLANES), :]
                return pltpu.make_async_copy(zbuf.at[pl.ds(0, SUBLANES), :], dst, zsem)

            def start(r, c2):
                zcopy(r).start()
                return c2

            def wait(r, c2):
                zcopy(r).wait()
                return c2

            lax.fori_loop(0, npad, start, 0)
            lax.fori_loop(0, npad, wait, 0)
            return carry

        lax.fori_loop(0, N_EXPERTS, pad_expert, 0)

        tile_rows = FFN_TILE * SUBLANES
        n_tiles = xs_ref.shape[0] // tile_rows

        def tcopy(j):
            dst = xs_ref.at[pl.ds(pl.multiple_of(j * tile_rows, tile_rows), tile_rows), :]
            return pltpu.make_async_copy(zbuf, dst, zsem)

        def tstart(j, c2):
            tcopy(j).start()
            return c2

        def twait(j, c2):
            tcopy(j).wait()
            return c2

        lax.fori_loop(nv_ref[0], n_tiles, tstart, 0)
        lax.fori_loop(nv_ref[0], n_tiles, twait, 0)


def _dispatch(pos_flat, counts, offs, n_valid, x1r, n_rows):
    T = x1r.shape[0] // SUBLANES
    G = min(DISPATCH_GROUP, T)
    return pl.pallas_call(
        functools.partial(_dispatch_kernel, n_groups=T // G),
        out_shape=jax.ShapeDtypeStruct((n_rows * SUBLANES, LANES), F32),
        grid_spec=pltpu.PrefetchScalarGridSpec(
            num_scalar_prefetch=4,
            grid=(T // G,),
            in_specs=[pl.BlockSpec((G * SUBLANES, LANES), lambda i, *_: (i, 0))],
            out_specs=pl.BlockSpec(memory_space=pl.ANY),
            scratch_shapes=[pltpu.VMEM((2, G * SUBLANES, LANES), F32),
                            pltpu.VMEM((FFN_TILE * SUBLANES, LANES), F32),
                            pltpu.SemaphoreType.DMA((2,)),
                            pltpu.SemaphoreType.DMA(())]),
        compiler_params=_params(("arbitrary",)),
        name="dispatch_rows",
    )(pos_flat, counts, offs, n_valid, x1r)


def _ffn_kernel(te_ref, nv_ref, grp_ref, nxt_ref, xs_ref, wgu_hbm, wd_hbm, bg_ref, bl_ref, bd_ref, y_ref,
                xb_ref, wg_s, wl_s, wd_s, wgu_buf, wd_buf, wsems, *, layer):
    tm = FFN_TILE
    j = pl.program_id(0)
    valid = j < nv_ref[0]
    first = jnp.logical_or(j == 0, te_ref[j] != te_ref[jnp.maximum(j - 1, 0)])
    slot = grp_ref[j] % 2

    def weight_copies(e, s):
        return (pltpu.make_async_copy(wgu_hbm.at[layer, e], wgu_buf.at[s], wsems.at[0, s]),
                pltpu.make_async_copy(wd_hbm.at[layer, e], wd_buf.at[s], wsems.at[1, s]))

    @pl.when(j == 0)
    def _():
        for cp in weight_copies(te_ref[0], 0):
            cp.start()

    @pl.when(jnp.logical_and(valid, first))
    def _():
        for cp in weight_copies(te_ref[j], slot):
            cp.wait()

        @pl.when(nxt_ref[j] >= 0)
        def _():
            for cp in weight_copies(nxt_ref[j], 1 - slot):
                cp.start()

        pi = lax.broadcasted_iota(I32, (256, 256), 0)
        pj = lax.broadcasted_iota(I32, (256, 256), 1)
        src = jnp.where(pj < 128, 2 * pj, 2 * (pj - 128) + 1)
        perm = jnp.where(pi == src, 1.0, 0.0).astype(BF16)
        for blk in range(8):
            r = _dot(wgu_buf[slot, :, blk * 256:(blk + 1) * 256].astype(BF16), perm)
            wg_s[:, blk * 128:(blk + 1) * 128] = r[:, 0:128].astype(BF16)
            wl_s[:, blk * 128:(blk + 1) * 128] = r[:, 128:256].astype(BF16)
        wd_s[...] = wd_buf[slot].astype(BF16)

    @pl.when(valid)
    def _():
        for c in range(SUBLANES):
            xb_ref[:, c * LANES:(c + 1) * LANES] = xs_ref[pl.ds(c, tm, stride=SUBLANES), :].astype(BF16)
        xb = xb_ref[...]
        hg = _dot(xb, wg_s[...]) + bg_ref[...]
        hl = _dot(xb, wl_s[...]) + bl_ref[...]
        glu = jnp.minimum(hg, SWIGLU_LIMIT)
        lin = jnp.clip(hl, -SWIGLU_LIMIT, SWIGLU_LIMIT)
        act = glu * _sigmoid(SWIGLU_ALPHA * glu) * (lin + 1.0)
        y = _dot(act.astype(BF16), wd_s[...]) + bd_ref[...]
        for c in range(SUBLANES):
            y_ref[pl.ds(c, tm, stride=SUBLANES), :] = y[:, c * LANES:(c + 1) * LANES]

    @pl.when(jnp.logical_not(valid))
    def _():
        y_ref[...] = jnp.zeros_like(y_ref)


def _ffn(tile_expert, n_valid, tile_group, next_expert, xs, w_gate_up, w_down, bg, bl, b_down, layer, n_tiles):
    tm = FFN_TILE
    D = w_down.shape[-1]
    tile = lambda j, te, nv, grp, nxt: (jnp.minimum(j, nv[0] - 1), 0)
    out_tile = lambda j, te, nv, grp, nxt: (j, 0)
    bias = lambda j, te, nv, grp, nxt: (layer, te[j], 0, 0)
    return pl.pallas_call(
        functools.partial(_ffn_kernel, layer=layer),
        out_shape=jax.ShapeDtypeStruct(xs.shape, F32),
        grid_spec=pltpu.PrefetchScalarGridSpec(
            num_scalar_prefetch=4,
            grid=(n_tiles,),
            in_specs=[pl.BlockSpec((tm * SUBLANES, LANES), tile),
                      pl.BlockSpec(memory_space=pl.ANY),
                      pl.BlockSpec(memory_space=pl.ANY),
                      pl.BlockSpec((None, None, 1, D), bias),
                      pl.BlockSpec((None, None, 1, D), bias),
                      pl.BlockSpec((None, None, 1, D), bias)],
            out_specs=pl.BlockSpec((tm * SUBLANES, LANES), out_tile),
            scratch_shapes=[pltpu.VMEM((tm, D), BF16),
                            pltpu.VMEM((D, D), BF16),
                            pltpu.VMEM((D, D), BF16),
                            pltpu.VMEM((D, D), BF16),
                            pltpu.VMEM((2, D, 2 * D), F32),
                            pltpu.VMEM((2, D, D), F32),
                            pltpu.SemaphoreType.DMA((2, 2))]),
        compiler_params=_params(("arbitrary",)),
        name="expert_ffn",
    )(tile_expert, n_valid, tile_group, next_expert, xs, w_gate_up, w_down, bg, bl, b_down)


def _combine_kernel(pos_ref, y_ref, gate_ref, x1_ref, g_ref, b_ref, o_ref, ybuf0, ybuf1, zs, sems):
    tc = COMBINE_TILE
    i = pl.program_id(0)
    n = pl.num_programs(0)
    bufs = (ybuf0, ybuf1)
    rows = TOP_K * tc * SUBLANES
    per_chunk = tc // SUBLANES

    def row_copy(tile_base, s, tt, k):
        p = pos_ref[tile_base + tt * TOP_K + k]
        src = y_ref.at[pl.ds(pl.multiple_of(p * SUBLANES, SUBLANES), SUBLANES), :]
        dst = bufs[s].at[pl.ds(pl.multiple_of((k * tc + tt) * SUBLANES, SUBLANES), SUBLANES), :]
        return pltpu.make_async_copy(src, dst, sems.at[s])

    def issue_loop(tile_base, s):
        def tok(tt, carry):
            for k in range(TOP_K):
                row_copy(tile_base, s, tt, k).start(priority=k % 2)
            return carry

        lax.fori_loop(0, tc, tok, 0)

    def slot_wait(s):
        pltpu.make_async_copy(y_ref.at[pl.ds(0, rows), :], bufs[s], sems.at[s]).wait()

    def step(s):
        slot_wait(s)
        next_base = jnp.minimum(i + 1, n - 1) * (tc * TOP_K)
        gate = gate_ref[...]
        s1 = jnp.zeros((tc, 1), F32)
        for c in range(SUBLANES):
            for tt in range(c * per_chunk, (c + 1) * per_chunk):
                for k in range(TOP_K):
                    row_copy(next_base, 1 - s, tt, k).start(priority=k % 2)
            z = ALPHA * x1_ref[:, c * LANES:(c + 1) * LANES]
            for k in range(TOP_K):
                z = z + gate[:, k:k + 1] * bufs[s][pl.ds(k * tc * SUBLANES + c, tc, stride=SUBLANES), :]
            zs[:, c * LANES:(c + 1) * LANES] = z
            s1 = s1 + jnp.sum(z, axis=1, keepdims=True)
        zv = zs[...]
        mu = s1 / zv.shape[1]
        zc = zv - mu
        var = jnp.mean(zc * zc, axis=1, keepdims=True)
        o_ref[...] = zc * lax.rsqrt(var + LN_EPS) * g_ref[...] + b_ref[...]

        @pl.when(i == n - 1)
        def _():
            slot_wait(1 - s)

    @pl.when(i == 0)
    def _():
        issue_loop(0, 0)

    @pl.when(i % 2 == 0)
    def _():
        step(0)

    @pl.when(i % 2 == 1)
    def _():
        step(1)


def _combine(pos_flat, y, gate, x1, ln_g, ln_b, layer):
    T, D = x1.shape
    tc = COMBINE_TILE
    row = lambda i, pos: (i, 0)
    return pl.pallas_call(
        _combine_kernel,
        out_shape=jax.ShapeDtypeStruct((T, D), F32),
        grid_spec=pltpu.PrefetchScalarGridSpec(
            num_scalar_prefetch=1,
            grid=(T // tc,),
            in_specs=[pl.BlockSpec(memory_space=pl.ANY),
                      pl.BlockSpec((tc, LANES), row),
                      pl.BlockSpec((tc, D), row),
                      pl.BlockSpec((None, 1, D), lambda i, pos: (layer, 0, 0)),
                      pl.BlockSpec((None, 1, D), lambda i, pos: (layer, 0, 0))],
            out_specs=pl.BlockSpec((tc, D), row),
            scratch_shapes=[pltpu.VMEM((TOP_K * tc * SUBLANES, LANES), F32),
                            pltpu.VMEM((TOP_K * tc * SUBLANES, LANES), F32),
                            pltpu.VMEM((tc, D), F32),
                            pltpu.SemaphoreType.DMA((2,))]),
        compiler_params=_params(("arbitrary",)),
        name="combine_ln",
    )(pos_flat, y, gate, x1, ln_g.reshape(-1, 1, D), ln_b.reshape(-1, 1, D))


def _moe(x1, x1r, eid, rank, gate, cnt, layer, w_gate_up, bg, bl, w_down, b_down, ln_g, ln_b):
    T, D = x1.shape
    tm = FFN_TILE
    n_tiles = (T * TOP_K) // tm + N_EXPERTS
    n_rows = n_tiles * tm
    counts = cnt[0, :N_EXPERTS].astype(I32)
    tiles_e = (counts + tm - 1) // tm
    tile_end = jnp.cumsum(tiles_e)
    offs = (tile_end - tiles_e) * tm
    n_valid = tile_end[-1]
    jj = jnp.minimum(jnp.arange(n_tiles, dtype=I32), n_valid - 1)
    tile_expert = jnp.minimum(jnp.sum(tile_end[None, :] <= jj[:, None], axis=1), N_EXPERTS - 1).astype(I32)
    off_row = jnp.pad(offs, (0, LANES - N_EXPERTS)).astype(F32).reshape(1, LANES)
    present = tiles_e > 0
    group_of_e = jnp.cumsum(present.astype(I32)) - 1
    eidx = jnp.arange(N_EXPERTS, dtype=I32)
    later = jnp.where(present[None, :] & (eidx[None, :] > eidx[:, None]), eidx[None, :], N_EXPERTS)
    next_e = jnp.min(later, axis=1)
    next_e = jnp.where(next_e == N_EXPERTS, -1, next_e).astype(I32)
    tile_group = group_of_e[tile_expert].astype(I32)
    next_expert = next_e[tile_expert]

    pos = _positions(eid, rank, off_row)
    pos_flat = pos[:, :TOP_K].reshape(-1)
    nv = n_valid.reshape(1).astype(I32)
    xs = _dispatch(pos_flat, counts, offs.astype(I32), nv, x1r, n_rows)
    y = _ffn(tile_expert, nv, tile_group, next_expert, xs, w_gate_up, w_down, bg, bl,
             b_down.reshape(DEPTH, N_EXPERTS, 1, D), layer, n_tiles)
    return _combine(pos_flat, y, gate, x1, ln_g, ln_b, layer)


def kernel(x, even_w_in, pool_w, pool_scale, conv_w, conv_b, i_bias, f_bias, ml_norm, even_w_out, odd_w_in, gla_w2, gla_b, gla_norm, odd_w_out, ln1_g, ln1_b, ln2_g, ln2_b, router_w, router_b, w_gate_up, b_gate_up, w_down, b_down):
    B, S, D = x.shape
    xt = x.reshape(B * S, D)
    bgu = b_gate_up.reshape(DEPTH, N_EXPERTS, D, 2)
    bg = bgu[..., 0].reshape(DEPTH, N_EXPERTS, 1, D)
    bl = bgu[..., 1].reshape(DEPTH, N_EXPERTS, 1, D)
    for layer in range(DEPTH):
        li = layer // 2
        if layer % 2 == 0:
            p = _project(xt, even_w_in, li, 2560)
            mix = _even_mixer(p, xt, B, S, even_w_in, li, pool_w, pool_scale, conv_w, conv_b,
                              i_bias, f_bias, ml_norm)
            w_out = even_w_out
        else:
            p = _project(xt, odd_w_in, li, 3072)
            mix = _odd_mixer(p, xt, B, S, odd_w_in, li, gla_w2, gla_b, gla_norm)
            w_out = odd_w_out
        x1, x1r, eid, rank, gate, cnt = _post(mix, xt, w_out, li, ln1_g, ln1_b, router_w, router_b, layer)
        xt = _moe(x1, x1r, eid, rank, gate, cnt, layer, w_gate_up, bg, bl, w_down, b_down,
                  ln2_g, ln2_b)
    return xt.reshape(B, S, D)
```

```python
import functools

import jax
import jax.numpy as jnp
from jax import lax
from jax.experimental import pallas as pl
from jax.experimental.pallas import tpu as pltpu

F32 = jnp.float32
BF16 = jnp.bfloat16
I32 = jnp.int32

LANES = 128
SUBLANES = 8
VMEM_LIMIT = 56 * 1024 * 1024

DEPTH = 4
ALPHA = (2 * DEPTH) ** 0.25
LN_EPS = 1e-5
NEG = -1e30
LOG2_E = 1.4426950408889634

POOL_WINDOWS = (2, 4, 8, 16)
POOL_CARRY = 16
CONV_K = 4
CONV_CARRY = 8
HEADS = 4
HEAD_DK = 128
GLA_DV = 256
GLA_RANK = 16
GLA_TAU = 16.0
GLA_SUB = 16
N_EXPERTS = 32
TOP_K = 4
SWIGLU_LIMIT = 7.0
SWIGLU_ALPHA = 1.702

SEQ_CHUNK = 128
POST_TILE = 512
POST_SUB = 128
FFN_TILE = 256
COMBINE_TILE = 128
DISPATCH_GROUP = 128


def _dot(a, b):
    return jnp.dot(a, b, preferred_element_type=F32)


def _dot_nt(a, b):
    return lax.dot_general(a, b, (((1,), (1,)), ((), ())), preferred_element_type=F32)


def _split3(a):
    a1 = a.astype(BF16)
    r1 = a - a1.astype(F32)
    a2 = r1.astype(BF16)
    r2 = r1 - a2.astype(F32)
    return a1, a2, r2.astype(BF16)


def _dot01_left(m01, a):
    a1, a2, a3 = _split3(a)
    return _dot(m01, a1) + _dot(m01, a2) + _dot(m01, a3)


def _dot01_right(a, m01):
    a1, a2, a3 = _split3(a)
    return _dot(a1, m01) + _dot(a2, m01) + _dot(a3, m01)


def _log_sigmoid(x):
    return jnp.minimum(x, 0.0) - jnp.log1p(jnp.exp(-jnp.abs(x)))


def _sigmoid(x):
    return 1.0 / (1.0 + jnp.exp(-x))


def _params(sem):
    return pltpu.CompilerParams(dimension_semantics=sem, vmem_limit_bytes=VMEM_LIMIT)


def _proj_kernel(x_ref, w_hbm, o_ref, wb_ref, stage, sems, *, li, n_cols, tn):
    @pl.when(pl.program_id(0) == 0)
    def _():
        n_chunks = n_cols // tn

        def chunk_copy(n):
            return pltpu.make_async_copy(w_hbm.at[li, :, pl.ds(n * tn, tn)], stage.at[n % 2], sems.at[n % 2])

        chunk_copy(0).start()
        for n in range(n_chunks):
            if n + 1 < n_chunks:
                chunk_copy(n + 1).start()
            chunk_copy(n).wait()
            wb_ref[:, n * tn:(n + 1) * tn] = stage[n % 2].astype(BF16)

    o_ref[...] = _dot(x_ref[...].astype(BF16), wb_ref[...])


def _project(x2d, w3d, li, n_cols, tn=512):
    T, D = x2d.shape
    tm = min(512, T)
    return pl.pallas_call(
        functools.partial(_proj_kernel, li=li, n_cols=n_cols, tn=tn),
        out_shape=jax.ShapeDtypeStruct((T, n_cols), F32),
        grid=(T // tm,),
        in_specs=[pl.BlockSpec((tm, D), lambda i: (i, 0)),
                  pl.BlockSpec(memory_space=pl.ANY)],
        out_specs=pl.BlockSpec((tm, n_cols), lambda i: (i, 0)),
        scratch_shapes=[pltpu.VMEM((D, n_cols), BF16),
                        pltpu.VMEM((2, D, tn), F32),
                        pltpu.SemaphoreType.DMA((2,))],
        compiler_params=_params(("arbitrary",)),
        name="in_proj",
    )(x2d, w3d)


def _even_mixer_kernel(p_ref, x_ref, wgc_ref, wgr_ref, gbr_ref, gbc_ref, poolw_ref, pscale_ref,
                       convw_ref, convb_ref, mlnorm_ref, mix_ref,
                       ubuf, qkbuf, c_st, n_st, m_st, tri):
    L = SEQ_CHUNK
    c = pl.program_id(1)

    @pl.when(c == 0)
    def _():
        ri0 = lax.broadcasted_iota(I32, (L, L), 0)
        ci0 = lax.broadcasted_iota(I32, (L, L), 1)
        tri[0] = jnp.where(ci0 <= ri0, 1.0, 0.0).astype(BF16)
        tri[1] = jnp.where(ri0 <= ci0, 1.0, 0.0).astype(BF16)
        ubuf[0:POOL_CARRY, :] = jnp.zeros((POOL_CARRY, 512), F32)
        qkbuf[0:CONV_CARRY, :] = jnp.zeros((CONV_CARRY, 1024), F32)
        c_st[...] = jnp.zeros_like(c_st)
        n_st[...] = jnp.zeros_like(n_st)
        m_st[...] = jnp.zeros_like(m_st)

    ubuf[POOL_CARRY:POOL_CARRY + L, :] = p_ref[:, 0:512]
    pos1 = lax.broadcasted_iota(I32, (L, 1), 0) + c * L + 1
    for g, w in enumerate(POOL_WINDOWS):
        lo = g * 128
        acc = ubuf[pl.ds(POOL_CARRY, L), lo:lo + 128]
        cur = acc
        for j in range(1, w):
            acc = acc + ubuf[pl.ds(POOL_CARRY - j, L), lo:lo + 128]
        inv_cnt = 1.0 / jnp.minimum(pos1, w).astype(F32)
        d = acc * inv_cnt - cur
        yp = _dot(d.astype(BF16), poolw_ref[g].astype(BF16)) * pscale_ref[:, lo:lo + 128]
        mix_ref[:, lo:lo + 128] = yp.astype(BF16)
    ubuf[0:POOL_CARRY, :] = ubuf[L:L + POOL_CARRY, :]

    qkbuf[CONV_CARRY:CONV_CARRY + L, :] = p_ref[:, 512:1536]
    acc = jnp.zeros((L, 1024), F32) + convb_ref[...]
    for j in range(CONV_K):
        acc = acc + convw_ref[j:j + 1, :] * qkbuf[pl.ds(CONV_CARRY - (CONV_K - 1) + j, L), :]
    qk = acc * _sigmoid(acc)
    qkbuf[0:CONV_CARRY, :] = qkbuf[L:L + CONV_CARRY, :]

    xb = x_ref[...].astype(BF16)
    gcol = _dot(xb, wgc_ref[...].astype(BF16)) + gbr_ref[...]
    grow = _dot_nt(wgr_ref[...].astype(BF16), xb) + gbc_ref[:, 0:1]
    ri = lax.broadcasted_iota(I32, (L, L), 0)
    ci = lax.broadcasted_iota(I32, (L, L), 1)
    causal = ci <= ri
    b_col = _dot01_left(tri[0], _log_sigmoid(gcol))
    b_row = _dot01_right(_log_sigmoid(grow), tri[1])

    kscale = HEAD_DK ** -0.5
    H = range(HEADS)
    qh = [qk[:, h * 128:(h + 1) * 128] for h in H]
    kh = [qk[:, 512 + h * 128:512 + (h + 1) * 128] * kscale for h in H]
    qb = [q.astype(BF16) for q in qh]
    kb = [k.astype(BF16) for k in kh]
    vb = [p_ref[:, 1536 + h * 128:1536 + (h + 1) * 128].astype(BF16) for h in H]
    b_c = [b_col[:, 4 + h:5 + h] for h in H]
    b_r = [b_row[4 + h:5 + h, :] for h in H]
    ig_c = [gcol[:, h:h + 1] for h in H]
    ig_r = [grow[h:h + 1, :] for h in H]
    g = [b_r[h][:, L - 1:L] for h in H]
    m_prev = [m_st[h][:, 0:1] for h in H]

    dmat = [jnp.where(causal, b_c[h] - b_r[h] + ig_r[h], NEG) for h in H]
    m_inter = [b_c[h] + m_prev[h] for h in H]
    dmax = [jnp.max(dmat[h], axis=1, keepdims=True) for h in H]
    qkt = [_dot_nt(qb[h], kb[h]) for h in H]
    qc = [_dot(qb[h], c_st[h].astype(BF16)) for h in H]
    qn = [jnp.sum(qh[h] * n_st[h], axis=1, keepdims=True) for h in H]
    m_t = [jnp.maximum(m_inter[h], dmax[h]) for h in H]
    sts = [qkt[h] * jnp.exp(dmat[h] - m_t[h]) for h in H]
    sc = [jnp.exp(m_inter[h] - m_t[h]) for h in H]
    sv = [_dot(sts[h].astype(BF16), vb[h]) for h in H]
    ssum = [jnp.sum(sts[h], axis=1, keepdims=True) for h in H]

    a_c = [g[h] - b_c[h] + ig_c[h] for h in H]
    a_r = [g[h] - b_r[h] + ig_r[h] for h in H]
    m_new = [jnp.maximum(g[h] + m_prev[h], jnp.max(a_r[h], axis=1, keepdims=True)) for h in H]
    decay = [jnp.exp(g[h] + m_prev[h] - m_new[h]) for h in H]
    wk = [jnp.exp(a_c[h] - m_new[h]) * kh[h] for h in H]
    wkt = [wk[h].T.astype(BF16) for h in H]
    kv = [_dot(wkt[h], vb[h]) for h in H]

    num = [sv[h] + sc[h] * qc[h] for h in H]
    den = [ssum[h] + sc[h] * qn[h] for h in H]
    hh = [num[h] * (1.0 / jnp.maximum(jnp.abs(den[h]), jnp.exp(-m_t[h]))) for h in H]
    mu = [jnp.mean(hh[h], axis=1, keepdims=True) for h in H]
    hc = [hh[h] - mu[h] for h in H]
    var = [jnp.mean(hc[h] * hc[h], axis=1, keepdims=True) for h in H]
    for h in H:
        lo = h * 128
        c_st[h] = decay[h] * c_st[h] + kv[h]
        n_st[h] = decay[h] * n_st[h] + jnp.sum(wk[h], axis=0, keepdims=True)
        m_st[h] = jnp.broadcast_to(m_new[h], (1, LANES))
        og = p_ref[:, 2048 + lo:2048 + lo + 128]
        hn = hc[h] * lax.rsqrt(var[h] + LN_EPS) * mlnorm_ref[:, lo:lo + 128]
        mix_ref[:, 512 + lo:512 + lo + 128] = (hn * _sigmoid(og)).astype(BF16)


def _even_mixer(p, x2d, B, S, w_in, li, pool_w, pool_scale, conv_w, conv_b, i_bias, f_bias, ml_norm):
    T, D = x2d.shape
    L = SEQ_CHUNK
    nc = S // L
    wg = w_in[li, :, 2560:2568]
    wgc = jnp.pad(wg, ((0, 0), (0, LANES - 8)))
    wgr = wg.T
    gb = jnp.concatenate([i_bias[li], f_bias[li]])
    gbr = jnp.pad(gb, (0, LANES - 8)).reshape(1, LANES)
    gbc = jnp.broadcast_to(gb.reshape(8, 1), (8, LANES))
    row = lambda b, c: (b * nc + c, 0)
    full2 = lambda b, c: (0, 0)
    return pl.pallas_call(
        _even_mixer_kernel,
        out_shape=jax.ShapeDtypeStruct((T, D), BF16),
        grid=(B, nc),
        in_specs=[pl.BlockSpec((L, 2560), row),
                  pl.BlockSpec((L, D), row),
                  pl.BlockSpec((D, LANES), full2),
                  pl.BlockSpec((8, D), full2),
                  pl.BlockSpec((1, LANES), full2),
                  pl.BlockSpec((8, LANES), full2),
                  pl.BlockSpec((None, 4, 128, 128), lambda b, c: (li, 0, 0, 0)),
                  pl.BlockSpec((None, 1, 512), lambda b, c: (li, 0, 0)),
                  pl.BlockSpec((None, CONV_K, 1024), lambda b, c: (li, 0, 0)),
                  pl.BlockSpec((None, 1, 1024), lambda b, c: (li, 0, 0)),
                  pl.BlockSpec((None, 1, 512), lambda b, c: (li, 0, 0))],
        out_specs=pl.BlockSpec((L, D), row),
        scratch_shapes=[pltpu.VMEM((L + POOL_CARRY, 512), F32),
                        pltpu.VMEM((L + CONV_CARRY, 1024), F32),
                        pltpu.VMEM((HEADS, 128, 128), F32),
                        pltpu.VMEM((HEADS, 1, 128), F32),
                        pltpu.VMEM((HEADS, 1, LANES), F32),
                        pltpu.VMEM((2, L, L), BF16)],
        compiler_params=_params(("arbitrary", "arbitrary")),
        name="even_mixer",
    )(p, x2d, wgc, wgr, gbr, gbc, pool_w, pool_scale.reshape(-1, 1, 512), conv_w,
      conv_b.reshape(-1, 1, 1024), ml_norm.reshape(-1, 1, 512))


def _odd_mixer_kernel(p_ref, x_ref, wr_ref, w2_ref, gb_ref, gnorm_ref, mix_ref, s_st):
    L = SEQ_CHUNK
    c = pl.program_id(1)

    @pl.when(c == 0)
    def _():
        s_st[...] = jnp.zeros_like(s_st)

    xb = x_ref[...].astype(BF16)
    glr = _dot(xb, wr_ref[...].astype(BF16))
    z = _dot(glr.astype(BF16), w2_ref[...].astype(BF16)) + gb_ref[...]
    lg = _log_sigmoid(z) * (1.0 / GLA_TAU)

    ri = lax.broadcasted_iota(I32, (L, L), 0)
    ci = lax.broadcasted_iota(I32, (L, L), 1)
    one = lambda m: jnp.where(m, 1.0, 0.0).astype(BF16)
    causal = ci <= ri
    blk0 = (ri // GLA_SUB) * GLA_SUB
    m_in = one(causal & (ci >= blk0))
    m_all = one(causal)
    lg3 = _split3(lg)
    cum = lambda m01: _dot(m01, lg3[0]) + _dot(m01, lg3[1]) + _dot(m01, lg3[2])
    b_in = cum(m_in)
    b_all = cum(m_all)

    levels = []
    for hs in (64, 32, 16):
        ps = 2 * hs
        pair0 = (ri // ps) * ps
        refrow = pair0 + hs - 1
        upper = (ri - pair0) >= hs
        dq = cum(one(upper & (ci > refrow) & causal))
        dk = cum(one((~upper) & (ci > ri) & (ci <= refrow)))
        rcol = lax.broadcasted_iota(I32, (L, HEAD_DK), 0)
        up_c = ((rcol % ps) >= hs)
        same_pair = (ri // ps) == (ci // ps)
        levels.append((dq, dk, up_c, same_pair, ps))

    rows16 = lax.broadcasted_iota(I32, (GLA_SUB, LANES), 0)
    lane16 = lax.broadcasted_iota(I32, (GLA_SUB, LANES), 1)
    eye = ri == ci
    qscale = HEAD_DK ** -0.5
    H = range(HEADS)
    qh = [p_ref[:, h * 128:(h + 1) * 128] * qscale for h in H]
    kh = [p_ref[:, 512 + h * 128:512 + (h + 1) * 128] for h in H]
    vb = [p_ref[:, 1024 + h * GLA_DV:1024 + (h + 1) * GLA_DV].astype(BF16) for h in H]
    bh = [b_all[:, h * 128:(h + 1) * 128] for h in H]
    bin_h = [b_in[:, h * 128:(h + 1) * 128] for h in H]

    o0 = [_dot((qh[h] * jnp.exp(bh[h])).astype(BF16), s_st[h].astype(BF16)) for h in H]

    amat = [jnp.zeros((L, L), F32) for _ in H]
    for dq, dk, up_c, same_pair, ps in levels:
        qt = [jnp.where(up_c, qh[h] * jnp.exp(dq[:, h * 128:(h + 1) * 128]), 0.0).astype(BF16) for h in H]
        kt = [jnp.where(up_c, 0.0, kh[h] * jnp.exp(dk[:, h * 128:(h + 1) * 128])).astype(BF16) for h in H]
        a_lev = [_dot_nt(qt[h], kt[h]) for h in H]
        amat = [amat[h] + (a_lev[h] if ps == L else jnp.where(same_pair, a_lev[h], 0.0)) for h in H]

    bl = [bh[h][L - 1:L, :] for h in H]
    kdec_t = [(kh[h] * jnp.exp(bl[h] - bh[h])).T.astype(BF16) for h in H]
    ecol = [jnp.sum(jnp.where(eye, jnp.exp(bl[h]), 0.0), axis=1, keepdims=True) for h in H]
    kv = [_dot(kdec_t[h], vb[h]) for h in H]

    blocks = [[] for _ in H]
    for blk in range(L // GLA_SUB):
        r0 = blk * GLA_SUB
        qb16 = [qh[h][r0:r0 + GLA_SUB, :] for h in H]
        kb16 = [kh[h][r0:r0 + GLA_SUB, :] for h in H]
        bb16 = [bin_h[h][r0:r0 + GLA_SUB, :] * LOG2_E for h in H]
        ablk = [jnp.zeros((GLA_SUB, LANES), F32) for _ in H]
        for s in range(GLA_SUB):
            for h in H:
                pr = qb16[h] * kb16[h][s:s + 1, :] * jnp.exp2(bb16[h] - bb16[h][s:s + 1, :])
                a = jnp.sum(pr, axis=1, keepdims=True)
                ablk[h] = jnp.where(lane16 == (r0 + s), a, ablk[h])
        for h in H:
            blocks[h].append(jnp.where(lane16 - r0 <= rows16, ablk[h], 0.0))
    amat = [amat[h] + jnp.concatenate(blocks[h], axis=0) for h in H]

    o = [o0[h] + _dot(amat[h].astype(BF16), vb[h]) for h in H]
    mu = [jnp.mean(o[h], axis=1, keepdims=True) for h in H]
    oc = [o[h] - mu[h] for h in H]
    var = [jnp.mean(oc[h] * oc[h], axis=1, keepdims=True) for h in H]
    for h in H:
        s_st[h] = ecol[h] * s_st[h] + kv[h]
        rg = p_ref[:, 2048 + h * GLA_DV:2048 + (h + 1) * GLA_DV]
        on = oc[h] * lax.rsqrt(var[h] + LN_EPS) * gnorm_ref[:, h * GLA_DV:(h + 1) * GLA_DV]
        mix_ref[:, h * GLA_DV:(h + 1) * GLA_DV] = (on * (rg * _sigmoid(rg))).astype(BF16)


def _odd_mixer(p, x2d, B, S, w_in, li, gla_w2, gla_b, gla_norm):
    T, D = x2d.shape
    L = SEQ_CHUNK
    nc = S // L
    wr = jnp.pad(w_in[li, :, 3072:3072 + GLA_RANK], ((0, 0), (0, LANES - GLA_RANK)))
    w2 = jnp.pad(gla_w2[li], ((0, LANES - GLA_RANK), (0, 0)))
    row = lambda b, c: (b * nc + c, 0)
    full2 = lambda b, c: (0, 0)
    return pl.pallas_call(
        _odd_mixer_kernel,
        out_shape=jax.ShapeDtypeStruct((T, D), BF16),
        grid=(B, nc),
        in_specs=[pl.BlockSpec((L, 3072), row),
                  pl.BlockSpec((L, D), row),
                  pl.BlockSpec((D, LANES), full2),
                  pl.BlockSpec((LANES, 512), full2),
                  pl.BlockSpec((None, 1, 512), lambda b, c: (li, 0, 0)),
                  pl.BlockSpec((None, 1, 1024), lambda b, c: (li, 0, 0))],
        out_specs=pl.BlockSpec((L, D), row),
        scratch_shapes=[pltpu.VMEM((HEADS, HEAD_DK, GLA_DV), F32)],
        compiler_params=_params(("arbitrary", "arbitrary")),
        name="odd_mixer",
    )(p, x2d, wr, w2, gla_b.reshape(-1, 1, 512), gla_norm.reshape(-1, 1, 1024))


def _post_kernel(mix_ref, x_ref, wout_ref, g_ref, b_ref, rw_ref, rb_ref,
                 x1_ref, x1r_ref, eid_ref, rank_ref, gate_ref, cnt_ref,
                 wb_ref, carry_ref):
    tm = mix_ref.shape[0]
    sub = min(POST_SUB, tm)
    i = pl.program_id(0)

    @pl.when(i == 0)
    def _():
        wb_ref[...] = wout_ref[...].astype(BF16)
        carry_ref[...] = jnp.zeros_like(carry_ref)

    rw = rw_ref[...]
    whi = rw.astype(BF16)
    wlo = (rw - whi.astype(F32)).astype(BF16)
    lane = lax.broadcasted_iota(I32, (sub, LANES), 1)
    lane_f = lane.astype(F32)
    ri = lax.broadcasted_iota(I32, (sub, sub), 0)
    ci = lax.broadcasted_iota(I32, (sub, sub), 1)
    strict = jnp.where(ci < ri, 1.0, 0.0).astype(BF16)
    carry = carry_ref[...]

    blocks = [pl.ds(r0, sub) for r0 in range(0, tm, sub)]
    nb = len(blocks)
    x1s = []
    for rows in blocks:
        y = _dot(mix_ref[rows, :], wb_ref[...]) + ALPHA * x_ref[rows, :]
        mu = jnp.mean(y, axis=1, keepdims=True)
        yc = y - mu
        var = jnp.mean(yc * yc, axis=1, keepdims=True)
        x1s.append(yc * lax.rsqrt(var + LN_EPS) * g_ref[...] + b_ref[...])
    for b, rows in enumerate(blocks):
        x1_ref[rows, :] = x1s[b]
        for c in range(SUBLANES):
            x1r_ref[pl.ds(b * sub * SUBLANES + c, sub, stride=SUBLANES), :] = x1s[b][:, c * LANES:(c + 1) * LANES]

    ls = []
    for x1 in x1s:
        hi = x1.astype(BF16)
        lo = (x1 - hi.astype(F32)).astype(BF16)
        ls.append(_dot(hi, whi) + _dot(lo, whi) + _dot(hi, wlo) + rb_ref[...])

    vals = [[] for _ in range(nb)]
    ids = [[] for _ in range(nb)]
    ohs = [[] for _ in range(nb)]
    for _ in range(TOP_K):
        ms = [jnp.max(l, axis=1, keepdims=True) for l in ls]
        idxs = [jnp.min(jnp.where(l == m, lane_f, float(LANES)), axis=1, keepdims=True) for l, m in zip(ls, ms)]
        for b in range(nb):
            oh = lane_f == idxs[b]
            vals[b].append(ms[b])
            ids[b].append(idxs[b])
            ohs[b].append(oh)
            ls[b] = jnp.where(oh, -jnp.inf, ls[b])

    for b, rows in enumerate(blocks):
        es = [jnp.exp(v - vals[b][0]) for v in vals[b]]
        tot = es[0] + es[1] + es[2] + es[3]
        onehot = jnp.zeros((sub, LANES), F32)
        for oh in ohs[b]:
            onehot = onehot + jnp.where(oh, 1.0, 0.0)
        before = _dot(strict, onehot.astype(BF16)) + carry
        carry = carry + jnp.sum(onehot, axis=0, keepdims=True)

        eid = jnp.zeros((sub, LANES), F32)
        rank = jnp.zeros((sub, LANES), F32)
        gate = jnp.zeros((sub, LANES), F32)
        for k in range(TOP_K):
            rk = jnp.sum(jnp.where(ohs[b][k], before, 0.0), axis=1, keepdims=True)
            sel = lane == k
            eid = jnp.where(sel, ids[b][k], eid)
            rank = jnp.where(sel, rk, rank)
            gate = jnp.where(sel, es[k] / tot, gate)
        eid_ref[rows, :] = eid.astype(I32)
        rank_ref[rows, :] = rank.astype(I32)
        gate_ref[rows, :] = gate

    carry_ref[...] = carry
    cnt_ref[...] = carry


def _post(mix, x2d, w_out, li, ln_g, ln_b, router_w, router_b, layer):
    T, D = x2d.shape
    tm = min(POST_TILE, T)
    rw = jnp.pad(router_w[layer], ((0, 0), (0, LANES - N_EXPERTS)))
    rb = jnp.pad(router_b[layer], (0, LANES - N_EXPERTS), constant_values=NEG).reshape(1, LANES)
    row = lambda i: (i, 0)
    full2 = lambda i: (0, 0)
    lane_out = jax.ShapeDtypeStruct((T, LANES), I32)
    return pl.pallas_call(
        _post_kernel,
        out_shape=(jax.ShapeDtypeStruct((T, D), F32),
                   jax.ShapeDtypeStruct((T * SUBLANES, LANES), F32),
                   lane_out, lane_out,
                   jax.ShapeDtypeStruct((T, LANES), F32),
                   jax.ShapeDtypeStruct((1, LANES), F32)),
        grid=(T // tm,),
        in_specs=[pl.BlockSpec((tm, D), row),
                  pl.BlockSpec((tm, D), row),
                  pl.BlockSpec((None, D, D), lambda i: (li, 0, 0)),
                  pl.BlockSpec((None, 1, D), lambda i: (layer, 0, 0)),
                  pl.BlockSpec((None, 1, D), lambda i: (layer, 0, 0)),
                  pl.BlockSpec((D, LANES), full2),
                  pl.BlockSpec((1, LANES), full2)],
        out_specs=(pl.BlockSpec((tm, D), row),
                   pl.BlockSpec((tm * SUBLANES, LANES), row),
                   pl.BlockSpec((tm, LANES), row),
                   pl.BlockSpec((tm, LANES), row),
                   pl.BlockSpec((tm, LANES), row),
                   pl.BlockSpec((1, LANES), full2)),
        scratch_shapes=[pltpu.VMEM((D, D), BF16), pltpu.VMEM((1, LANES), F32)],
        compiler_params=_params(("arbitrary",)),
        name="out_proj_ln_router",
    )(mix, x2d, w_out, ln_g.reshape(-1, 1, D), ln_b.reshape(-1, 1, D), rw, rb)


def _pos_kernel(eid_ref, rank_ref, off_ref, pos_ref):
    tm = eid_ref.shape[0]
    lane = lax.broadcasted_iota(I32, (tm, LANES), 1)
    lane_f = lane.astype(F32)
    eid = eid_ref[...].astype(F32)
    off = off_ref[...]
    pos = jnp.zeros((tm, LANES), F32)
    for k in range(TOP_K):
        ek = jnp.sum(jnp.where(lane == k, eid, 0.0), axis=1, keepdims=True)
        ok = jnp.sum(jnp.where(lane_f == ek, off, 0.0), axis=1, keepdims=True)
        pos = jnp.where(lane == k, ok, pos)
    pos_ref[...] = pos.astype(I32) + rank_ref[...]


def _positions(eid, rank, off_row):
    T = eid.shape[0]
    tm = min(1024, T)
    row = lambda i: (i, 0)
    return pl.pallas_call(
        _pos_kernel,
        out_shape=jax.ShapeDtypeStruct((T, LANES), I32),
        grid=(T // tm,),
        in_specs=[pl.BlockSpec((tm, LANES), row), pl.BlockSpec((tm, LANES), row),
                  pl.BlockSpec((1, LANES), lambda i: (0, 0))],
        out_specs=pl.BlockSpec((tm, LANES), row),
        compiler_params=_params(("arbitrary",)),
        name="route_positions",
    )(eid, rank, off_row)


def _dispatch_kernel(pos_ref, cnt_ref, off_ref, nv_ref, x_ref, xs_ref, stage, zbuf, sems, zsem, *, n_groups):
    G = x_ref.shape[0] // SUBLANES
    gi = pl.program_id(0)
    slot = gi % 2

    def row_copy(tt, k):
        src = stage.at[slot, pl.ds(pl.multiple_of(tt * SUBLANES, SUBLANES), SUBLANES), :]
        p = pos_ref[(gi * G + tt) * TOP_K + k]
        dst = xs_ref.at[pl.ds(pl.multiple_of(p * SUBLANES, SUBLANES), SUBLANES), :]
        return pltpu.make_async_copy(src, dst, sems.at[slot])

    def group_wait(s):
        n = G * TOP_K * SUBLANES
        pltpu.make_async_copy(xs_ref.at[pl.ds(0, n), :], xs_ref.at[pl.ds(0, n), :], sems.at[s]).wait()

    @pl.when(gi >= 2)
    def _():
        group_wait(slot)

    stage[slot] = x_ref[...]

    def tok(tt, carry):
        for k in range(TOP_K):
            row_copy(tt, k).start(priority=k % 2)
        return carry

    lax.fori_loop(0, G, tok, 0)

    @pl.when(gi == n_groups - 1)
    def _():
        group_wait(slot)
        if n_groups >= 2:
            group_wait(1 - slot)

        zbuf[...] = jnp.zeros_like(zbuf)

        def pad_expert(e, carry):
            cnt = cnt_ref[e]
            npad = (FFN_TILE - cnt % FFN_TILE) % FFN_TILE
            base = off_ref[e] + cnt

            def chunk(size):
                first = base + npad - (npad & (2 * size - 1))
                dst = xs_ref.at[pl.ds(pl.multiple_of(first * SUBLANES, SUBLANES), size * SUBLANES), :]
                return pltpu.make_async_copy(zbuf.at[pl.ds(0, size * SUBLANES), :], dst, zsem)

            sizes = [FFN_TILE >> (b + 1) for b in range(FFN_TILE.bit_length() - 1)]
            for size in sizes:
                @pl.when((npad & size) != 0)
                def _():
                    chunk(size).start()
            for size in sizes:
                @pl.when((npad & size) != 0)
                def _():
                    chunk(size).wait()
            return carry

        lax.fori_loop(0, N_EXPERTS, pad_expert, 0)

        tile_rows = FFN_TILE * SUBLANES
        n_tiles = xs_ref.shape[0] // tile_rows

        def tcopy(j):
            dst = xs_ref.at[pl.ds(pl.multiple_of(j * tile_rows, tile_rows), tile_rows), :]
            return pltpu.make_async_copy(zbuf, dst, zsem)

        def tstart(j, c2):
            tcopy(j).start()
            return c2

        def twait(j, c2):
            tcopy(j).wait()
            return c2

        lax.fori_loop(nv_ref[0], n_tiles, tstart, 0)
        lax.fori_loop(nv_ref[0], n_tiles, twait, 0)


def _dispatch(pos_flat, counts, offs, n_valid, x1r, n_rows):
    T = x1r.shape[0] // SUBLANES
    G = min(DISPATCH_GROUP, T)
    return pl.pallas_call(
        functools.partial(_dispatch_kernel, n_groups=T // G),
        out_shape=jax.ShapeDtypeStruct((n_rows * SUBLANES, LANES), F32),
        grid_spec=pltpu.PrefetchScalarGridSpec(
            num_scalar_prefetch=4,
            grid=(T // G,),
            in_specs=[pl.BlockSpec((G * SUBLANES, LANES), lambda i, *_: (i, 0))],
            out_specs=pl.BlockSpec(memory_space=pl.ANY),
            scratch_shapes=[pltpu.VMEM((2, G * SUBLANES, LANES), F32),
                            pltpu.VMEM((FFN_TILE * SUBLANES, LANES), F32),
                            pltpu.SemaphoreType.DMA((2,)),
                            pltpu.SemaphoreType.DMA(())]),
        compiler_params=_params(("arbitrary",)),
        name="dispatch_rows",
    )(pos_flat, counts, offs, n_valid, x1r)


def _ffn_kernel(te_ref, nv_ref, grp_ref, nxt_ref, xs_ref, wgu_hbm, wd_hbm, bg_ref, bl_ref, bd_ref, y_ref,
                xb_ref, wg_s, wl_s, wd_s, wgu_buf, wd_buf, wsems, *, layer):
    tm = FFN_TILE
    j = pl.program_id(0)
    valid = j < nv_ref[0]
    first = jnp.logical_or(j == 0, te_ref[j] != te_ref[jnp.maximum(j - 1, 0)])
    slot = grp_ref[j] % 2

    def weight_copies(e, s):
        return (pltpu.make_async_copy(wgu_hbm.at[layer, e], wgu_buf.at[s], wsems.at[0, s]),
                pltpu.make_async_copy(wd_hbm.at[layer, e], wd_buf.at[s], wsems.at[1, s]))

    @pl.when(j == 0)
    def _():
        for cp in weight_copies(te_ref[0], 0):
            cp.start()

    @pl.when(jnp.logical_and(valid, first))
    def _():
        for cp in weight_copies(te_ref[j], slot):
            cp.wait()

        @pl.when(nxt_ref[j] >= 0)
        def _():
            for cp in weight_copies(nxt_ref[j], 1 - slot):
                cp.start()

        pi = lax.broadcasted_iota(I32, (256, 256), 0)
        pj = lax.broadcasted_iota(I32, (256, 256), 1)
        src = jnp.where(pj < 128, 2 * pj, 2 * (pj - 128) + 1)
        perm = jnp.where(pi == src, 1.0, 0.0).astype(BF16)
        for blk in range(8):
            r = _dot(wgu_buf[slot, :, blk * 256:(blk + 1) * 256].astype(BF16), perm)
            wg_s[:, blk * 128:(blk + 1) * 128] = r[:, 0:128].astype(BF16)
            wl_s[:, blk * 128:(blk + 1) * 128] = r[:, 128:256].astype(BF16)
        wd_s[...] = wd_buf[slot].astype(BF16)

    @pl.when(valid)
    def _():
        for c in range(SUBLANES):
            xb_ref[:, c * LANES:(c + 1) * LANES] = xs_ref[pl.ds(c, tm, stride=SUBLANES), :].astype(BF16)
        xb = xb_ref[...]
        hg = _dot(xb, wg_s[...]) + bg_ref[...]
        hl = _dot(xb, wl_s[...]) + bl_ref[...]
        glu = jnp.minimum(hg, SWIGLU_LIMIT)
        lin = jnp.clip(hl, -SWIGLU_LIMIT, SWIGLU_LIMIT)
        act = glu * _sigmoid(SWIGLU_ALPHA * glu) * (lin + 1.0)
        y = _dot(act.astype(BF16), wd_s[...]) + bd_ref[...]
        for c in range(SUBLANES):
            y_ref[pl.ds(c, tm, stride=SUBLANES), :] = y[:, c * LANES:(c + 1) * LANES]

    @pl.when(jnp.logical_not(valid))
    def _():
        y_ref[...] = jnp.zeros_like(y_ref)


def _ffn(tile_expert, n_valid, tile_group, next_expert, xs, w_gate_up, w_down, bg, bl, b_down, layer, n_tiles):
    tm = FFN_TILE
    D = w_down.shape[-1]
    tile = lambda j, te, nv, grp, nxt: (jnp.minimum(j, nv[0] - 1), 0)
    out_tile = lambda j, te, nv, grp, nxt: (j, 0)
    bias = lambda j, te, nv, grp, nxt: (layer, te[j], 0, 0)
    return pl.pallas_call(
        functools.partial(_ffn_kernel, layer=layer),
        out_shape=jax.ShapeDtypeStruct(xs.shape, F32),
        grid_spec=pltpu.PrefetchScalarGridSpec(
            num_scalar_prefetch=4,
            grid=(n_tiles,),
            in_specs=[pl.BlockSpec((tm * SUBLANES, LANES), tile),
                      pl.BlockSpec(memory_space=pl.ANY),
                      pl.BlockSpec(memory_space=pl.ANY),
                      pl.BlockSpec((None, None, 1, D), bias),
                      pl.BlockSpec((None, None, 1, D), bias),
                      pl.BlockSpec((None, None, 1, D), bias)],
            out_specs=pl.BlockSpec((tm * SUBLANES, LANES), out_tile),
            scratch_shapes=[pltpu.VMEM((tm, D), BF16),
                            pltpu.VMEM((D, D), BF16),
                            pltpu.VMEM((D, D), BF16),
                            pltpu.VMEM((D, D), BF16),
                            pltpu.VMEM((2, D, 2 * D), F32),
                            pltpu.VMEM((2, D, D), F32),
                            pltpu.SemaphoreType.DMA((2, 2))]),
        compiler_params=_params(("arbitrary",)),
        name="expert_ffn",
    )(tile_expert, n_valid, tile_group, next_expert, xs, w_gate_up, w_down, bg, bl, b_down)


def _combine_kernel(pos_ref, y_ref, gate_ref, x1_ref, g_ref, b_ref, o_ref, ybuf0, ybuf1, zs, sems):
    tc = COMBINE_TILE
    i = pl.program_id(0)
    n = pl.num_programs(0)
    bufs = (ybuf0, ybuf1)
    rows = TOP_K * tc * SUBLANES
    per_chunk = tc // SUBLANES

    def row_copy(tile_base, s, tt, k):
        p = pos_ref[tile_base + tt * TOP_K + k]
        src = y_ref.at[pl.ds(pl.multiple_of(p * SUBLANES, SUBLANES), SUBLANES), :]
        dst = bufs[s].at[pl.ds(pl.multiple_of((k * tc + tt) * SUBLANES, SUBLANES), SUBLANES), :]
        return pltpu.make_async_copy(src, dst, sems.at[s])

    def issue_loop(tile_base, s):
        def tok(tt, carry):
            for k in range(TOP_K):
                row_copy(tile_base, s, tt, k).start(priority=k % 2)
            return carry

        lax.fori_loop(0, tc, tok, 0)

    def slot_wait(s):
        pltpu.make_async_copy(y_ref.at[pl.ds(0, rows), :], bufs[s], sems.at[s]).wait()

    def step(s):
        slot_wait(s)
        next_base = jnp.minimum(i + 1, n - 1) * (tc * TOP_K)
        gate = gate_ref[...]
        s1 = jnp.zeros((tc, 1), F32)
        for c in range(SUBLANES):
            for tt in range(c * per_chunk, (c + 1) * per_chunk):
                for k in range(TOP_K):
                    row_copy(next_base, 1 - s, tt, k).start(priority=k % 2)
            z = ALPHA * x1_ref[:, c * LANES:(c + 1) * LANES]
            for k in range(TOP_K):
                z = z + gate[:, k:k + 1] * bufs[s][pl.ds(k * tc * SUBLANES + c, tc, stride=SUBLANES), :]
            zs[:, c * LANES:(c + 1) * LANES] = z
            s1 = s1 + jnp.sum(z, axis=1, keepdims=True)
        zv = zs[...]
        mu = s1 / zv.shape[1]
        zc = zv - mu
        var = jnp.mean(zc * zc, axis=1, keepdims=True)
        o_ref[...] = zc * lax.rsqrt(var + LN_EPS) * g_ref[...] + b_ref[...]

        @pl.when(i == n - 1)
        def _():
            slot_wait(1 - s)

    @pl.when(i == 0)
    def _():
        issue_loop(0, 0)

    @pl.when(i % 2 == 0)
    def _():
        step(0)

    @pl.when(i % 2 == 1)
    def _():
        step(1)


def _combine(pos_flat, y, gate, x1, ln_g, ln_b, layer):
    T, D = x1.shape
    tc = COMBINE_TILE
    row = lambda i, pos: (i, 0)
    return pl.pallas_call(
        _combine_kernel,
        out_shape=jax.ShapeDtypeStruct((T, D), F32),
        grid_spec=pltpu.PrefetchScalarGridSpec(
            num_scalar_prefetch=1,
            grid=(T // tc,),
            in_specs=[pl.BlockSpec(memory_space=pl.ANY),
                      pl.BlockSpec((tc, LANES), row),
                      pl.BlockSpec((tc, D), row),
                      pl.BlockSpec((None, 1, D), lambda i, pos: (layer, 0, 0)),
                      pl.BlockSpec((None, 1, D), lambda i, pos: (layer, 0, 0))],
            out_specs=pl.BlockSpec((tc, D), row),
            scratch_shapes=[pltpu.VMEM((TOP_K * tc * SUBLANES, LANES), F32),
                            pltpu.VMEM((TOP_K * tc * SUBLANES, LANES), F32),
                            pltpu.VMEM((tc, D), F32),
                            pltpu.SemaphoreType.DMA((2,))]),
        compiler_params=_params(("arbitrary",)),
        name="combine_ln",
    )(pos_flat, y, gate, x1, ln_g.reshape(-1, 1, D), ln_b.reshape(-1, 1, D))


def _moe(x1, x1r, eid, rank, gate, cnt, layer, w_gate_up, bg, bl, w_down, b_down, ln_g, ln_b):
    T, D = x1.shape
    tm = FFN_TILE
    n_tiles = (T * TOP_K) // tm + N_EXPERTS
    n_rows = n_tiles * tm
    counts = cnt[0, :N_EXPERTS].astype(I32)
    tiles_e = (counts + tm - 1) // tm
    tile_end = jnp.cumsum(tiles_e)
    offs = (tile_end - tiles_e) * tm
    n_valid = tile_end[-1]
    jj = jnp.minimum(jnp.arange(n_tiles, dtype=I32), n_valid - 1)
    tile_expert = jnp.minimum(jnp.sum(tile_end[None, :] <= jj[:, None], axis=1), N_EXPERTS - 1).astype(I32)
    off_row = jnp.pad(offs, (0, LANES - N_EXPERTS)).astype(F32).reshape(1, LANES)
    present = tiles_e > 0
    group_of_e = jnp.cumsum(present.astype(I32)) - 1
    eidx = jnp.arange(N_EXPERTS, dtype=I32)
    later = jnp.where(present[None, :] & (eidx[None, :] > eidx[:, None]), eidx[None, :], N_EXPERTS)
    next_e = jnp.min(later, axis=1)
    next_e = jnp.where(next_e == N_EXPERTS, -1, next_e).astype(I32)
    mine = tile_expert[:, None] == eidx[None, :]
    tile_group = jnp.sum(jnp.where(mine, group_of_e[None, :], 0), axis=1).astype(I32)
    next_expert = jnp.sum(jnp.where(mine, next_e[None, :], 0), axis=1).astype(I32)

    pos = _positions(eid, rank, off_row)
    pos_flat = pos[:, :TOP_K].reshape(-1)
    nv = n_valid.reshape(1).astype(I32)
    xs = _dispatch(pos_flat, counts, offs.astype(I32), nv, x1r, n_rows)
    y = _ffn(tile_expert, nv, tile_group, next_expert, xs, w_gate_up, w_down, bg, bl,
             b_down.reshape(DEPTH, N_EXPERTS, 1, D), layer, n_tiles)
    return _combine(pos_flat, y, gate, x1, ln_g, ln_b, layer)


def kernel(x, even_w_in, pool_w, pool_scale, conv_w, conv_b, i_bias, f_bias, ml_norm, even_w_out, odd_w_in, gla_w2, gla_b, gla_norm, odd_w_out, ln1_g, ln1_b, ln2_g, ln2_b, router_w, router_b, w_gate_up, b_gate_up, w_down, b_down):
    B, S, D = x.shape
    xt = x.reshape(B * S, D)
    bgu = b_gate_up.reshape(DEPTH, N_EXPERTS, D, 2)
    bg = bgu[..., 0].reshape(DEPTH, N_EXPERTS, 1, D)
    bl = bgu[..., 1].reshape(DEPTH, N_EXPERTS, 1, D)
    for layer in range(DEPTH):
        li = layer // 2
        if layer % 2 == 0:
            p = _project(xt, even_w_in, li, 2560)
            mix = _even_mixer(p, xt, B, S, even_w_in, li, pool_w, pool_scale, conv_w, conv_b,
                              i_bias, f_bias, ml_norm)
            w_out = even_w_out
        else:
            p = _project(xt, odd_w_in, li, 3072)
            mix = _odd_mixer(p, xt, B, S, odd_w_in, li, gla_w2, gla_b, gla_norm)
            w_out = odd_w_out
        x1, x1r, eid, rank, gate, cnt = _post(mix, xt, w_out, li, ln1_g, ln1_b, router_w, router_b, layer)
        xt = _moe(x1, x1r, eid, rank, gate, cnt, layer, w_gate_up, bg, bl, w_down, b_down,
                  ln2_g, ln2_b)
    return xt.reshape(B, S, D)
```

```python
import functools

import jax
import jax.numpy as jnp
from jax import lax
from jax.experimental import pallas as pl
from jax.experimental.pallas import tpu as pltpu

F32 = jnp.float32
BF16 = jnp.bfloat16
I32 = jnp.int32

LANES = 128
SUBLANES = 8
VMEM_LIMIT = 56 * 1024 * 1024

DEPTH = 4
ALPHA = (2 * DEPTH) ** 0.25
LN_EPS = 1e-5
NEG = -1e30
LOG2_E = 1.4426950408889634

POOL_WINDOWS = (2, 4, 8, 16)
POOL_CARRY = 16
CONV_K = 4
CONV_CARRY = 8
HEADS = 4
HEAD_DK = 128
GLA_DV = 256
GLA_RANK = 16
GLA_TAU = 16.0
GLA_SUB = 16
N_EXPERTS = 32
TOP_K = 4
SWIGLU_LIMIT = 7.0
SWIGLU_ALPHA = 1.702

SEQ_CHUNK = 128
POST_TILE = 512
POST_SUB = 128
FFN_TILE = 256
COMBINE_TILE = 128
DISPATCH_GROUP = 128


def _dot(a, b):
    return jnp.dot(a, b, preferred_element_type=F32)


def _dot_nt(a, b):
    return lax.dot_general(a, b, (((1,), (1,)), ((), ())), preferred_element_type=F32)


def _split3(a):
    a1 = a.astype(BF16)
    r1 = a - a1.astype(F32)
    a2 = r1.astype(BF16)
    r2 = r1 - a2.astype(F32)
    return a1, a2, r2.astype(BF16)


def _dot01_left(m01, a):
    a1, a2, a3 = _split3(a)
    return _dot(m01, a1) + _dot(m01, a2) + _dot(m01, a3)


def _dot01_right(a, m01):
    a1, a2, a3 = _split3(a)
    return _dot(a1, m01) + _dot(a2, m01) + _dot(a3, m01)


def _log_sigmoid(x):
    return jnp.minimum(x, 0.0) - jnp.log1p(jnp.exp(-jnp.abs(x)))


def _sigmoid(x):
    return 1.0 / (1.0 + jnp.exp(-x))


def _params(sem):
    return pltpu.CompilerParams(dimension_semantics=sem, vmem_limit_bytes=VMEM_LIMIT)


def _proj_kernel(x_ref, w_hbm, o_ref, wb_ref, stage, sems, *, li, n_cols, tn):
    @pl.when(pl.program_id(0) == 0)
    def _():
        n_chunks = n_cols // tn

        def chunk_copy(n):
            return pltpu.make_async_copy(w_hbm.at[li, :, pl.ds(n * tn, tn)], stage.at[n % 2], sems.at[n % 2])

        chunk_copy(0).start()
        for n in range(n_chunks):
            if n + 1 < n_chunks:
                chunk_copy(n + 1).start()
            chunk_copy(n).wait()
            wb_ref[:, n * tn:(n + 1) * tn] = stage[n % 2].astype(BF16)

    o_ref[...] = _dot(x_ref[...].astype(BF16), wb_ref[...])


def _project(x2d, w3d, li, n_cols, tn=512):
    T, D = x2d.shape
    tm = min(512, T)
    return pl.pallas_call(
        functools.partial(_proj_kernel, li=li, n_cols=n_cols, tn=tn),
        out_shape=jax.ShapeDtypeStruct((T, n_cols), F32),
        grid=(T // tm,),
        in_specs=[pl.BlockSpec((tm, D), lambda i: (i, 0)),
                  pl.BlockSpec(memory_space=pl.ANY)],
        out_specs=pl.BlockSpec((tm, n_cols), lambda i: (i, 0)),
        scratch_shapes=[pltpu.VMEM((D, n_cols), BF16),
                        pltpu.VMEM((2, D, tn), F32),
                        pltpu.SemaphoreType.DMA((2,))],
        compiler_params=_params(("arbitrary",)),
        name="in_proj",
    )(x2d, w3d)


def _even_mixer_kernel(p_ref, x_ref, wgc_ref, wgr_ref, gbr_ref, gbc_ref, poolw_ref, pscale_ref,
                       convw_ref, convb_ref, mlnorm_ref, mix_ref,
                       ubuf, qkbuf, c_st, n_st, m_st, tri):
    L = SEQ_CHUNK
    c = pl.program_id(1)

    @pl.when(c == 0)
    def _():
        ri0 = lax.broadcasted_iota(I32, (L, L), 0)
        ci0 = lax.broadcasted_iota(I32, (L, L), 1)
        tri[0] = jnp.where(ci0 <= ri0, 1.0, 0.0).astype(BF16)
        tri[1] = jnp.where(ri0 <= ci0, 1.0, 0.0).astype(BF16)
        ubuf[0:POOL_CARRY, :] = jnp.zeros((POOL_CARRY, 512), F32)
        qkbuf[0:CONV_CARRY, :] = jnp.zeros((CONV_CARRY, 1024), F32)
        c_st[...] = jnp.zeros_like(c_st)
        n_st[...] = jnp.zeros_like(n_st)
        m_st[...] = jnp.zeros_like(m_st)

    ubuf[POOL_CARRY:POOL_CARRY + L, :] = p_ref[:, 0:512]
    pos1 = lax.broadcasted_iota(I32, (L, 1), 0) + c * L + 1
    for g, w in enumerate(POOL_WINDOWS):
        lo = g * 128
        acc = ubuf[pl.ds(POOL_CARRY, L), lo:lo + 128]
        cur = acc
        for j in range(1, w):
            acc = acc + ubuf[pl.ds(POOL_CARRY - j, L), lo:lo + 128]
        inv_cnt = 1.0 / jnp.minimum(pos1, w).astype(F32)
        d = acc * inv_cnt - cur
        yp = _dot(d.astype(BF16), poolw_ref[g].astype(BF16)) * pscale_ref[:, lo:lo + 128]
        mix_ref[:, lo:lo + 128] = yp.astype(BF16)
    ubuf[0:POOL_CARRY, :] = ubuf[L:L + POOL_CARRY, :]

    qkbuf[CONV_CARRY:CONV_CARRY + L, :] = p_ref[:, 512:1536]
    acc = jnp.zeros((L, 1024), F32) + convb_ref[...]
    for j in range(CONV_K):
        acc = acc + convw_ref[j:j + 1, :] * qkbuf[pl.ds(CONV_CARRY - (CONV_K - 1) + j, L), :]
    qk = acc * _sigmoid(acc)
    qkbuf[0:CONV_CARRY, :] = qkbuf[L:L + CONV_CARRY, :]

    xb = x_ref[...].astype(BF16)
    gcol = _dot(xb, wgc_ref[...].astype(BF16)) + gbr_ref[...]
    grow = _dot_nt(wgr_ref[...].astype(BF16), xb) + gbc_ref[:, 0:1]
    ri = lax.broadcasted_iota(I32, (L, L), 0)
    ci = lax.broadcasted_iota(I32, (L, L), 1)
    causal = ci <= ri
    b_col = _dot01_left(tri[0], _log_sigmoid(gcol))
    b_row = _dot01_right(_log_sigmoid(grow), tri[1])

    kscale = HEAD_DK ** -0.5
    H = range(HEADS)
    qh = [qk[:, h * 128:(h + 1) * 128] for h in H]
    kh = [qk[:, 512 + h * 128:512 + (h + 1) * 128] * kscale for h in H]
    qb = [q.astype(BF16) for q in qh]
    kb = [k.astype(BF16) for k in kh]
    vb = [p_ref[:, 1536 + h * 128:1536 + (h + 1) * 128].astype(BF16) for h in H]
    b_c = [b_col[:, 4 + h:5 + h] for h in H]
    b_r = [b_row[4 + h:5 + h, :] for h in H]
    ig_c = [gcol[:, h:h + 1] for h in H]
    ig_r = [grow[h:h + 1, :] for h in H]
    g = [b_r[h][:, L - 1:L] for h in H]
    m_prev = [m_st[h][:, 0:1] for h in H]

    dmat = [jnp.where(causal, b_c[h] - b_r[h] + ig_r[h], NEG) for h in H]
    m_inter = [b_c[h] + m_prev[h] for h in H]
    dmax = [jnp.max(dmat[h], axis=1, keepdims=True) for h in H]
    qkt = [_dot_nt(qb[h], kb[h]) for h in H]
    qc = [_dot(qb[h], c_st[h].astype(BF16)) for h in H]
    qn = [jnp.sum(qh[h] * n_st[h], axis=1, keepdims=True) for h in H]
    m_t = [jnp.maximum(m_inter[h], dmax[h]) for h in H]
    sts = [qkt[h] * jnp.exp(dmat[h] - m_t[h]) for h in H]
    sc = [jnp.exp(m_inter[h] - m_t[h]) for h in H]
    sv = [_dot(sts[h].astype(BF16), vb[h]) for h in H]
    ssum = [jnp.sum(sts[h], axis=1, keepdims=True) for h in H]

    a_c = [g[h] - b_c[h] + ig_c[h] for h in H]
    a_r = [g[h] - b_r[h] + ig_r[h] for h in H]
    m_new = [jnp.maximum(g[h] + m_prev[h], jnp.max(a_r[h], axis=1, keepdims=True)) for h in H]
    decay = [jnp.exp(g[h] + m_prev[h] - m_new[h]) for h in H]
    wk = [jnp.exp(a_c[h] - m_new[h]) * kh[h] for h in H]
    wkt = [wk[h].T.astype(BF16) for h in H]
    kv = [_dot(wkt[h], vb[h]) for h in H]

    num = [sv[h] + sc[h] * qc[h] for h in H]
    den = [ssum[h] + sc[h] * qn[h] for h in H]
    hh = [num[h] * (1.0 / jnp.maximum(jnp.abs(den[h]), jnp.exp(-m_t[h]))) for h in H]
    mu = [jnp.mean(hh[h], axis=1, keepdims=True) for h in H]
    hc = [hh[h] - mu[h] for h in H]
    var = [jnp.mean(hc[h] * hc[h], axis=1, keepdims=True) for h in H]
    for h in H:
        lo = h * 128
        c_st[h] = decay[h] * c_st[h] + kv[h]
        n_st[h] = decay[h] * n_st[h] + jnp.sum(wk[h], axis=0, keepdims=True)
        m_st[h] = jnp.broadcast_to(m_new[h], (1, LANES))
        og = p_ref[:, 2048 + lo:2048 + lo + 128]
        hn = hc[h] * lax.rsqrt(var[h] + LN_EPS) * mlnorm_ref[:, lo:lo + 128]
        mix_ref[:, 512 + lo:512 + lo + 128] = (hn * _sigmoid(og)).astype(BF16)


def _even_mixer(p, x2d, B, S, w_in, li, pool_w, pool_scale, conv_w, conv_b, i_bias, f_bias, ml_norm):
    T, D = x2d.shape
    L = SEQ_CHUNK
    nc = S // L
    wg = w_in[li, :, 2560:2568]
    wgc = jnp.pad(wg, ((0, 0), (0, LANES - 8)))
    wgr = wg.T
    gb = jnp.concatenate([i_bias[li], f_bias[li]])
    gbr = jnp.pad(gb, (0, LANES - 8)).reshape(1, LANES)
    gbc = jnp.broadcast_to(gb.reshape(8, 1), (8, LANES))
    row = lambda b, c: (b * nc + c, 0)
    full2 = lambda b, c: (0, 0)
    return pl.pallas_call(
        _even_mixer_kernel,
        out_shape=jax.ShapeDtypeStruct((T, D), BF16),
        grid=(B, nc),
        in_specs=[pl.BlockSpec((L, 2560), row),
                  pl.BlockSpec((L, D), row),
                  pl.BlockSpec((D, LANES), full2),
                  pl.BlockSpec((8, D), full2),
                  pl.BlockSpec((1, LANES), full2),
                  pl.BlockSpec((8, LANES), full2),
                  pl.BlockSpec((None, 4, 128, 128), lambda b, c: (li, 0, 0, 0)),
                  pl.BlockSpec((None, 1, 512), lambda b, c: (li, 0, 0)),
                  pl.BlockSpec((None, CONV_K, 1024), lambda b, c: (li, 0, 0)),
                  pl.BlockSpec((None, 1, 1024), lambda b, c: (li, 0, 0)),
                  pl.BlockSpec((None, 1, 512), lambda b, c: (li, 0, 0))],
        out_specs=pl.BlockSpec((L, D), row),
        scratch_shapes=[pltpu.VMEM((L + POOL_CARRY, 512), F32),
                        pltpu.VMEM((L + CONV_CARRY, 1024), F32),
                        pltpu.VMEM((HEADS, 128, 128), F32),
                        pltpu.VMEM((HEADS, 1, 128), F32),
                        pltpu.VMEM((HEADS, 1, LANES), F32),
                        pltpu.VMEM((2, L, L), BF16)],
        compiler_params=_params(("arbitrary", "arbitrary")),
        name="even_mixer",
    )(p, x2d, wgc, wgr, gbr, gbc, pool_w, pool_scale.reshape(-1, 1, 512), conv_w,
      conv_b.reshape(-1, 1, 1024), ml_norm.reshape(-1, 1, 512))


def _odd_mixer_kernel(p_ref, x_ref, wr_ref, w2_ref, gb_ref, gnorm_ref, mix_ref, s_st):
    L = SEQ_CHUNK
    c = pl.program_id(1)

    @pl.when(c == 0)
    def _():
        s_st[...] = jnp.zeros_like(s_st)

    xb = x_ref[...].astype(BF16)
    glr = _dot(xb, wr_ref[...].astype(BF16))
    z = _dot(glr.astype(BF16), w2_ref[...].astype(BF16)) + gb_ref[...]
    lg = _log_sigmoid(z) * (1.0 / GLA_TAU)

    ri = lax.broadcasted_iota(I32, (L, L), 0)
    ci = lax.broadcasted_iota(I32, (L, L), 1)
    one = lambda m: jnp.where(m, 1.0, 0.0).astype(BF16)
    causal = ci <= ri
    blk0 = (ri // GLA_SUB) * GLA_SUB
    m_in = one(causal & (ci >= blk0))
    m_all = one(causal)
    lg3 = _split3(lg)
    cum = lambda m01: _dot(m01, lg3[0]) + _dot(m01, lg3[1]) + _dot(m01, lg3[2])
    b_in = cum(m_in)
    b_all = cum(m_all)

    levels = []
    for hs in (64, 32, 16):
        ps = 2 * hs
        pair0 = (ri // ps) * ps
        refrow = pair0 + hs - 1
        upper = (ri - pair0) >= hs
        dq = cum(one(upper & (ci > refrow) & causal))
        dk = cum(one((~upper) & (ci > ri) & (ci <= refrow)))
        rcol = lax.broadcasted_iota(I32, (L, HEAD_DK), 0)
        up_c = ((rcol % ps) >= hs)
        same_pair = (ri // ps) == (ci // ps)
        levels.append((dq, dk, up_c, same_pair, ps))

    rows16 = lax.broadcasted_iota(I32, (GLA_SUB, LANES), 0)
    lane16 = lax.broadcasted_iota(I32, (GLA_SUB, LANES), 1)
    lane8 = lax.broadcasted_iota(I32, (SUBLANES, LANES), 1)
    eye = ri == ci
    qscale = HEAD_DK ** -0.5
    H = range(HEADS)
    qh = [p_ref[:, h * 128:(h + 1) * 128] * qscale for h in H]
    kh = [p_ref[:, 512 + h * 128:512 + (h + 1) * 128] for h in H]
    vb = [p_ref[:, 1024 + h * GLA_DV:1024 + (h + 1) * GLA_DV].astype(BF16) for h in H]
    bh = [b_all[:, h * 128:(h + 1) * 128] for h in H]
    bin_h = [b_in[:, h * 128:(h + 1) * 128] for h in H]

    o0 = [_dot((qh[h] * jnp.exp(bh[h])).astype(BF16), s_st[h].astype(BF16)) for h in H]

    amat = [jnp.zeros((L, L), F32) for _ in H]
    for dq, dk, up_c, same_pair, ps in levels:
        qt = [jnp.where(up_c, qh[h] * jnp.exp(dq[:, h * 128:(h + 1) * 128]), 0.0).astype(BF16) for h in H]
        kt = [jnp.where(up_c, 0.0, kh[h] * jnp.exp(dk[:, h * 128:(h + 1) * 128])).astype(BF16) for h in H]
        a_lev = [_dot_nt(qt[h], kt[h]) for h in H]
        amat = [amat[h] + (a_lev[h] if ps == L else jnp.where(same_pair, a_lev[h], 0.0)) for h in H]

    bl = [bh[h][L - 1:L, :] for h in H]
    kdec_t = [(kh[h] * jnp.exp(bl[h] - bh[h])).T.astype(BF16) for h in H]
    ecol = [jnp.sum(jnp.where(eye, jnp.exp(bl[h]), 0.0), axis=1, keepdims=True) for h in H]
    kv = [_dot(kdec_t[h], vb[h]) for h in H]

    blocks = [[] for _ in H]
    for blk in range(L // GLA_SUB):
        r0 = blk * GLA_SUB
        qb16 = [qh[h][r0:r0 + GLA_SUB, :] for h in H]
        kb16 = [kh[h][r0:r0 + GLA_SUB, :] for h in H]
        bb16 = [bin_h[h][r0:r0 + GLA_SUB, :] * LOG2_E for h in H]
        groups = [slice(g0, g0 + SUBLANES) for g0 in range(0, GLA_SUB, SUBLANES)]
        agrp = [[jnp.zeros((SUBLANES, LANES), F32) for _ in groups] for _ in H]
        for s in range(GLA_SUB):
            for h in H:
                ks = kb16[h][s:s + 1, :]
                bs = bb16[h][s:s + 1, :]
                for gi, rows in enumerate(groups):
                    if rows.stop <= s:
                        continue
                    pr = qb16[h][rows, :] * ks * jnp.exp2(bb16[h][rows, :] - bs)
                    a = jnp.sum(pr, axis=1, keepdims=True)
                    agrp[h][gi] = jnp.where(lane8 == (r0 + s), a, agrp[h][gi])
        for h in H:
            ablk = jnp.concatenate(agrp[h], axis=0)
            blocks[h].append(jnp.where(lane16 - r0 <= rows16, ablk, 0.0))
    amat = [amat[h] + jnp.concatenate(blocks[h], axis=0) for h in H]

    o = [o0[h] + _dot(amat[h].astype(BF16), vb[h]) for h in H]
    mu = [jnp.mean(o[h], axis=1, keepdims=True) for h in H]
    oc = [o[h] - mu[h] for h in H]
    var = [jnp.mean(oc[h] * oc[h], axis=1, keepdims=True) for h in H]
    for h in H:
        s_st[h] = ecol[h] * s_st[h] + kv[h]
        rg = p_ref[:, 2048 + h * GLA_DV:2048 + (h + 1) * GLA_DV]
        on = oc[h] * lax.rsqrt(var[h] + LN_EPS) * gnorm_ref[:, h * GLA_DV:(h + 1) * GLA_DV]
        mix_ref[:, h * GLA_DV:(h + 1) * GLA_DV] = (on * (rg * _sigmoid(rg))).astype(BF16)


def _odd_mixer(p, x2d, B, S, w_in, li, gla_w2, gla_b, gla_norm):
    T, D = x2d.shape
    L = SEQ_CHUNK
    nc = S // L
    wr = jnp.pad(w_in[li, :, 3072:3072 + GLA_RANK], ((0, 0), (0, LANES - GLA_RANK)))
    w2 = jnp.pad(gla_w2[li], ((0, LANES - GLA_RANK), (0, 0)))
    row = lambda b, c: (b * nc + c, 0)
    full2 = lambda b, c: (0, 0)
    return pl.pallas_call(
        _odd_mixer_kernel,
        out_shape=jax.ShapeDtypeStruct((T, D), BF16),
        grid=(B, nc),
        in_specs=[pl.BlockSpec((L, 3072), row),
                  pl.BlockSpec((L, D), row),
                  pl.BlockSpec((D, LANES), full2),
                  pl.BlockSpec((LANES, 512), full2),
                  pl.BlockSpec((None, 1, 512), lambda b, c: (li, 0, 0)),
                  pl.BlockSpec((None, 1, 1024), lambda b, c: (li, 0, 0))],
        out_specs=pl.BlockSpec((L, D), row),
        scratch_shapes=[pltpu.VMEM((HEADS, HEAD_DK, GLA_DV), F32)],
        compiler_params=_params(("arbitrary", "arbitrary")),
        name="odd_mixer",
    )(p, x2d, wr, w2, gla_b.reshape(-1, 1, 512), gla_norm.reshape(-1, 1, 1024))


def _post_kernel(mix_ref, x_ref, wout_ref, g_ref, b_ref, rw_ref, rb_ref,
                 x1_ref, x1r_ref, eid_ref, rank_ref, gate_ref, cnt_ref,
                 wb_ref, carry_ref):
    tm = mix_ref.shape[0]
    sub = min(POST_SUB, tm)
    i = pl.program_id(0)

    @pl.when(i == 0)
    def _():
        wb_ref[...] = wout_ref[...].astype(BF16)
        carry_ref[...] = jnp.zeros_like(carry_ref)

    rw = rw_ref[...]
    whi = rw.astype(BF16)
    wlo = (rw - whi.astype(F32)).astype(BF16)
    lane = lax.broadcasted_iota(I32, (sub, LANES), 1)
    lane_f = lane.astype(F32)
    ri = lax.broadcasted_iota(I32, (sub, sub), 0)
    ci = lax.broadcasted_iota(I32, (sub, sub), 1)
    strict = jnp.where(ci < ri, 1.0, 0.0).astype(BF16)
    carry = carry_ref[...]

    blocks = [pl.ds(r0, sub) for r0 in range(0, tm, sub)]
    nb = len(blocks)
    x1s = []
    for rows in blocks:
        y = _dot(mix_ref[rows, :], wb_ref[...]) + ALPHA * x_ref[rows, :]
        mu = jnp.mean(y, axis=1, keepdims=True)
        yc = y - mu
        var = jnp.mean(yc * yc, axis=1, keepdims=True)
        x1s.append(yc * lax.rsqrt(var + LN_EPS) * g_ref[...] + b_ref[...])
    for b, rows in enumerate(blocks):
        x1_ref[rows, :] = x1s[b]
        for c in range(SUBLANES):
            x1r_ref[pl.ds(b * sub * SUBLANES + c, sub, stride=SUBLANES), :] = x1s[b][:, c * LANES:(c + 1) * LANES]

    ls = []
    for x1 in x1s:
        hi = x1.astype(BF16)
        lo = (x1 - hi.astype(F32)).astype(BF16)
        ls.append(_dot(hi, whi) + _dot(lo, whi) + _dot(hi, wlo) + rb_ref[...])

    vals = [[] for _ in range(nb)]
    ids = [[] for _ in range(nb)]
    ohs = [[] for _ in range(nb)]
    for _ in range(TOP_K):
        ms = [jnp.max(l, axis=1, keepdims=True) for l in ls]
        idxs = [jnp.min(jnp.where(l == m, lane_f, float(LANES)), axis=1, keepdims=True) for l, m in zip(ls, ms)]
        for b in range(nb):
            oh = lane_f == idxs[b]
            vals[b].append(ms[b])
            ids[b].append(idxs[b])
            ohs[b].append(oh)
            ls[b] = jnp.where(oh, -jnp.inf, ls[b])

    for b, rows in enumerate(blocks):
        es = [jnp.exp(v - vals[b][0]) for v in vals[b]]
        tot = es[0] + es[1] + es[2] + es[3]
        onehot = jnp.zeros((sub, LANES), F32)
        for oh in ohs[b]:
            onehot = onehot + jnp.where(oh, 1.0, 0.0)
        before = _dot(strict, onehot.astype(BF16)) + carry
        carry = carry + jnp.sum(onehot, axis=0, keepdims=True)

        eid = jnp.zeros((sub, LANES), F32)
        rank = jnp.zeros((sub, LANES), F32)
        gate = jnp.zeros((sub, LANES), F32)
        for k in range(TOP_K):
            rk = jnp.sum(jnp.where(ohs[b][k], before, 0.0), axis=1, keepdims=True)
            sel = lane == k
            eid = jnp.where(sel, ids[b][k], eid)
            rank = jnp.where(sel, rk, rank)
            gate = jnp.where(sel, es[k] / tot, gate)
        eid_ref[rows, :] = eid.astype(I32)
        rank_ref[rows, :] = rank.astype(I32)
        gate_ref[rows, :] = gate

    carry_ref[...] = carry
    cnt_ref[...] = carry


def _post(mix, x2d, w_out, li, ln_g, ln_b, router_w, router_b, layer):
    T, D = x2d.shape
    tm = min(POST_TILE, T)
    rw = jnp.pad(router_w[layer], ((0, 0), (0, LANES - N_EXPERTS)))
    rb = jnp.pad(router_b[layer], (0, LANES - N_EXPERTS), constant_values=NEG).reshape(1, LANES)
    row = lambda i: (i, 0)
    full2 = lambda i: (0, 0)
    lane_out = jax.ShapeDtypeStruct((T, LANES), I32)
    return pl.pallas_call(
        _post_kernel,
        out_shape=(jax.ShapeDtypeStruct((T, D), F32),
                   jax.ShapeDtypeStruct((T * SUBLANES, LANES), F32),
                   lane_out, lane_out,
                   jax.ShapeDtypeStruct((T, LANES), F32),
                   jax.ShapeDtypeStruct((1, LANES), F32)),
        grid=(T // tm,),
        in_specs=[pl.BlockSpec((tm, D), row),
                  pl.BlockSpec((tm, D), row),
                  pl.BlockSpec((None, D, D), lambda i: (li, 0, 0)),
                  pl.BlockSpec((None, 1, D), lambda i: (layer, 0, 0)),
                  pl.BlockSpec((None, 1, D), lambda i: (layer, 0, 0)),
                  pl.BlockSpec((D, LANES), full2),
                  pl.BlockSpec((1, LANES), full2)],
        out_specs=(pl.BlockSpec((tm, D), row),
                   pl.BlockSpec((tm * SUBLANES, LANES), row),
                   pl.BlockSpec((tm, LANES), row),
                   pl.BlockSpec((tm, LANES), row),
                   pl.BlockSpec((tm, LANES), row),
                   pl.BlockSpec((1, LANES), full2)),
        scratch_shapes=[pltpu.VMEM((D, D), BF16), pltpu.VMEM((1, LANES), F32)],
        compiler_params=_params(("arbitrary",)),
        name="out_proj_ln_router",
    )(mix, x2d, w_out, ln_g.reshape(-1, 1, D), ln_b.reshape(-1, 1, D), rw, rb)


def _pos_kernel(eid_ref, rank_ref, off_ref, pos_ref):
    tm = eid_ref.shape[0]
    lane = lax.broadcasted_iota(I32, (tm, LANES), 1)
    lane_f = lane.astype(F32)
    eid = eid_ref[...].astype(F32)
    off = off_ref[...]
    pos = jnp.zeros((tm, LANES), F32)
    for k in range(TOP_K):
        ek = jnp.sum(jnp.where(lane == k, eid, 0.0), axis=1, keepdims=True)
        ok = jnp.sum(jnp.where(lane_f == ek, off, 0.0), axis=1, keepdims=True)
        pos = jnp.where(lane == k, ok, pos)
    pos_ref[...] = pos.astype(I32) + rank_ref[...]


def _positions(eid, rank, off_row):
    T = eid.shape[0]
    tm = min(1024, T)
    row = lambda i: (i, 0)
    return pl.pallas_call(
        _pos_kernel,
        out_shape=jax.ShapeDtypeStruct((T, LANES), I32),
        grid=(T // tm,),
        in_specs=[pl.BlockSpec((tm, LANES), row), pl.BlockSpec((tm, LANES), row),
                  pl.BlockSpec((1, LANES), lambda i: (0, 0))],
        out_specs=pl.BlockSpec((tm, LANES), row),
        compiler_params=_params(("arbitrary",)),
        name="route_positions",
    )(eid, rank, off_row)


def _dispatch_kernel(pos_ref, cnt_ref, off_ref, nv_ref, x_ref, xs_ref, stage, zbuf, sems, zsem, *, n_groups):
    G = x_ref.shape[0] // SUBLANES
    gi = pl.program_id(0)
    slot = gi % 2

    def row_copy(tt, k):
        src = stage.at[slot, pl.ds(pl.multiple_of(tt * SUBLANES, SUBLANES), SUBLANES), :]
        p = pos_ref[(gi * G + tt) * TOP_K + k]
        dst = xs_ref.at[pl.ds(pl.multiple_of(p * SUBLANES, SUBLANES), SUBLANES), :]
        return pltpu.make_async_copy(src, dst, sems.at[slot])

    def group_wait(s):
        n = G * TOP_K * SUBLANES
        pltpu.make_async_copy(xs_ref.at[pl.ds(0, n), :], xs_ref.at[pl.ds(0, n), :], sems.at[s]).wait()

    @pl.when(gi >= 2)
    def _():
        group_wait(slot)

    stage[slot] = x_ref[...]

    def tok(tt, carry):
        for k in range(TOP_K):
            row_copy(tt, k).start(priority=k % 2)
        return carry

    lax.fori_loop(0, G, tok, 0)

    @pl.when(gi == n_groups - 1)
    def _():
        group_wait(slot)
        if n_groups >= 2:
            group_wait(1 - slot)

        zbuf[...] = jnp.zeros_like(zbuf)

        def pad_expert(e, carry):
            cnt = cnt_ref[e]
            npad = (FFN_TILE - cnt % FFN_TILE) % FFN_TILE
            base = off_ref[e] + cnt

            def chunk(size):
                first = base + npad - (npad & (2 * size - 1))
                dst = xs_ref.at[pl.ds(pl.multiple_of(first * SUBLANES, SUBLANES), size * SUBLANES), :]
                return pltpu.make_async_copy(zbuf.at[pl.ds(0, size * SUBLANES), :], dst, zsem)

            sizes = [FFN_TILE >> (b + 1) for b in range(FFN_TILE.bit_length() - 1)]
            for size in sizes:
                @pl.when((npad & size) != 0)
                def _():
                    chunk(size).start()
            for size in sizes:
                @pl.when((npad & size) != 0)
                def _():
                    chunk(size).wait()
            return carry

        lax.fori_loop(0, N_EXPERTS, pad_expert, 0)

        tile_rows = FFN_TILE * SUBLANES
        n_tiles = xs_ref.shape[0] // tile_rows

        def tcopy(j):
            dst = xs_ref.at[pl.ds(pl.multiple_of(j * tile_rows, tile_rows), tile_rows), :]
            return pltpu.make_async_copy(zbuf, dst, zsem)

        def tstart(j, c2):
            tcopy(j).start()
            return c2

        def twait(j, c2):
            tcopy(j).wait()
            return c2

        lax.fori_loop(nv_ref[0], n_tiles, tstart, 0)
        lax.fori_loop(nv_ref[0], n_tiles, twait, 0)


def _dispatch(pos_flat, counts, offs, n_valid, x1r, n_rows):
    T = x1r.shape[0] // SUBLANES
    G = min(DISPATCH_GROUP, T)
    return pl.pallas_call(
        functools.partial(_dispatch_kernel, n_groups=T // G),
        out_shape=jax.ShapeDtypeStruct((n_rows * SUBLANES, LANES), F32),
        grid_spec=pltpu.PrefetchScalarGridSpec(
            num_scalar_prefetch=4,
            grid=(T // G,),
            in_specs=[pl.BlockSpec((G * SUBLANES, LANES), lambda i, *_: (i, 0))],
            out_specs=pl.BlockSpec(memory_space=pl.ANY),
            scratch_shapes=[pltpu.VMEM((2, G * SUBLANES, LANES), F32),
                            pltpu.VMEM((FFN_TILE * SUBLANES, LANES), F32),
                            pltpu.SemaphoreType.DMA((2,)),
                            pltpu.SemaphoreType.DMA(())]),
        compiler_params=_params(("arbitrary",)),
        name="dispatch_rows",
    )(pos_flat, counts, offs, n_valid, x1r)


def _ffn_kernel(te_ref, nv_ref, grp_ref, nxt_ref, xs_ref, wgu_hbm, wd_hbm, bg_ref, bl_ref, bd_ref, y_ref,
                xb_ref, wg_s, wl_s, wd_s, wgu_buf, wd_buf, wsems, *, layer):
    tm = FFN_TILE
    j = pl.program_id(0)
    valid = j < nv_ref[0]
    first = jnp.logical_or(j == 0, te_ref[j] != te_ref[jnp.maximum(j - 1, 0)])
    slot = grp_ref[j] % 2

    def weight_copies(e, s):
        return (pltpu.make_async_copy(wgu_hbm.at[layer, e], wgu_buf.at[s], wsems.at[0, s]),
                pltpu.make_async_copy(wd_hbm.at[layer, e], wd_buf.at[s], wsems.at[1, s]))

    @pl.when(j == 0)
    def _():
        for cp in weight_copies(te_ref[0], 0):
            cp.start()

    @pl.when(jnp.logical_and(valid, first))
    def _():
        for cp in weight_copies(te_ref[j], slot):
            cp.wait()

        @pl.when(nxt_ref[j] >= 0)
        def _():
            for cp in weight_copies(nxt_ref[j], 1 - slot):
                cp.start()

        pi = lax.broadcasted_iota(I32, (256, 256), 0)
        pj = lax.broadcasted_iota(I32, (256, 256), 1)
        src = jnp.where(pj < 128, 2 * pj, 2 * (pj - 128) + 1)
        perm = jnp.where(pi == src, 1.0, 0.0).astype(BF16)
        for blk in range(8):
            r = _dot(wgu_buf[slot, :, blk * 256:(blk + 1) * 256].astype(BF16), perm)
            wg_s[:, blk * 128:(blk + 1) * 128] = r[:, 0:128].astype(BF16)
            wl_s[:, blk * 128:(blk + 1) * 128] = r[:, 128:256].astype(BF16)
        wd_s[...] = wd_buf[slot].astype(BF16)

    @pl.when(valid)
    def _():
        for c in range(SUBLANES):
            xb_ref[:, c * LANES:(c + 1) * LANES] = xs_ref[pl.ds(c, tm, stride=SUBLANES), :].astype(BF16)
        xb = xb_ref[...]
        hg = _dot(xb, wg_s[...]) + bg_ref[...]
        hl = _dot(xb, wl_s[...]) + bl_ref[...]
        glu = jnp.minimum(hg, SWIGLU_LIMIT)
        lin = jnp.clip(hl, -SWIGLU_LIMIT, SWIGLU_LIMIT)
        act = glu * _sigmoid(SWIGLU_ALPHA * glu) * (lin + 1.0)
        y = _dot(act.astype(BF16), wd_s[...]) + bd_ref[...]
        for c in range(SUBLANES):
            y_ref[pl.ds(c, tm, stride=SUBLANES), :] = y[:, c * LANES:(c + 1) * LANES]

    @pl.when(jnp.logical_not(valid))
    def _():
        y_ref[...] = jnp.zeros_like(y_ref)


def _ffn(tile_expert, n_valid, tile_group, next_expert, xs, w_gate_up, w_down, bg, bl, b_down, layer, n_tiles):
    tm = FFN_TILE
    D = w_down.shape[-1]
    tile = lambda j, te, nv, grp, nxt: (jnp.minimum(j, nv[0] - 1), 0)
    out_tile = lambda j, te, nv, grp, nxt: (j, 0)
    bias = lambda j, te, nv, grp, nxt: (layer, te[j], 0, 0)
    return pl.pallas_call(
        functools.partial(_ffn_kernel, layer=layer),
        out_shape=jax.ShapeDtypeStruct(xs.shape, F32),
        grid_spec=pltpu.PrefetchScalarGridSpec(
            num_scalar_prefetch=4,
            grid=(n_tiles,),
            in_specs=[pl.BlockSpec((tm * SUBLANES, LANES), tile),
                      pl.BlockSpec(memory_space=pl.ANY),
                      pl.BlockSpec(memory_space=pl.ANY),
                      pl.BlockSpec((None, None, 1, D), bias),
                      pl.BlockSpec((None, None, 1, D), bias),
                      pl.BlockSpec((None, None, 1, D), bias)],
            out_specs=pl.BlockSpec((tm * SUBLANES, LANES), out_tile),
            scratch_shapes=[pltpu.VMEM((tm, D), BF16),
                            pltpu.VMEM((D, D), BF16),
                            pltpu.VMEM((D, D), BF16),
                            pltpu.VMEM((D, D), BF16),
                            pltpu.VMEM((2, D, 2 * D), F32),
                            pltpu.VMEM((2, D, D), F32),
                            pltpu.SemaphoreType.DMA((2, 2))]),
        compiler_params=_params(("arbitrary",)),
        name="expert_ffn",
    )(tile_expert, n_valid, tile_group, next_expert, xs, w_gate_up, w_down, bg, bl, b_down)


def _combine_kernel(pos_ref, y_ref, gate_ref, x1_ref, g_ref, b_ref, o_ref, ybuf0, ybuf1, zs, sems):
    tc = COMBINE_TILE
    i = pl.program_id(0)
    n = pl.num_programs(0)
    bufs = (ybuf0, ybuf1)
    rows = TOP_K * tc * SUBLANES
    per_chunk = tc // SUBLANES

    def row_copy(tile_base, s, tt, k):
        p = pos_ref[tile_base + tt * TOP_K + k]
        src = y_ref.at[pl.ds(pl.multiple_of(p * SUBLANES, SUBLANES), SUBLANES), :]
        dst = bufs[s].at[pl.ds(pl.multiple_of((k * tc + tt) * SUBLANES, SUBLANES), SUBLANES), :]
        return pltpu.make_async_copy(src, dst, sems.at[s])

    def issue_loop(tile_base, s):
        def tok(tt, carry):
            for k in range(TOP_K):
                row_copy(tile_base, s, tt, k).start(priority=k % 2)
            return carry

        lax.fori_loop(0, tc, tok, 0)

    def slot_wait(s):
        pltpu.make_async_copy(y_ref.at[pl.ds(0, rows), :], bufs[s], sems.at[s]).wait()

    def step(s):
        slot_wait(s)
        next_base = jnp.minimum(i + 1, n - 1) * (tc * TOP_K)
        gate = gate_ref[...]
        s1 = jnp.zeros((tc, 1), F32)
        for c in range(SUBLANES):
            for tt in range(c * per_chunk, (c + 1) * per_chunk):
                for k in range(TOP_K):
                    row_copy(next_base, 1 - s, tt, k).start(priority=k % 2)
            z = ALPHA * x1_ref[:, c * LANES:(c + 1) * LANES]
            for k in range(TOP_K):
                z = z + gate[:, k:k + 1] * bufs[s][pl.ds(k * tc * SUBLANES + c, tc, stride=SUBLANES), :]
            zs[:, c * LANES:(c + 1) * LANES] = z
            s1 = s1 + jnp.sum(z, axis=1, keepdims=True)
        zv = zs[...]
        mu = s1 / zv.shape[1]
        zc = zv - mu
        var = jnp.mean(zc * zc, axis=1, keepdims=True)
        o_ref[...] = zc * lax.rsqrt(var + LN_EPS) * g_ref[...] + b_ref[...]

        @pl.when(i == n - 1)
        def _():
            slot_wait(1 - s)

    @pl.when(i == 0)
    def _():
        issue_loop(0, 0)

    @pl.when(i % 2 == 0)
    def _():
        step(0)

    @pl.when(i % 2 == 1)
    def _():
        step(1)


def _combine(pos_flat, y, gate, x1, ln_g, ln_b, layer):
    T, D = x1.shape
    tc = COMBINE_TILE
    row = lambda i, pos: (i, 0)
    return pl.pallas_call(
        _combine_kernel,
        out_shape=jax.ShapeDtypeStruct((T, D), F32),
        grid_spec=pltpu.PrefetchScalarGridSpec(
            num_scalar_prefetch=1,
            grid=(T // tc,),
            in_specs=[pl.BlockSpec(memory_space=pl.ANY),
                      pl.BlockSpec((tc, LANES), row),
                      pl.BlockSpec((tc, D), row),
                      pl.BlockSpec((None, 1, D), lambda i, pos: (layer, 0, 0)),
                      pl.BlockSpec((None, 1, D), lambda i, pos: (layer, 0, 0))],
            out_specs=pl.BlockSpec((tc, D), row),
            scratch_shapes=[pltpu.VMEM((TOP_K * tc * SUBLANES, LANES), F32),
                            pltpu.VMEM((TOP_K * tc * SUBLANES, LANES), F32),
                            pltpu.VMEM((tc, D), F32),
                            pltpu.SemaphoreType.DMA((2,))]),
        compiler_params=_params(("arbitrary",)),
        name="combine_ln",
    )(pos_flat, y, gate, x1, ln_g.reshape(-1, 1, D), ln_b.reshape(-1, 1, D))


def _moe(x1, x1r, eid, rank, gate, cnt, layer, w_gate_up, bg, bl, w_down, b_down, ln_g, ln_b):
    T, D = x1.shape
    tm = FFN_TILE
    n_tiles = (T * TOP_K) // tm + N_EXPERTS
    n_rows = n_tiles * tm
    counts = cnt[0, :N_EXPERTS].astype(I32)
    tiles_e = (counts + tm - 1) // tm
    tile_end = jnp.cumsum(tiles_e)
    offs = (tile_end - tiles_e) * tm
    n_valid = tile_end[-1]
    jj = jnp.minimum(jnp.arange(n_tiles, dtype=I32), n_valid - 1)
    tile_expert = jnp.minimum(jnp.sum(tile_end[None, :] <= jj[:, None], axis=1), N_EXPERTS - 1).astype(I32)
    off_row = jnp.pad(offs, (0, LANES - N_EXPERTS)).astype(F32).reshape(1, LANES)
    present = tiles_e > 0
    group_of_e = jnp.cumsum(present.astype(I32)) - 1
    eidx = jnp.arange(N_EXPERTS, dtype=I32)
    later = jnp.where(present[None, :] & (eidx[None, :] > eidx[:, None]), eidx[None, :], N_EXPERTS)
    next_e = jnp.min(later, axis=1)
    next_e = jnp.where(next_e == N_EXPERTS, -1, next_e).astype(I32)
    mine = tile_expert[:, None] == eidx[None, :]
    tile_group = jnp.sum(jnp.where(mine, group_of_e[None, :], 0), axis=1).astype(I32)
    next_expert = jnp.sum(jnp.where(mine, next_e[None, :], 0), axis=1).astype(I32)

    pos = _positions(eid, rank, off_row)
    pos_flat = pos[:, :TOP_K].reshape(-1)
    nv = n_valid.reshape(1).astype(I32)
    xs = _dispatch(pos_flat, counts, offs.astype(I32), nv, x1r, n_rows)
    y = _ffn(tile_expert, nv, tile_group, next_expert, xs, w_gate_up, w_down, bg, bl,
             b_down.reshape(DEPTH, N_EXPERTS, 1, D), layer, n_tiles)
    return _combine(pos_flat, y, gate, x1, ln_g, ln_b, layer)


def kernel(x, even_w_in, pool_w, pool_scale, conv_w, conv_b, i_bias, f_bias, ml_norm, even_w_out, odd_w_in, gla_w2, gla_b, gla_norm, odd_w_out, ln1_g, ln1_b, ln2_g, ln2_b, router_w, router_b, w_gate_up, b_gate_up, w_down, b_down):
    B, S, D = x.shape
    xt = x.reshape(B * S, D)
    bgu = b_gate_up.reshape(DEPTH, N_EXPERTS, D, 2)
    bg = bgu[..., 0].reshape(DEPTH, N_EXPERTS, 1, D)
    bl = bgu[..., 1].reshape(DEPTH, N_EXPERTS, 1, D)
    for layer in range(DEPTH):
        li = layer // 2
        if layer % 2 == 0:
            p = _project(xt, even_w_in, li, 2560)
            mix = _even_mixer(p, xt, B, S, even_w_in, li, pool_w, pool_scale, conv_w, conv_b,
                              i_bias, f_bias, ml_norm)
            w_out = even_w_out
        else:
            p = _project(xt, odd_w_in, li, 3072)
            mix = _odd_mixer(p, xt, B, S, odd_w_in, li, gla_w2, gla_b, gla_norm)
            w_out = odd_w_out
        x1, x1r, eid, rank, gate, cnt = _post(mix, xt, w_out, li, ln1_g, ln1_b, router_w, router_b, layer)
        xt = _moe(x1, x1r, eid, rank, gate, cnt, layer, w_gate_up, bg, bl, w_down, b_down,
                  ln2_g, ln2_b)
    return xt.reshape(B, S, D)
```

```python
import functools

import jax
import jax.numpy as jnp
from jax import lax
from jax.experimental import pallas as pl
from jax.experimental.pallas import tpu as pltpu

F32 = jnp.float32
BF16 = jnp.bfloat16
I32 = jnp.int32

LANES = 128
SUBLANES = 8
VMEM_LIMIT = 56 * 1024 * 1024

DEPTH = 4
ALPHA = (2 * DEPTH) ** 0.25
LN_EPS = 1e-5
NEG = -1e30
LOG2_E = 1.4426950408889634

POOL_WINDOWS = (2, 4, 8, 16)
POOL_CARRY = 16
CONV_K = 4
CONV_CARRY = 8
HEADS = 4
HEAD_DK = 128
GLA_DV = 256
GLA_RANK = 16
GLA_TAU = 16.0
GLA_SUB = 16
N_EXPERTS = 32
TOP_K = 4
SWIGLU_LIMIT = 7.0
SWIGLU_ALPHA = 1.702

SEQ_CHUNK = 128
POST_TILE = 1024
POST_SUB = 128
FFN_TILE = 256
COMBINE_TILE = 128
DISPATCH_GROUP = 128


def _dot(a, b):
    return jnp.dot(a, b, preferred_element_type=F32)


def _dot_nt(a, b):
    return lax.dot_general(a, b, (((1,), (1,)), ((), ())), preferred_element_type=F32)


def _split3(a):
    a1 = a.astype(BF16)
    r1 = a - a1.astype(F32)
    a2 = r1.astype(BF16)
    r2 = r1 - a2.astype(F32)
    return a1, a2, r2.astype(BF16)


def _dot01_left(m01, a):
    a1, a2, a3 = _split3(a)
    return _dot(m01, a1) + _dot(m01, a2) + _dot(m01, a3)


def _dot01_right(a, m01):
    a1, a2, a3 = _split3(a)
    return _dot(a1, m01) + _dot(a2, m01) + _dot(a3, m01)


def _log_sigmoid(x):
    return jnp.minimum(x, 0.0) - jnp.log1p(jnp.exp(-jnp.abs(x)))


def _sigmoid(x):
    return 1.0 / (1.0 + jnp.exp(-x))


def _params(sem):
    return pltpu.CompilerParams(dimension_semantics=sem, vmem_limit_bytes=VMEM_LIMIT)


def _proj_kernel(x_ref, w_hbm, o_ref, wb_ref, stage, sems, *, li, n_cols, tn):
    @pl.when(pl.program_id(0) == 0)
    def _():
        n_chunks = n_cols // tn

        def chunk_copy(n):
            return pltpu.make_async_copy(w_hbm.at[li, :, pl.ds(n * tn, tn)], stage.at[n % 2], sems.at[n % 2])

        chunk_copy(0).start()
        for n in range(n_chunks):
            if n + 1 < n_chunks:
                chunk_copy(n + 1).start()
            chunk_copy(n).wait()
            wb_ref[:, n * tn:(n + 1) * tn] = stage[n % 2].astype(BF16)

    o_ref[...] = _dot(x_ref[...].astype(BF16), wb_ref[...])


def _project(x2d, w3d, li, n_cols, tn=512):
    T, D = x2d.shape
    tm = min(512, T)
    return pl.pallas_call(
        functools.partial(_proj_kernel, li=li, n_cols=n_cols, tn=tn),
        out_shape=jax.ShapeDtypeStruct((T, n_cols), F32),
        grid=(T // tm,),
        in_specs=[pl.BlockSpec((tm, D), lambda i: (i, 0)),
                  pl.BlockSpec(memory_space=pl.ANY)],
        out_specs=pl.BlockSpec((tm, n_cols), lambda i: (i, 0)),
        scratch_shapes=[pltpu.VMEM((D, n_cols), BF16),
                        pltpu.VMEM((2, D, tn), F32),
                        pltpu.SemaphoreType.DMA((2,))],
        compiler_params=_params(("arbitrary",)),
        name="in_proj",
    )(x2d, w3d)


def _even_mixer_kernel(p_ref, x_ref, wgc_ref, wgr_ref, gbr_ref, gbc_ref, poolw_ref, pscale_ref,
                       convw_ref, convb_ref, mlnorm_ref, mix_ref,
                       ubuf, qkbuf, c_st, n_st, m_st, tri):
    L = SEQ_CHUNK
    c = pl.program_id(1)

    @pl.when(c == 0)
    def _():
        ri0 = lax.broadcasted_iota(I32, (L, L), 0)
        ci0 = lax.broadcasted_iota(I32, (L, L), 1)
        tri[0] = jnp.where(ci0 <= ri0, 1.0, 0.0).astype(BF16)
        tri[1] = jnp.where(ri0 <= ci0, 1.0, 0.0).astype(BF16)
        ubuf[0:POOL_CARRY, :] = jnp.zeros((POOL_CARRY, 512), F32)
        qkbuf[0:CONV_CARRY, :] = jnp.zeros((CONV_CARRY, 1024), F32)
        c_st[...] = jnp.zeros_like(c_st)
        n_st[...] = jnp.zeros_like(n_st)
        m_st[...] = jnp.zeros_like(m_st)

    ubuf[POOL_CARRY:POOL_CARRY + L, :] = p_ref[:, 0:512]
    pos1 = lax.broadcasted_iota(I32, (L, 1), 0) + c * L + 1
    for g, w in enumerate(POOL_WINDOWS):
        lo = g * 128
        acc = ubuf[pl.ds(POOL_CARRY, L), lo:lo + 128]
        cur = acc
        for j in range(1, w):
            acc = acc + ubuf[pl.ds(POOL_CARRY - j, L), lo:lo + 128]
        inv_cnt = 1.0 / jnp.minimum(pos1, w).astype(F32)
        d = acc * inv_cnt - cur
        yp = _dot(d.astype(BF16), poolw_ref[g].astype(BF16)) * pscale_ref[:, lo:lo + 128]
        mix_ref[:, lo:lo + 128] = yp.astype(BF16)
    ubuf[0:POOL_CARRY, :] = ubuf[L:L + POOL_CARRY, :]

    qkbuf[CONV_CARRY:CONV_CARRY + L, :] = p_ref[:, 512:1536]
    acc = jnp.zeros((L, 1024), F32) + convb_ref[...]
    for j in range(CONV_K):
        acc = acc + convw_ref[j:j + 1, :] * qkbuf[pl.ds(CONV_CARRY - (CONV_K - 1) + j, L), :]
    qk = acc * _sigmoid(acc)
    qkbuf[0:CONV_CARRY, :] = qkbuf[L:L + CONV_CARRY, :]

    xb = x_ref[...].astype(BF16)
    gcol = _dot(xb, wgc_ref[...].astype(BF16)) + gbr_ref[...]
    grow = _dot_nt(wgr_ref[...].astype(BF16), xb) + gbc_ref[:, 0:1]
    ri = lax.broadcasted_iota(I32, (L, L), 0)
    ci = lax.broadcasted_iota(I32, (L, L), 1)
    causal = ci <= ri
    b_col = _dot01_left(tri[0], _log_sigmoid(gcol))
    b_row = _dot01_right(_log_sigmoid(grow), tri[1])

    kscale = HEAD_DK ** -0.5
    H = range(HEADS)
    qh = [qk[:, h * 128:(h + 1) * 128] for h in H]
    kh = [qk[:, 512 + h * 128:512 + (h + 1) * 128] * kscale for h in H]
    qb = [q.astype(BF16) for q in qh]
    kb = [k.astype(BF16) for k in kh]
    vb = [p_ref[:, 1536 + h * 128:1536 + (h + 1) * 128].astype(BF16) for h in H]
    b_c = [b_col[:, 4 + h:5 + h] for h in H]
    b_r = [b_row[4 + h:5 + h, :] for h in H]
    ig_c = [gcol[:, h:h + 1] for h in H]
    ig_r = [grow[h:h + 1, :] for h in H]
    g = [b_r[h][:, L - 1:L] for h in H]
    m_prev = [m_st[h][:, 0:1] for h in H]

    dmat = [jnp.where(causal, b_c[h] - b_r[h] + ig_r[h], NEG) for h in H]
    m_inter = [b_c[h] + m_prev[h] for h in H]
    dmax = [jnp.max(dmat[h], axis=1, keepdims=True) for h in H]
    qkt = [_dot_nt(qb[h], kb[h]) for h in H]
    qc = [_dot(qb[h], c_st[h].astype(BF16)) for h in H]
    qn = [jnp.sum(qh[h] * n_st[h], axis=1, keepdims=True) for h in H]
    m_t = [jnp.maximum(m_inter[h], dmax[h]) for h in H]
    sts = [qkt[h] * jnp.exp(dmat[h] - m_t[h]) for h in H]
    sc = [jnp.exp(m_inter[h] - m_t[h]) for h in H]
    sv = [_dot(sts[h].astype(BF16), vb[h]) for h in H]
    ssum = [jnp.sum(sts[h], axis=1, keepdims=True) for h in H]

    a_c = [g[h] - b_c[h] + ig_c[h] for h in H]
    a_r = [g[h] - b_r[h] + ig_r[h] for h in H]
    m_new = [jnp.maximum(g[h] + m_prev[h], jnp.max(a_r[h], axis=1, keepdims=True)) for h in H]
    decay = [jnp.exp(g[h] + m_prev[h] - m_new[h]) for h in H]
    wk = [jnp.exp(a_c[h] - m_new[h]) * kh[h] for h in H]
    wkt = [wk[h].T.astype(BF16) for h in H]
    kv = [_dot(wkt[h], vb[h]) for h in H]

    num = [sv[h] + sc[h] * qc[h] for h in H]
    den = [ssum[h] + sc[h] * qn[h] for h in H]
    hh = [num[h] * (1.0 / jnp.maximum(jnp.abs(den[h]), jnp.exp(-m_t[h]))) for h in H]
    mu = [jnp.mean(hh[h], axis=1, keepdims=True) for h in H]
    hc = [hh[h] - mu[h] for h in H]
    var = [jnp.mean(hc[h] * hc[h], axis=1, keepdims=True) for h in H]
    for h in H:
        lo = h * 128
        c_st[h] = decay[h] * c_st[h] + kv[h]
        n_st[h] = decay[h] * n_st[h] + jnp.sum(wk[h], axis=0, keepdims=True)
        m_st[h] = jnp.broadcast_to(m_new[h], (1, LANES))
        og = p_ref[:, 2048 + lo:2048 + lo + 128]
        hn = hc[h] * lax.rsqrt(var[h] + LN_EPS) * mlnorm_ref[:, lo:lo + 128]
        mix_ref[:, 512 + lo:512 + lo + 128] = (hn * _sigmoid(og)).astype(BF16)


def _even_mixer(p, x2d, B, S, w_in, li, pool_w, pool_scale, conv_w, conv_b, i_bias, f_bias, ml_norm):
    T, D = x2d.shape
    L = SEQ_CHUNK
    nc = S // L
    wg = w_in[li, :, 2560:2568]
    wgc = jnp.pad(wg, ((0, 0), (0, LANES - 8)))
    wgr = wg.T
    gb = jnp.concatenate([i_bias[li], f_bias[li]])
    gbr = jnp.pad(gb, (0, LANES - 8)).reshape(1, LANES)
    gbc = jnp.broadcast_to(gb.reshape(8, 1), (8, LANES))
    row = lambda b, c: (b * nc + c, 0)
    full2 = lambda b, c: (0, 0)
    return pl.pallas_call(
        _even_mixer_kernel,
        out_shape=jax.ShapeDtypeStruct((T, D), BF16),
        grid=(B, nc),
        in_specs=[pl.BlockSpec((L, 2560), row),
                  pl.BlockSpec((L, D), row),
                  pl.BlockSpec((D, LANES), full2),
                  pl.BlockSpec((8, D), full2),
                  pl.BlockSpec((1, LANES), full2),
                  pl.BlockSpec((8, LANES), full2),
                  pl.BlockSpec((None, 4, 128, 128), lambda b, c: (li, 0, 0, 0)),
                  pl.BlockSpec((None, 1, 512), lambda b, c: (li, 0, 0)),
                  pl.BlockSpec((None, CONV_K, 1024), lambda b, c: (li, 0, 0)),
                  pl.BlockSpec((None, 1, 1024), lambda b, c: (li, 0, 0)),
                  pl.BlockSpec((None, 1, 512), lambda b, c: (li, 0, 0))],
        out_specs=pl.BlockSpec((L, D), row),
        scratch_shapes=[pltpu.VMEM((L + POOL_CARRY, 512), F32),
                        pltpu.VMEM((L + CONV_CARRY, 1024), F32),
                        pltpu.VMEM((HEADS, 128, 128), F32),
                        pltpu.VMEM((HEADS, 1, 128), F32),
                        pltpu.VMEM((HEADS, 1, LANES), F32),
                        pltpu.VMEM((2, L, L), BF16)],
        compiler_params=_params(("arbitrary", "arbitrary")),
        name="even_mixer",
    )(p, x2d, wgc, wgr, gbr, gbc, pool_w, pool_scale.reshape(-1, 1, 512), conv_w,
      conv_b.reshape(-1, 1, 1024), ml_norm.reshape(-1, 1, 512))


def _odd_mixer_kernel(p_ref, x_ref, wr_ref, w2_ref, gb_ref, gnorm_ref, mix_ref, s_st):
    L = SEQ_CHUNK
    c = pl.program_id(1)

    @pl.when(c == 0)
    def _():
        s_st[...] = jnp.zeros_like(s_st)

    xb = x_ref[...].astype(BF16)
    glr = _dot(xb, wr_ref[...].astype(BF16))
    z = _dot(glr.astype(BF16), w2_ref[...].astype(BF16)) + gb_ref[...]
    lg = _log_sigmoid(z) * (1.0 / GLA_TAU)

    ri = lax.broadcasted_iota(I32, (L, L), 0)
    ci = lax.broadcasted_iota(I32, (L, L), 1)
    one = lambda m: jnp.where(m, 1.0, 0.0).astype(BF16)
    causal = ci <= ri
    blk0 = (ri // GLA_SUB) * GLA_SUB
    m_in = one(causal & (ci >= blk0))
    m_all = one(causal)
    lg3 = _split3(lg)
    cum = lambda m01: _dot(m01, lg3[0]) + _dot(m01, lg3[1]) + _dot(m01, lg3[2])
    b_in = cum(m_in)
    b_all = cum(m_all)

    levels = []
    for hs in (64, 32, 16):
        ps = 2 * hs
        pair0 = (ri // ps) * ps
        refrow = pair0 + hs - 1
        upper = (ri - pair0) >= hs
        dq = cum(one(upper & (ci > refrow) & causal))
        dk = cum(one((~upper) & (ci > ri) & (ci <= refrow)))
        rcol = lax.broadcasted_iota(I32, (L, HEAD_DK), 0)
        up_c = ((rcol % ps) >= hs)
        same_pair = (ri // ps) == (ci // ps)
        levels.append((dq, dk, up_c, same_pair, ps))

    rows16 = lax.broadcasted_iota(I32, (GLA_SUB, LANES), 0)
    lane16 = lax.broadcasted_iota(I32, (GLA_SUB, LANES), 1)
    lane8 = lax.broadcasted_iota(I32, (SUBLANES, LANES), 1)
    eye = ri == ci
    qscale = HEAD_DK ** -0.5
    H = range(HEADS)
    qh = [p_ref[:, h * 128:(h + 1) * 128] * qscale for h in H]
    kh = [p_ref[:, 512 + h * 128:512 + (h + 1) * 128] for h in H]
    vb = [p_ref[:, 1024 + h * GLA_DV:1024 + (h + 1) * GLA_DV].astype(BF16) for h in H]
    bh = [b_all[:, h * 128:(h + 1) * 128] for h in H]
    bin_h = [b_in[:, h * 128:(h + 1) * 128] for h in H]

    o0 = [_dot((qh[h] * jnp.exp(bh[h])).astype(BF16), s_st[h].astype(BF16)) for h in H]

    amat = [jnp.zeros((L, L), F32) for _ in H]
    for dq, dk, up_c, same_pair, ps in levels:
        qt = [jnp.where(up_c, qh[h] * jnp.exp(dq[:, h * 128:(h + 1) * 128]), 0.0).astype(BF16) for h in H]
        kt = [jnp.where(up_c, 0.0, kh[h] * jnp.exp(dk[:, h * 128:(h + 1) * 128])).astype(BF16) for h in H]
        a_lev = [_dot_nt(qt[h], kt[h]) for h in H]
        amat = [amat[h] + (a_lev[h] if ps == L else jnp.where(same_pair, a_lev[h], 0.0)) for h in H]

    bl = [bh[h][L - 1:L, :] for h in H]
    kdec_t = [(kh[h] * jnp.exp(bl[h] - bh[h])).T.astype(BF16) for h in H]
    ecol = [jnp.sum(jnp.where(eye, jnp.exp(bl[h]), 0.0), axis=1, keepdims=True) for h in H]
    kv = [_dot(kdec_t[h], vb[h]) for h in H]

    blocks = [[] for _ in H]
    for blk in range(L // GLA_SUB):
        r0 = blk * GLA_SUB
        qb16 = [qh[h][r0:r0 + GLA_SUB, :] for h in H]
        kb16 = [kh[h][r0:r0 + GLA_SUB, :] for h in H]
        bb16 = [bin_h[h][r0:r0 + GLA_SUB, :] * LOG2_E for h in H]
        groups = [slice(g0, g0 + SUBLANES) for g0 in range(0, GLA_SUB, SUBLANES)]
        agrp = [[jnp.zeros((SUBLANES, LANES), F32) for _ in groups] for _ in H]
        for s in range(GLA_SUB):
            for h in H:
                ks = kb16[h][s:s + 1, :]
                bs = bb16[h][s:s + 1, :]
                for gi, rows in enumerate(groups):
                    if rows.stop <= s:
                        continue
                    pr = qb16[h][rows, :] * ks * jnp.exp2(bb16[h][rows, :] - bs)
                    a = jnp.sum(pr, axis=1, keepdims=True)
                    agrp[h][gi] = jnp.where(lane8 == (r0 + s), a, agrp[h][gi])
        for h in H:
            ablk = jnp.concatenate(agrp[h], axis=0)
            blocks[h].append(jnp.where(lane16 - r0 <= rows16, ablk, 0.0))
    amat = [amat[h] + jnp.concatenate(blocks[h], axis=0) for h in H]

    o = [o0[h] + _dot(amat[h].astype(BF16), vb[h]) for h in H]
    mu = [jnp.mean(o[h], axis=1, keepdims=True) for h in H]
    oc = [o[h] - mu[h] for h in H]
    var = [jnp.mean(oc[h] * oc[h], axis=1, keepdims=True) for h in H]
    for h in H:
        s_st[h] = ecol[h] * s_st[h] + kv[h]
        rg = p_ref[:, 2048 + h * GLA_DV:2048 + (h + 1) * GLA_DV]
        on = oc[h] * lax.rsqrt(var[h] + LN_EPS) * gnorm_ref[:, h * GLA_DV:(h + 1) * GLA_DV]
        mix_ref[:, h * GLA_DV:(h + 1) * GLA_DV] = (on * (rg * _sigmoid(rg))).astype(BF16)


def _odd_mixer(p, x2d, B, S, w_in, li, gla_w2, gla_b, gla_norm):
    T, D = x2d.shape
    L = SEQ_CHUNK
    nc = S // L
    wr = jnp.pad(w_in[li, :, 3072:3072 + GLA_RANK], ((0, 0), (0, LANES - GLA_RANK)))
    w2 = jnp.pad(gla_w2[li], ((0, LANES - GLA_RANK), (0, 0)))
    row = lambda b, c: (b * nc + c, 0)
    full2 = lambda b, c: (0, 0)
    return pl.pallas_call(
        _odd_mixer_kernel,
        out_shape=jax.ShapeDtypeStruct((T, D), BF16),
        grid=(B, nc),
        in_specs=[pl.BlockSpec((L, 3072), row),
                  pl.BlockSpec((L, D), row),
                  pl.BlockSpec((D, LANES), full2),
                  pl.BlockSpec((LANES, 512), full2),
                  pl.BlockSpec((None, 1, 512), lambda b, c: (li, 0, 0)),
                  pl.BlockSpec((None, 1, 1024), lambda b, c: (li, 0, 0))],
        out_specs=pl.BlockSpec((L, D), row),
        scratch_shapes=[pltpu.VMEM((HEADS, HEAD_DK, GLA_DV), F32)],
        compiler_params=_params(("arbitrary", "arbitrary")),
        name="odd_mixer",
    )(p, x2d, wr, w2, gla_b.reshape(-1, 1, 512), gla_norm.reshape(-1, 1, 1024))


def _post_kernel(mix_ref, x_ref, wout_ref, g_ref, b_ref, rw_ref, rb_ref,
                 x1_ref, x1r_ref, eid_ref, rank_ref, gate_ref, cnt_ref,
                 wb_ref, carry_ref):
    tm = mix_ref.shape[0]
    sub = min(POST_SUB, tm)
    i = pl.program_id(0)

    @pl.when(i == 0)
    def _():
        wb_ref[...] = wout_ref[...].astype(BF16)
        carry_ref[...] = jnp.zeros_like(carry_ref)

    rw = rw_ref[...]
    whi = rw.astype(BF16)
    wlo = (rw - whi.astype(F32)).astype(BF16)
    lane = lax.broadcasted_iota(I32, (sub, LANES), 1)
    lane_f = lane.astype(F32)
    ri = lax.broadcasted_iota(I32, (sub, sub), 0)
    ci = lax.broadcasted_iota(I32, (sub, sub), 1)
    strict = jnp.where(ci < ri, 1.0, 0.0).astype(BF16)
    carry = carry_ref[...]

    blocks = [pl.ds(r0, sub) for r0 in range(0, tm, sub)]
    nb = len(blocks)
    x1s = []
    for rows in blocks:
        y = _dot(mix_ref[rows, :], wb_ref[...]) + ALPHA * x_ref[rows, :]
        mu = jnp.mean(y, axis=1, keepdims=True)
        yc = y - mu
        var = jnp.mean(yc * yc, axis=1, keepdims=True)
        x1s.append(yc * lax.rsqrt(var + LN_EPS) * g_ref[...] + b_ref[...])
    for b, rows in enumerate(blocks):
        x1_ref[rows, :] = x1s[b]
        for c in range(SUBLANES):
            x1r_ref[pl.ds(b * sub * SUBLANES + c, sub, stride=SUBLANES), :] = x1s[b][:, c * LANES:(c + 1) * LANES]

    ls = []
    for x1 in x1s:
        hi = x1.astype(BF16)
        lo = (x1 - hi.astype(F32)).astype(BF16)
        ls.append(_dot(hi, whi) + _dot(lo, whi) + _dot(hi, wlo) + rb_ref[...])

    vals = [[] for _ in range(nb)]
    ids = [[] for _ in range(nb)]
    ohs = [[] for _ in range(nb)]
    for _ in range(TOP_K):
        ms = [jnp.max(l, axis=1, keepdims=True) for l in ls]
        idxs = [jnp.min(jnp.where(l == m, lane_f, float(LANES)), axis=1, keepdims=True) for l, m in zip(ls, ms)]
        for b in range(nb):
            oh = lane_f == idxs[b]
            vals[b].append(ms[b])
            ids[b].append(idxs[b])
            ohs[b].append(oh)
            ls[b] = jnp.where(oh, -jnp.inf, ls[b])

    for b, rows in enumerate(blocks):
        es = [jnp.exp(v - vals[b][0]) for v in vals[b]]
        tot = es[0] + es[1] + es[2] + es[3]
        onehot = jnp.zeros((sub, LANES), F32)
        for oh in ohs[b]:
            onehot = onehot + jnp.where(oh, 1.0, 0.0)
        before = _dot(strict, onehot.astype(BF16)) + carry
        carry = carry + jnp.sum(onehot, axis=0, keepdims=True)

        eid = jnp.zeros((sub, LANES), F32)
        rank = jnp.zeros((sub, LANES), F32)
        gate = jnp.zeros((sub, LANES), F32)
        for k in range(TOP_K):
            rk = jnp.sum(jnp.where(ohs[b][k], before, 0.0), axis=1, keepdims=True)
            sel = lane == k
            eid = jnp.where(sel, ids[b][k], eid)
            rank = jnp.where(sel, rk, rank)
            gate = jnp.where(sel, es[k] / tot, gate)
        eid_ref[rows, :] = eid.astype(I32)
        rank_ref[rows, :] = rank.astype(I32)
        gate_ref[rows, :] = gate

    carry_ref[...] = carry
    cnt_ref[...] = carry


def _post(mix, x2d, w_out, li, ln_g, ln_b, router_w, router_b, layer):
    T, D = x2d.shape
    tm = min(POST_TILE, T)
    rw = jnp.pad(router_w[layer], ((0, 0), (0, LANES - N_EXPERTS)))
    rb = jnp.pad(router_b[layer], (0, LANES - N_EXPERTS), constant_values=NEG).reshape(1, LANES)
    row = lambda i: (i, 0)
    full2 = lambda i: (0, 0)
    lane_out = jax.ShapeDtypeStruct((T, LANES), I32)
    return pl.pallas_call(
        _post_kernel,
        out_shape=(jax.ShapeDtypeStruct((T, D), F32),
                   jax.ShapeDtypeStruct((T * SUBLANES, LANES), F32),
                   lane_out, lane_out,
                   jax.ShapeDtypeStruct((T, LANES), F32),
                   jax.ShapeDtypeStruct((1, LANES), F32)),
        grid=(T // tm,),
        in_specs=[pl.BlockSpec((tm, D), row),
                  pl.BlockSpec((tm, D), row),
                  pl.BlockSpec((None, D, D), lambda i: (li, 0, 0)),
                  pl.BlockSpec((None, 1, D), lambda i: (layer, 0, 0)),
                  pl.BlockSpec((None, 1, D), lambda i: (layer, 0, 0)),
                  pl.BlockSpec((D, LANES), full2),
                  pl.BlockSpec((1, LANES), full2)],
        out_specs=(pl.BlockSpec((tm, D), row),
                   pl.BlockSpec((tm * SUBLANES, LANES), row),
                   pl.BlockSpec((tm, LANES), row),
                   pl.BlockSpec((tm, LANES), row),
                   pl.BlockSpec((tm, LANES), row),
                   pl.BlockSpec((1, LANES), full2)),
        scratch_shapes=[pltpu.VMEM((D, D), BF16), pltpu.VMEM((1, LANES), F32)],
        compiler_params=_params(("arbitrary",)),
        name="out_proj_ln_router",
    )(mix, x2d, w_out, ln_g.reshape(-1, 1, D), ln_b.reshape(-1, 1, D), rw, rb)


def _pos_kernel(eid_ref, rank_ref, off_ref, pos_ref):
    tm = eid_ref.shape[0]
    lane = lax.broadcasted_iota(I32, (tm, LANES), 1)
    lane_f = lane.astype(F32)
    eid = eid_ref[...].astype(F32)
    off = off_ref[...]
    pos = jnp.zeros((tm, LANES), F32)
    for k in range(TOP_K):
        ek = jnp.sum(jnp.where(lane == k, eid, 0.0), axis=1, keepdims=True)
        ok = jnp.sum(jnp.where(lane_f == ek, off, 0.0), axis=1, keepdims=True)
        pos = jnp.where(lane == k, ok, pos)
    pos_ref[...] = pos.astype(I32) + rank_ref[...]


def _positions(eid, rank, off_row):
    T = eid.shape[0]
    tm = min(1024, T)
    row = lambda i: (i, 0)
    return pl.pallas_call(
        _pos_kernel,
        out_shape=jax.ShapeDtypeStruct((T, LANES), I32),
        grid=(T // tm,),
        in_specs=[pl.BlockSpec((tm, LANES), row), pl.BlockSpec((tm, LANES), row),
                  pl.BlockSpec((1, LANES), lambda i: (0, 0))],
        out_specs=pl.BlockSpec((tm, LANES), row),
        compiler_params=_params(("arbitrary",)),
        name="route_positions",
    )(eid, rank, off_row)


def _dispatch_kernel(pos_ref, cnt_ref, off_ref, nv_ref, x_ref, xs_ref, stage, zbuf, sems, zsem, *, n_groups):
    G = x_ref.shape[0] // SUBLANES
    gi = pl.program_id(0)
    slot = gi % 2

    def row_copy(tt, k):
        src = stage.at[slot, pl.ds(pl.multiple_of(tt * SUBLANES, SUBLANES), SUBLANES), :]
        p = pos_ref[(gi * G + tt) * TOP_K + k]
        dst = xs_ref.at[pl.ds(pl.multiple_of(p * SUBLANES, SUBLANES), SUBLANES), :]
        return pltpu.make_async_copy(src, dst, sems.at[slot])

    def group_wait(s):
        n = G * TOP_K * SUBLANES
        pltpu.make_async_copy(xs_ref.at[pl.ds(0, n), :], xs_ref.at[pl.ds(0, n), :], sems.at[s]).wait()

    @pl.when(gi >= 2)
    def _():
        group_wait(slot)

    stage[slot] = x_ref[...]

    def tok(tt, carry):
        for k in range(TOP_K):
            row_copy(tt, k).start(priority=k % 2)
        return carry

    lax.fori_loop(0, G, tok, 0)

    @pl.when(gi == n_groups - 1)
    def _():
        group_wait(slot)
        if n_groups >= 2:
            group_wait(1 - slot)

        zbuf[...] = jnp.zeros_like(zbuf)

        def pad_expert(e, carry):
            cnt = cnt_ref[e]
            npad = (FFN_TILE - cnt % FFN_TILE) % FFN_TILE
            base = off_ref[e] + cnt

            def chunk(size):
                first = base + npad - (npad & (2 * size - 1))
                dst = xs_ref.at[pl.ds(pl.multiple_of(first * SUBLANES, SUBLANES), size * SUBLANES), :]
                return pltpu.make_async_copy(zbuf.at[pl.ds(0, size * SUBLANES), :], dst, zsem)

            sizes = [FFN_TILE >> (b + 1) for b in range(FFN_TILE.bit_length() - 1)]
            for size in sizes:
                @pl.when((npad & size) != 0)
                def _():
                    chunk(size).start()
            for size in sizes:
                @pl.when((npad & size) != 0)
                def _():
                    chunk(size).wait()
            return carry

        lax.fori_loop(0, N_EXPERTS, pad_expert, 0)

        tile_rows = FFN_TILE * SUBLANES
        n_tiles = xs_ref.shape[0] // tile_rows

        def tcopy(j):
            dst = xs_ref.at[pl.ds(pl.multiple_of(j * tile_rows, tile_rows), tile_rows), :]
            return pltpu.make_async_copy(zbuf, dst, zsem)

        def tstart(j, c2):
            tcopy(j).start()
            return c2

        def twait(j, c2):
            tcopy(j).wait()
            return c2

        lax.fori_loop(nv_ref[0], n_tiles, tstart, 0)
        lax.fori_loop(nv_ref[0], n_tiles, twait, 0)


def _dispatch(pos_flat, counts, offs, n_valid, x1r, n_rows):
    T = x1r.shape[0] // SUBLANES
    G = min(DISPATCH_GROUP, T)
    return pl.pallas_call(
        functools.partial(_dispatch_kernel, n_groups=T // G),
        out_shape=jax.ShapeDtypeStruct((n_rows * SUBLANES, LANES), F32),
        grid_spec=pltpu.PrefetchScalarGridSpec(
            num_scalar_prefetch=4,
            grid=(T // G,),
            in_specs=[pl.BlockSpec((G * SUBLANES, LANES), lambda i, *_: (i, 0))],
            out_specs=pl.BlockSpec(memory_space=pl.ANY),
            scratch_shapes=[pltpu.VMEM((2, G * SUBLANES, LANES), F32),
                            pltpu.VMEM((FFN_TILE * SUBLANES, LANES), F32),
                            pltpu.SemaphoreType.DMA((2,)),
                            pltpu.SemaphoreType.DMA(())]),
        compiler_params=_params(("arbitrary",)),
        name="dispatch_rows",
    )(pos_flat, counts, offs, n_valid, x1r)


def _ffn_kernel(te_ref, nv_ref, grp_ref, nxt_ref, xs_ref, wgu_hbm, wd_hbm, bg_ref, bl_ref, bd_ref, y_ref,
                xb_ref, wg_s, wl_s, wd_s, wgu_buf, wd_buf, wsems, *, layer):
    tm = FFN_TILE
    j = pl.program_id(0)
    valid = j < nv_ref[0]
    first = jnp.logical_or(j == 0, te_ref[j] != te_ref[jnp.maximum(j - 1, 0)])
    slot = grp_ref[j] % 2

    def weight_copies(e, s):
        return (pltpu.make_async_copy(wgu_hbm.at[layer, e], wgu_buf.at[s], wsems.at[0, s]),
                pltpu.make_async_copy(wd_hbm.at[layer, e], wd_buf.at[s], wsems.at[1, s]))

    @pl.when(j == 0)
    def _():
        for cp in weight_copies(te_ref[0], 0):
            cp.start()

    @pl.when(jnp.logical_and(valid, first))
    def _():
        for cp in weight_copies(te_ref[j], slot):
            cp.wait()

        @pl.when(nxt_ref[j] >= 0)
        def _():
            for cp in weight_copies(nxt_ref[j], 1 - slot):
                cp.start()

        pi = lax.broadcasted_iota(I32, (256, 256), 0)
        pj = lax.broadcasted_iota(I32, (256, 256), 1)
        src = jnp.where(pj < 128, 2 * pj, 2 * (pj - 128) + 1)
        perm = jnp.where(pi == src, 1.0, 0.0).astype(BF16)
        for blk in range(8):
            r = _dot(wgu_buf[slot, :, blk * 256:(blk + 1) * 256].astype(BF16), perm)
            wg_s[:, blk * 128:(blk + 1) * 128] = r[:, 0:128].astype(BF16)
            wl_s[:, blk * 128:(blk + 1) * 128] = r[:, 128:256].astype(BF16)
        wd_s[...] = wd_buf[slot].astype(BF16)

    @pl.when(valid)
    def _():
        for c in range(SUBLANES):
            xb_ref[:, c * LANES:(c + 1) * LANES] = xs_ref[pl.ds(c, tm, stride=SUBLANES), :].astype(BF16)
        xb = xb_ref[...]
        hg = _dot(xb, wg_s[...]) + bg_ref[...]
        hl = _dot(xb, wl_s[...]) + bl_ref[...]
        glu = jnp.minimum(hg, SWIGLU_LIMIT)
        lin = jnp.clip(hl, -SWIGLU_LIMIT, SWIGLU_LIMIT)
        act = glu * _sigmoid(SWIGLU_ALPHA * glu) * (lin + 1.0)
        y = _dot(act.astype(BF16), wd_s[...]) + bd_ref[...]
        for c in range(SUBLANES):
            y_ref[pl.ds(c, tm, stride=SUBLANES), :] = y[:, c * LANES:(c + 1) * LANES]

    @pl.when(jnp.logical_not(valid))
    def _():
        y_ref[...] = jnp.zeros_like(y_ref)


def _ffn(tile_expert, n_valid, tile_group, next_expert, xs, w_gate_up, w_down, bg, bl, b_down, layer, n_tiles):
    tm = FFN_TILE
    D = w_down.shape[-1]
    tile = lambda j, te, nv, grp, nxt: (jnp.minimum(j, nv[0] - 1), 0)
    out_tile = lambda j, te, nv, grp, nxt: (j, 0)
    bias = lambda j, te, nv, grp, nxt: (layer, te[j], 0, 0)
    return pl.pallas_call(
        functools.partial(_ffn_kernel, layer=layer),
        out_shape=jax.ShapeDtypeStruct(xs.shape, F32),
        grid_spec=pltpu.PrefetchScalarGridSpec(
            num_scalar_prefetch=4,
            grid=(n_tiles,),
            in_specs=[pl.BlockSpec((tm * SUBLANES, LANES), tile),
                      pl.BlockSpec(memory_space=pl.ANY),
                      pl.BlockSpec(memory_space=pl.ANY),
                      pl.BlockSpec((None, None, 1, D), bias),
                      pl.BlockSpec((None, None, 1, D), bias),
                      pl.BlockSpec((None, None, 1, D), bias)],
            out_specs=pl.BlockSpec((tm * SUBLANES, LANES), out_tile),
            scratch_shapes=[pltpu.VMEM((tm, D), BF16),
                            pltpu.VMEM((D, D), BF16),
                            pltpu.VMEM((D, D), BF16),
                            pltpu.VMEM((D, D), BF16),
                            pltpu.VMEM((2, D, 2 * D), F32),
                            pltpu.VMEM((2, D, D), F32),
                            pltpu.SemaphoreType.DMA((2, 2))]),
        compiler_params=_params(("arbitrary",)),
        name="expert_ffn",
    )(tile_expert, n_valid, tile_group, next_expert, xs, w_gate_up, w_down, bg, bl, b_down)


def _combine_kernel(pos_ref, y_ref, gate_ref, x1_ref, g_ref, b_ref, o_ref, ybuf0, ybuf1, zs, sems):
    tc = COMBINE_TILE
    i = pl.program_id(0)
    n = pl.num_programs(0)
    bufs = (ybuf0, ybuf1)
    rows = TOP_K * tc * SUBLANES
    per_chunk = tc // SUBLANES

    def row_copy(tile_base, s, tt, k):
        p = pos_ref[tile_base + tt * TOP_K + k]
        src = y_ref.at[pl.ds(pl.multiple_of(p * SUBLANES, SUBLANES), SUBLANES), :]
        dst = bufs[s].at[pl.ds(pl.multiple_of((k * tc + tt) * SUBLANES, SUBLANES), SUBLANES), :]
        return pltpu.make_async_copy(src, dst, sems.at[s])

    def issue_loop(tile_base, s):
        def tok(tt, carry):
            for k in range(TOP_K):
                row_copy(tile_base, s, tt, k).start(priority=k % 2)
            return carry

        lax.fori_loop(0, tc, tok, 0)

    def slot_wait(s):
        pltpu.make_async_copy(y_ref.at[pl.ds(0, rows), :], bufs[s], sems.at[s]).wait()

    def step(s):
        slot_wait(s)
        next_base = jnp.minimum(i + 1, n - 1) * (tc * TOP_K)
        gate = gate_ref[...]
        s1 = jnp.zeros((tc, 1), F32)
        for c in range(SUBLANES):
            for tt in range(c * per_chunk, (c + 1) * per_chunk):
                for k in range(TOP_K):
                    row_copy(next_base, 1 - s, tt, k).start(priority=k % 2)
            z = ALPHA * x1_ref[:, c * LANES:(c + 1) * LANES]
            for k in range(TOP_K):
                z = z + gate[:, k:k + 1] * bufs[s][pl.ds(k * tc * SUBLANES + c, tc, stride=SUBLANES), :]
            zs[:, c * LANES:(c + 1) * LANES] = z
            s1 = s1 + jnp.sum(z, axis=1, keepdims=True)
        zv = zs[...]
        mu = s1 / zv.shape[1]
        zc = zv - mu
        var = jnp.mean(zc * zc, axis=1, keepdims=True)
        o_ref[...] = zc * lax.rsqrt(var + LN_EPS) * g_ref[...] + b_ref[...]

        @pl.when(i == n - 1)
        def _():
            slot_wait(1 - s)

    @pl.when(i == 0)
    def _():
        issue_loop(0, 0)

    @pl.when(i % 2 == 0)
    def _():
        step(0)

    @pl.when(i % 2 == 1)
    def _():
        step(1)


def _combine(pos_flat, y, gate, x1, ln_g, ln_b, layer):
    T, D = x1.shape
    tc = COMBINE_TILE
    row = lambda i, pos: (i, 0)
    return pl.pallas_call(
        _combine_kernel,
        out_shape=jax.ShapeDtypeStruct((T, D), F32),
        grid_spec=pltpu.PrefetchScalarGridSpec(
            num_scalar_prefetch=1,
            grid=(T // tc,),
            in_specs=[pl.BlockSpec(memory_space=pl.ANY),
                      pl.BlockSpec((tc, LANES), row),
                      pl.BlockSpec((tc, D), row),
                      pl.BlockSpec((None, 1, D), lambda i, pos: (layer, 0, 0)),
                      pl.BlockSpec((None, 1, D), lambda i, pos: (layer, 0, 0))],
            out_specs=pl.BlockSpec((tc, D), row),
            scratch_shapes=[pltpu.VMEM((TOP_K * tc * SUBLANES, LANES), F32),
                            pltpu.VMEM((TOP_K * tc * SUBLANES, LANES), F32),
                            pltpu.VMEM((tc, D), F32),
                            pltpu.SemaphoreType.DMA((2,))]),
        compiler_params=_params(("arbitrary",)),
        name="combine_ln",
    )(pos_flat, y, gate, x1, ln_g.reshape(-1, 1, D), ln_b.reshape(-1, 1, D))


def _moe(x1, x1r, eid, rank, gate, cnt, layer, w_gate_up, bg, bl, w_down, b_down, ln_g, ln_b):
    T, D = x1.shape
    tm = FFN_TILE
    n_tiles = (T * TOP_K) // tm + N_EXPERTS
    n_rows = n_tiles * tm
    counts = cnt[0, :N_EXPERTS].astype(I32)
    tiles_e = (counts + tm - 1) // tm
    tile_end = jnp.cumsum(tiles_e)
    offs = (tile_end - tiles_e) * tm
    n_valid = tile_end[-1]
    jj = jnp.minimum(jnp.arange(n_tiles, dtype=I32), n_valid - 1)
    tile_expert = jnp.minimum(jnp.sum(tile_end[None, :] <= jj[:, None], axis=1), N_EXPERTS - 1).astype(I32)
    off_row = jnp.pad(offs, (0, LANES - N_EXPERTS)).astype(F32).reshape(1, LANES)
    present = tiles_e > 0
    group_of_e = jnp.cumsum(present.astype(I32)) - 1
    eidx = jnp.arange(N_EXPERTS, dtype=I32)
    later = jnp.where(present[None, :] & (eidx[None, :] > eidx[:, None]), eidx[None, :], N_EXPERTS)
    next_e = jnp.min(later, axis=1)
    next_e = jnp.where(next_e == N_EXPERTS, -1, next_e).astype(I32)
    mine = tile_expert[:, None] == eidx[None, :]
    tile_group = jnp.sum(jnp.where(mine, group_of_e[None, :], 0), axis=1).astype(I32)
    next_expert = jnp.sum(jnp.where(mine, next_e[None, :], 0), axis=1).astype(I32)

    pos = _positions(eid, rank, off_row)
    pos_flat = pos[:, :TOP_K].reshape(-1)
    nv = n_valid.reshape(1).astype(I32)
    xs = _dispatch(pos_flat, counts, offs.astype(I32), nv, x1r, n_rows)
    y = _ffn(tile_expert, nv, tile_group, next_expert, xs, w_gate_up, w_down, bg, bl,
             b_down.reshape(DEPTH, N_EXPERTS, 1, D), layer, n_tiles)
    return _combine(pos_flat, y, gate, x1, ln_g, ln_b, layer)


def kernel(x, even_w_in, pool_w, pool_scale, conv_w, conv_b, i_bias, f_bias, ml_norm, even_w_out, odd_w_in, gla_w2, gla_b, gla_norm, odd_w_out, ln1_g, ln1_b, ln2_g, ln2_b, router_w, router_b, w_gate_up, b_gate_up, w_down, b_down):
    B, S, D = x.shape
    xt = x.reshape(B * S, D)
    bgu = b_gate_up.reshape(DEPTH, N_EXPERTS, D, 2)
    bg = bgu[..., 0].reshape(DEPTH, N_EXPERTS, 1, D)
    bl = bgu[..., 1].reshape(DEPTH, N_EXPERTS, 1, D)
    for layer in range(DEPTH):
        li = layer // 2
        if layer % 2 == 0:
            p = _project(xt, even_w_in, li, 2560)
            mix = _even_mixer(p, xt, B, S, even_w_in, li, pool_w, pool_scale, conv_w, conv_b,
                              i_bias, f_bias, ml_norm)
            w_out = even_w_out
        else:
            p = _project(xt, odd_w_in, li, 3072)
            mix = _odd_mixer(p, xt, B, S, odd_w_in, li, gla_w2, gla_b, gla_norm)
            w_out = odd_w_out
        x1, x1r, eid, rank, gate, cnt = _post(mix, xt, w_out, li, ln1_g, ln1_b, router_w, router_b, layer)
        xt = _moe(x1, x1r, eid, rank, gate, cnt, layer, w_gate_up, bg, bl, w_down, b_down,
                  ln2_g, ln2_b)
    return xt.reshape(B, S, D)
```
